```python
import jax
import jax.numpy as jnp
from jax import lax
import numpy as np

D_MODEL = 1024
BATCH = 32
SEQ = 2048
DEPTH = 2

N_EVEN = (DEPTH + 1) // 2
N_ODD = DEPTH // 2
RMS_EPS = 1e-6

MLA_HEADS = 8
MLA_NOPE = 64
MLA_ROPE = 32
MLA_V = 64
Q_LORA = 384
KV_LORA = 256
ROPE_BASE = 10000.0
Q_BLOCK = 128
MLA_OUT = MLA_HEADS * MLA_V
MLA_COLS = Q_LORA + KV_LORA + MLA_ROPE

RWKV_HEAD = 64
RWKV_DIM = D_MODEL // 2
RWKV_HEADS = RWKV_DIM // RWKV_HEAD
DECAY_LORA = 64
AAA_LORA = 64
GATE_LORA = 128
RWKV_GN_EPS = RWKV_HEAD * 1e-5
RWKV_COLS = 3 * RWKV_DIM + DECAY_LORA + AAA_LORA + GATE_LORA
IN_EVEN = MLA_COLS + RWKV_COLS
MIX_EVEN = MLA_OUT + RWKV_DIM

HG_K = 128
HG_HEADS = D_MODEL // HG_K
HG_V = D_MODEL // HG_HEADS
HG_QK_DIM = HG_HEADS * HG_K
HG_V_DIM = HG_HEADS * HG_V
HG_CHUNK = 32
IN_ODD = 2 * HG_QK_DIM + 2 * HG_V_DIM

FFN_HIDDEN = -(-8 * D_MODEL // (3 * 256)) * 256

kernel_name = 'hybrid_mla_rwkv7_hgrn2_adaln'


def rms_norm(x, gain, eps=RMS_EPS):
    xf = x.astype(jnp.float32)
    y = xf * lax.rsqrt(jnp.mean(xf * xf, axis=-1, keepdims=True) + eps)
    return (y * gain.astype(jnp.float32)).astype(x.dtype)


def modulate(h, shift, scale):
    return h * (1 + scale[:, None, :]) + shift[:, None, :]


def rope(x, cos, sin):
    x1, x2 = jnp.split(x, 2, axis=-1)
    return jnp.concatenate([x1 * cos - x2 * sin, x1 * sin + x2 * cos], axis=-1)


def mla_mix(p, positions, q_norm, w_uq, kv_norm, w_ukv):
    B, S, _ = p.shape
    c_q, c_kv, k_rope = jnp.split(p, [Q_LORA, Q_LORA + KV_LORA], axis=-1)
    q = (rms_norm(c_q, q_norm) @ w_uq).reshape(B, S, MLA_HEADS, MLA_NOPE + MLA_ROPE)
    q_nope, q_rope = q[..., :MLA_NOPE], q[..., MLA_NOPE:]
    kv = (rms_norm(c_kv, kv_norm) @ w_ukv).reshape(B, S, MLA_HEADS, MLA_NOPE + MLA_V)
    k_nope, v = kv[..., :MLA_NOPE], kv[..., MLA_NOPE:]
    inv_freq = 1.0 / (ROPE_BASE ** (jnp.arange(0, MLA_ROPE, 2, dtype=jnp.float32) / MLA_ROPE))
    ang = positions.astype(jnp.float32)[..., None] * inv_freq
    cos, sin = jnp.cos(ang).astype(p.dtype), jnp.sin(ang).astype(p.dtype)
    q_rope = rope(q_rope, cos[:, :, None, :], sin[:, :, None, :])
    k_rope = rope(k_rope, cos, sin)
    nb = S // Q_BLOCK
    qn_b = q_nope.reshape(B, nb, Q_BLOCK, MLA_HEADS, MLA_NOPE).transpose(1, 0, 2, 3, 4)
    qr_b = q_rope.reshape(B, nb, Q_BLOCK, MLA_HEADS, MLA_ROPE).transpose(1, 0, 2, 3, 4)
    scale = (MLA_NOPE + MLA_ROPE) ** -0.5
    kpos = jnp.arange(S)

    def block(args):
        i, qn, qr = args
        s = jnp.einsum('bqhd,bkhd->bhqk', qn, k_nope) + jnp.einsum('bqhr,bkr->bhqk', qr, k_rope)
        s = s.astype(jnp.float32) * scale
        qpos = i * Q_BLOCK + jnp.arange(Q_BLOCK)
        s = jnp.where(kpos[None, :] <= qpos[:, None], s, -jnp.inf)
        pr = jax.nn.softmax(s, axis=-1).astype(v.dtype)
        return jnp.einsum('bhqk,bkhd->bqhd', pr, v)

    o = lax.map(block, (jnp.arange(nb), qn_b, qr_b))
    return o.transpose(1, 0, 2, 3, 4).reshape(B, S, MLA_OUT)


def token_shift(p):
    return jnp.pad(p, ((0, 0), (1, 0), (0, 0)))[:, :-1]


def rwkv7_mix(p, mu, w0, w2, a0, a2, g2, k_k, k_a, r_k, ln_w, ln_b):
    B, S, _ = p.shape
    f32 = jnp.float32
    p = p + (token_shift(p) - p) * mu
    r, k, v, w_lo, a_lo, g_lo = jnp.split(
        p, [RWKV_DIM, 2 * RWKV_DIM, 3 * RWKV_DIM, 3 * RWKV_DIM + DECAY_LORA,
            3 * RWKV_DIM + DECAY_LORA + AAA_LORA], axis=-1)
    w_log = -jax.nn.softplus(-(w0 + jnp.tanh(w_lo) @ w2)) - 0.5
    decay = jnp.exp(-jnp.exp(w_log.astype(f32)))
    a = jax.nn.sigmoid(a0 + a_lo @ a2)
    g = jax.nn.sigmoid(g_lo) @ g2

    def heads(t):
        return t.reshape(B, S, RWKV_HEADS, RWKV_HEAD).astype(f32)

    kk = heads(k * k_k)
    kk = kk / jnp.maximum(jnp.linalg.norm(kk, axis=-1, keepdims=True), 1e-12)
    k = k * (1 + (a - 1) * k_a)
    r_h, k_h, v_h, w_h, a_h = heads(r), heads(k), heads(v), heads(decay), heads(a)

    def step(state, inp):
        r_t, w_t, k_t, v_t, kk_t, a_t = inp
        sa = jnp.einsum('bhvk,bhk->bhv', state, -kk_t)
        state = (state * w_t[:, :, None, :] + sa[..., None] * (kk_t * a_t)[:, :, None, :]
                 + v_t[..., None] * k_t[:, :, None, :])
        return state, jnp.einsum('bhvk,bhk->bhv', state, r_t)

    xs = tuple(t.transpose(1, 0, 2, 3) for t in (r_h, w_h, k_h, v_h, kk, a_h))
    state0 = jnp.zeros((B, RWKV_HEADS, RWKV_HEAD, RWKV_HEAD), f32)
    _, y = lax.scan(step, state0, xs)
    y = y.transpose(1, 0, 2, 3)
    mean = jnp.mean(y, axis=-1, keepdims=True)
    var = jnp.mean(jnp.square(y - mean), axis=-1, keepdims=True)
    y = (y - mean) * lax.rsqrt(var + RWKV_GN_EPS)
    y = y * ln_w.reshape(RWKV_HEADS, RWKV_HEAD) + ln_b.reshape(RWKV_HEADS, RWKV_HEAD)
    bonus = jnp.sum(r_h * k_h * r_k, axis=-1, keepdims=True) * v_h
    y = (y + bonus).reshape(B, S, RWKV_DIM) * g
    return y.astype(p.dtype)


def hgrn2_mix(p, lb, out_norm):
    B, S, _ = p.shape
    f32 = jnp.float32
    q, f, i, g = jnp.split(p, [HG_QK_DIM, 2 * HG_QK_DIM, 2 * HG_QK_DIM + HG_V_DIM], axis=-1)
    q = jax.nn.silu(q.astype(f32))
    forget = lb + (1 - lb) * jax.nn.sigmoid(f.astype(f32))
    key = 1 - forget
    logf = jnp.log(forget)
    nc = S // HG_CHUNK

    def chunks(t, d):
        return t.reshape(B, nc, HG_CHUNK, HG_HEADS, d).transpose(1, 0, 3, 2, 4)

    causal = jnp.tril(jnp.ones((HG_CHUNK, HG_CHUNK), bool))

    def step(state, inp):
        q_c, k_c, g_c, v_c = inp
        b = jnp.cumsum(g_c, axis=2)
        o_inter = jnp.einsum('bhck,bhkv->bhcv', q_c * jnp.exp(b), state)
        diff = b[:, :, :, None, :] - b[:, :, None, :, :]
        dec = jnp.exp(jnp.where(causal[:, :, None], diff, -jnp.inf))
        attn = jnp.einsum('bhtk,bhtsk,bhsk->bhts', q_c, dec, k_c)
        o = o_inter + jnp.einsum('bhts,bhsv->bhtv', attn, v_c)
        b_last = b[:, :, -1:, :]
        state = (state * jnp.exp(b_last[:, :, 0, :, None])
                 + jnp.einsum('bhsk,bhsv->bhkv', k_c * jnp.exp(b_last - b), v_c))
        return state, o

    xs = (chunks(q, HG_K), chunks(key, HG_K), chunks(logf, HG_K), chunks(i.astype(f32), HG_V))
    state0 = jnp.zeros((B, HG_HEADS, HG_K, HG_V), f32)
    _, o = lax.scan(step, state0, xs)
    o = o.transpose(1, 0, 3, 2, 4).reshape(B, S, HG_HEADS, HG_V)
    o = rms_norm(o, out_norm).reshape(B, S, HG_V_DIM) * jax.nn.silu(g.astype(f32))
    return o.astype(p.dtype)


def swiglu(h, w_gate, w_up, w_down):
    return (jax.nn.silu(h @ w_gate) * (h @ w_up)) @ w_down


def setup_inputs(seed: int = 0) -> dict:
    key = jax.random.key(seed)
    ks = iter(jax.random.split(key, 40))
    D = D_MODEL

    def nrm(shape, scale):
        return jax.random.normal(next(ks), shape, jnp.float32) * scale

    def gain(shape):
        return 1.0 + nrm(shape, 0.02)

    x = nrm((BATCH, SEQ, D), 1.0)
    c = nrm((BATCH, D), 1.0)
    offset = jax.random.randint(next(ks), (BATCH, 1), 0, 4096, dtype=jnp.int32)
    positions = offset + jnp.arange(SEQ, dtype=jnp.int32)[None, :]
    return {
        'x': x,
        'c': c,
        'positions': positions,
        'ada_w': nrm((DEPTH, D, 6 * D), 0.5 * D ** -0.5),
        'ada_b': nrm((DEPTH, 6 * D), 0.02),
        'norm_mix': gain((DEPTH, D)),
        'norm_ffn': gain((DEPTH, D)),
        'w_in_even': nrm((N_EVEN, D, IN_EVEN), D ** -0.5),
        'mla_q_norm': gain((N_EVEN, Q_LORA)),
        'mla_w_uq': nrm((N_EVEN, Q_LORA, MLA_HEADS * (MLA_NOPE + MLA_ROPE)), Q_LORA ** -0.5),
        'mla_kv_norm': gain((N_EVEN, KV_LORA)),
        'mla_w_ukv': nrm((N_EVEN, KV_LORA, MLA_HEADS * (MLA_NOPE + MLA_V)), KV_LORA ** -0.5),
        'rwkv_mu': jax.random.uniform(next(ks), (N_EVEN, RWKV_COLS), jnp.float32),
        'rwkv_w0': jax.random.uniform(next(ks), (N_EVEN, RWKV_DIM), jnp.float32, -6.0, -1.0),
        'rwkv_w2': nrm((N_EVEN, DECAY_LORA, RWKV_DIM), 0.5 * DECAY_LORA ** -0.5),
        'rwkv_a0': nrm((N_EVEN, RWKV_DIM), 0.1),
        'rwkv_a2': nrm((N_EVEN, AAA_LORA, RWKV_DIM), 0.5 * AAA_LORA ** -0.5),
        'rwkv_g2': nrm((N_EVEN, GATE_LORA, RWKV_DIM), GATE_LORA ** -0.5),
        'rwkv_k_k': 0.85 + nrm((N_EVEN, RWKV_DIM), 0.05),
        'rwkv_k_a': 1.0 + nrm((N_EVEN, RWKV_DIM), 0.05),
        'rwkv_r_k': nrm((N_EVEN, RWKV_HEADS, RWKV_HEAD), 0.1),
        'rwkv_ln_w': gain((N_EVEN, RWKV_DIM)),
        'rwkv_ln_b': nrm((N_EVEN, RWKV_DIM), 0.02),
        'w_out_even': nrm((N_EVEN, MIX_EVEN, D), MIX_EVEN ** -0.5),
        'w_in_odd': nrm((N_ODD, D, IN_ODD), D ** -0.5),
        'hg_lb_logits': nrm((DEPTH, HG_QK_DIM), 0.5),
        'hg_out_norm': gain((N_ODD, HG_V)),
        'w_out_odd': nrm((N_ODD, HG_V_DIM, D), HG_V_DIM ** -0.5),
        'ffn_w_gate': nrm((DEPTH, D, FFN_HIDDEN), D ** -0.5),
        'ffn_w_up': nrm((DEPTH, D, FFN_HIDDEN), D ** -0.5),
        'ffn_w_down': nrm((DEPTH, FFN_HIDDEN, D), FFN_HIDDEN ** -0.5),
        'final_norm': gain((D,)),
    }


def reference(x, c, positions, ada_w, ada_b, norm_mix, norm_ffn, w_in_even, mla_q_norm, mla_w_uq,
              mla_kv_norm, mla_w_ukv, rwkv_mu, rwkv_w0, rwkv_w2, rwkv_a0, rwkv_a2, rwkv_g2, rwkv_k_k,
              rwkv_k_a, rwkv_r_k, rwkv_ln_w, rwkv_ln_b, w_out_even, w_in_odd, hg_lb_logits, hg_out_norm,
              w_out_odd, ffn_w_gate, ffn_w_up, ffn_w_down, final_norm):
    cond = jax.nn.silu(c)
    lb_p = jax.nn.softmax(hg_lb_logits.astype(jnp.float32), axis=0)
    lb_all = jnp.cumsum(lb_p, axis=0) - lb_p[0]
    for l in range(DEPTH):
        mod = cond @ ada_w[l] + ada_b[l]
        sh_m, sc_m, g_m, sh_f, sc_f, g_f = jnp.split(mod, 6, axis=-1)
        h = modulate(rms_norm(x, norm_mix[l]), sh_m, sc_m)
        j = l // 2
        if l % 2 == 0:
            proj = h @ w_in_even[j]
            y_a = mla_mix(proj[..., :MLA_COLS], positions, mla_q_norm[j], mla_w_uq[j],
                          mla_kv_norm[j], mla_w_ukv[j])
            y_b = rwkv7_mix(proj[..., MLA_COLS:], rwkv_mu[j], rwkv_w0[j], rwkv_w2[j], rwkv_a0[j],
                            rwkv_a2[j], rwkv_g2[j], rwkv_k_k[j], rwkv_k_a[j], rwkv_r_k[j],
                            rwkv_ln_w[j], rwkv_ln_b[j])
            y = jnp.concatenate([y_a, y_b], axis=-1) @ w_out_even[j]
        else:
            y = hgrn2_mix(h @ w_in_odd[j], lb_all[l], hg_out_norm[j]) @ w_out_odd[j]
        x = x + g_m[:, None, :] * y
        h = modulate(rms_norm(x, norm_ffn[l]), sh_f, sc_f)
        x = x + g_f[:, None, :] * swiglu(h, ffn_w_gate[l], ffn_w_up[l], ffn_w_down[l])
    return rms_norm(x, final_norm)
```

```python
import functools

import numpy as np
import jax
import jax.numpy as jnp
from jax import lax
from jax.experimental import pallas as pl
from jax.experimental.pallas import tpu as pltpu

F32 = jnp.float32
BF16 = jnp.bfloat16

RMS_EPS = 1e-6
LANES = 128
VMEM_LIMIT = 56 * 1024 * 1024

MLA_HEADS = 8
MLA_NOPE = 64
MLA_ROPE = 32
MLA_V = 64
Q_LORA = 384
KV_LORA = 256
ROPE_BASE = 10000.0
MLA_COLS = Q_LORA + KV_LORA + MLA_ROPE
MLA_COLS_PAD = 768
MLA_HEAD_PAD = 128

RWKV_HEAD = 64
RWKV_DIM = 512
RWKV_HEADS = 8
DECAY_LORA = 64
AAA_LORA = 64
GATE_LORA = 128
RWKV_GN_EPS = RWKV_HEAD * 1e-5
RWKV_COLS = 3 * RWKV_DIM + DECAY_LORA + AAA_LORA + GATE_LORA
RWKV_CHUNK = 64
RWKV_GROUP = 4

HG_K = 128
HG_HEADS = 8
HG_V = 128
HG_CHUNK = 128
HG_LEVELS = 7


def _params(*sem):
    return pltpu.CompilerParams(dimension_semantics=sem, vmem_limit_bytes=VMEM_LIMIT)


def _const_spec(shape):
    nd = len(shape)
    return pl.BlockSpec(shape, lambda *_: (0,) * nd, pipeline_mode=pl.Buffered(1))


def _dot(a, b):
    return jnp.dot(a, b, preferred_element_type=F32)


def _dot_nt(a, b):
    return lax.dot_general(a, b, (((1,), (1,)), ((), ())), preferred_element_type=F32)


def _dot_tn(a, b):
    return lax.dot_general(a, b, (((0,), (0,)), ((), ())), preferred_element_type=F32)


def _split3(x):
    hi = x.astype(BF16)
    r1 = x - hi.astype(F32)
    mid = r1.astype(BF16)
    lo = (r1 - mid.astype(F32)).astype(BF16)
    return hi, mid, lo


def _dot3(sel, x):
    hi, mid, lo = _split3(x)
    return _dot(sel, hi) + _dot(sel, mid) + _dot(sel, lo)


def _dot3_r(x, sel):
    hi, mid, lo = _split3(x)
    return _dot(hi, sel) + _dot(mid, sel) + _dot(lo, sel)


def _ada_kernel(c_ref, w_ref, b_ref, o_ref):
    c = c_ref[...]
    cond = c * jax.nn.sigmoid(c)
    o_ref[0] = _dot(cond.astype(BF16), w_ref[0].astype(BF16)) + b_ref[0]


def _ada_mod(c, ada_w, ada_b):
    depth, d, n = ada_w.shape
    b = c.shape[0]
    tn = n // 4
    return pl.pallas_call(
        _ada_kernel,
        grid=(depth, n // tn),
        in_specs=[
            pl.BlockSpec((b, d), lambda l, j: (0, 0)),
            pl.BlockSpec((1, d, tn), lambda l, j: (l, 0, j)),
            pl.BlockSpec((1, 1, tn), lambda l, j: (l, 0, j)),
        ],
        out_specs=pl.BlockSpec((1, b, tn), lambda l, j: (l, 0, j)),
        out_shape=jax.ShapeDtypeStruct((depth, b, n), F32),
        compiler_params=_params("arbitrary", "arbitrary"),
        name="ada_mod",
    )(c, ada_w, ada_b.reshape(depth, 1, n))


def _norm_mod(x, gain, mod, sh, sc):
    ms = jnp.mean(x * x, axis=-1, keepdims=True)
    y = x * lax.rsqrt(ms + RMS_EPS) * gain
    return y * (1.0 + mod[sc:sc + 1]) + mod[sh:sh + 1]


def _in_proj_kernel(x_ref, mod_ref, gain_ref, *refs, n_out, sh, sc):
    w_refs, o_refs = refs[:n_out], refs[n_out:]
    h = _norm_mod(x_ref[...], gain_ref[...], mod_ref[0], sh, sc).astype(BF16)
    for w_ref, o_ref in zip(w_refs, o_refs):
        o_ref[...] = _dot(h, w_ref[...]).astype(o_ref.dtype)


def _in_proj(x, mod, gain, weights, *, seq, sh, sc, tm):
    t, d = x.shape
    per_b = seq // tm
    n_out = len(weights)
    in_specs = [
        pl.BlockSpec((tm, d), lambda i: (i, 0)),
        pl.BlockSpec((1, 6, d), lambda i: (i // per_b, 0, 0)),
        _const_spec((1, d)),
    ] + [_const_spec(w.shape) for w in weights]
    out_specs = [pl.BlockSpec((tm, w.shape[1]), lambda i: (i, 0)) for w in weights]
    out_shape = [jax.ShapeDtypeStruct((t, w.shape[1]), F32) for w in weights]
    return pl.pallas_call(
        functools.partial(_in_proj_kernel, n_out=n_out, sh=sh, sc=sc),
        grid=(t // tm,),
        in_specs=in_specs,
        out_specs=out_specs,
        out_shape=out_shape,
        compiler_params=_params("parallel"),
        name="in_proj",
    )(x, mod, gain.reshape(1, d), *weights)


def _out_proj_kernel(x_ref, mod_ref, *refs, n_in, gate):
    y_refs, w_refs, o_ref = refs[:n_in], refs[n_in:2 * n_in], refs[2 * n_in]
    acc = _dot(y_refs[0][...].astype(BF16), w_refs[0][...])
    for y_ref, w_ref in zip(y_refs[1:], w_refs[1:]):
        acc += _dot(y_ref[...].astype(BF16), w_ref[...])
    g = mod_ref[0][gate:gate + 1]
    o_ref[...] = x_ref[...] + g * acc


def _out_proj(x, mod, ys, weights, *, seq, gate, tm):
    t, d = x.shape
    per_b = seq // tm
    n_in = len(ys)
    in_specs = [
        pl.BlockSpec((tm, d), lambda i: (i, 0)),
        pl.BlockSpec((1, 6, d), lambda i: (i // per_b, 0, 0)),
    ] + [pl.BlockSpec((tm, y.shape[1]), lambda i: (i, 0)) for y in ys] \
      + [_const_spec(w.shape) for w in weights]
    return pl.pallas_call(
        functools.partial(_out_proj_kernel, n_in=n_in, gate=gate),
        grid=(t // tm,),
        in_specs=in_specs,
        out_specs=pl.BlockSpec((tm, d), lambda i: (i, 0)),
        out_shape=jax.ShapeDtypeStruct((t, d), F32),
        compiler_params=_params("parallel"),
        name="out_proj",
    )(x, mod, *ys, *weights)


def _ffn_kernel(x_ref, mod_ref, gain_ref, wg_ref, wu_ref, wd_ref, fin_ref, o_ref, *, final):
    x = x_ref[...]
    mod = mod_ref[0]
    h = _norm_mod(x, gain_ref[...], mod, 3, 4).astype(BF16)
    gate = _dot(h, wg_ref[...])
    up = _dot(h, wu_ref[...])
    act = (gate * jax.nn.sigmoid(gate) * up).astype(BF16)
    y = x + mod[5:6] * _dot(act, wd_ref[...])
    if final:
        ms = jnp.mean(y * y, axis=-1, keepdims=True)
        y = y * lax.rsqrt(ms + RMS_EPS) * fin_ref[...]
    o_ref[...] = y


def _ffn(x, mod, gain, wg, wu, wd, fin, *, seq, tm, final):
    t, d = x.shape
    per_b = seq // tm
    return pl.pallas_call(
        functools.partial(_ffn_kernel, final=final),
        grid=(t // tm,),
        in_specs=[
            pl.BlockSpec((tm, d), lambda i: (i, 0)),
            pl.BlockSpec((1, 6, d), lambda i: (i // per_b, 0, 0)),
            _const_spec((1, d)),
            _const_spec(wg.shape),
            _const_spec(wu.shape),
            _const_spec(wd.shape),
            _const_spec((1, d)),
        ],
        out_specs=pl.BlockSpec((tm, d), lambda i: (i, 0)),
        out_shape=jax.ShapeDtypeStruct((t, d), F32),
        compiler_params=_params("parallel"),
        name="ffn",
    )(x, mod, gain.reshape(1, d), wg, wu, wd, fin.reshape(1, d))


def _rope_partner(x, lane):
    n = x.shape[-1]
    half = MLA_ROPE // 2
    fwd = pltpu.roll(x, half, 1)
    bwd = pltpu.roll(x, n - half, 1)
    off = lane % MLA_HEAD_PAD
    is_x1 = (off >= MLA_NOPE) & (off < MLA_NOPE + half)
    return jnp.where(is_x1, -bwd, fwd)


def _mla_prep_kernel(p_ref, pos_ref, invf_ref, qn_ref, kvn_ref, wq_ref, wkv_ref,
                     q_ref, k_ref, v_ref, *, scale):
    p = p_ref[...]
    tm = p.shape[0]
    n_q = MLA_HEADS * MLA_HEAD_PAD
    c_q = p[:, :Q_LORA]
    ms = jnp.mean(c_q * c_q, axis=-1, keepdims=True)
    cqn = (c_q * lax.rsqrt(ms + RMS_EPS) * qn_ref[...]).astype(BF16)
    q = _dot(cqn, wq_ref[...]) * scale

    ckv = p[:, Q_LORA:]
    lane_kv = lax.broadcasted_iota(jnp.int32, ckv.shape, 1)
    is_kv = lane_kv < KV_LORA
    ss = jnp.sum(jnp.where(is_kv, ckv * ckv, 0.0), axis=-1, keepdims=True) / KV_LORA
    ckvn = jnp.where(is_kv, ckv * lax.rsqrt(ss + RMS_EPS) * kvn_ref[...], ckv).astype(BF16)
    kv = _dot(ckvn, wkv_ref[...])
    k = kv[:, :n_q]
    v_ref[...] = kv[:, n_q:].astype(v_ref.dtype)

    ang = pos_ref[...].astype(F32) * invf_ref[...]
    cos1, sin1 = jnp.cos(ang), jnp.sin(ang)
    cos = jnp.concatenate([cos1] * MLA_HEADS, axis=1)
    sin = jnp.concatenate([sin1] * MLA_HEADS, axis=1)
    lane = lax.broadcasted_iota(jnp.int32, (tm, n_q), 1)
    q_ref[...] = (q * cos + _rope_partner(q, lane) * sin).astype(q_ref.dtype)
    k_ref[...] = (k * cos + _rope_partner(k, lane) * sin).astype(k_ref.dtype)


def _mla_prep(p_mla, pos, invf, q_norm, kv_norm, wq, wkv, *, tm):
    t = p_mla.shape[0]
    n_q = MLA_HEADS * MLA_HEAD_PAD
    n_v = MLA_HEADS * MLA_V
    scale = (MLA_NOPE + MLA_ROPE) ** -0.5
    return pl.pallas_call(
        functools.partial(_mla_prep_kernel, scale=scale),
        grid=(t // tm,),
        in_specs=[
            pl.BlockSpec((tm, MLA_COLS_PAD), lambda i: (i, 0)),
            pl.BlockSpec((tm, 1), lambda i: (i, 0)),
            _const_spec((1, MLA_HEAD_PAD)),
            _const_spec((1, Q_LORA)),
            _const_spec((1, MLA_COLS_PAD - Q_LORA)),
            _const_spec(wq.shape),
            _const_spec(wkv.shape),
        ],
        out_specs=[
            pl.BlockSpec((tm, n_q), lambda i: (i, 0)),
            pl.BlockSpec((tm, n_q), lambda i: (i, 0)),
            pl.BlockSpec((tm, n_v), lambda i: (i, 0)),
        ],
        out_shape=[
            jax.ShapeDtypeStruct((t, n_q), BF16),
            jax.ShapeDtypeStruct((t, n_q), BF16),
            jax.ShapeDtypeStruct((t, n_v), BF16),
        ],
        compiler_params=_params("parallel"),
        name="mla_prep",
    )(p_mla, pos, invf, q_norm, kv_norm, wq, wkv)


def _attn_kernel(q_ref, k_ref, v_ref, o_ref, *, tq):
    i = pl.program_id(2)
    row = lax.broadcasted_iota(jnp.int32, (tq, tq), 0)
    col = lax.broadcasted_iota(jnp.int32, (tq, tq), 1)
    causal = col <= row
    outs = []
    for hh in range(2):
        q = q_ref[:, hh * MLA_HEAD_PAD:(hh + 1) * MLA_HEAD_PAD]

        def tile(j, carry, masked, q=q, hh=hh):
            m, l, acc = carry
            start = pl.multiple_of(j * tq, tq)
            k = k_ref[pl.ds(start, tq), hh * MLA_HEAD_PAD:(hh + 1) * MLA_HEAD_PAD]
            v = v_ref[pl.ds(start, tq), :]
            s = _dot_nt(q, k)
            if masked:
                s = jnp.where(causal, s, -jnp.inf)
            m_new = jnp.maximum(m, jnp.max(s, axis=-1, keepdims=True))
            pr = jnp.exp(s - m_new)
            alpha = jnp.exp(m - m_new)
            l = alpha * l + jnp.sum(pr, axis=-1, keepdims=True)
            acc = alpha * acc + _dot(pr.astype(BF16), v)
            return m_new, l, acc

        init = (jnp.full((tq, 1), -jnp.inf, F32), jnp.zeros((tq, 1), F32),
                jnp.zeros((tq, 2 * MLA_V), F32))
        carry = lax.fori_loop(0, i, functools.partial(tile, masked=False), init)
        _, l, acc = tile(i, carry, True)
        outs.append(acc / l)
    lane = lax.broadcasted_iota(jnp.int32, (tq, 2 * MLA_V), 1)
    o_ref[...] = jnp.where(lane < MLA_V, outs[0], outs[1])


def _attention(q, k, v, *, batch, seq, tq):
    t = q.shape[0]
    nq = seq // tq
    hp = MLA_HEADS // 2
    return pl.pallas_call(
        functools.partial(_attn_kernel, tq=tq),
        grid=(batch, hp, nq),
        in_specs=[
            pl.BlockSpec((tq, 2 * MLA_HEAD_PAD), lambda b, h, i: (b * nq + i, h)),
            pl.BlockSpec((seq, 2 * MLA_HEAD_PAD), lambda b, h, i: (b, h)),
            pl.BlockSpec((seq, 2 * MLA_V), lambda b, h, i: (b, h)),
        ],
        out_specs=pl.BlockSpec((tq, 2 * MLA_V), lambda b, h, i: (b * nq + i, h)),
        out_shape=jax.ShapeDtypeStruct((t, MLA_HEADS * MLA_V), F32),
        compiler_params=_params("parallel", "parallel", "arbitrary"),
        name="mla_attention",
    )(q, k, v)


def _stack_heads(x, lane_head):
    return jnp.concatenate(
        [jnp.where(lane_head == h, x, 0.0) for h in range(RWKV_GROUP)], axis=0)


def _rwkv_kernel(p_ref, mu_ref, vec_ref, w2_ref, a2_ref, g2_ref, ind_ref, tri_ref,
                 o_ref, state_ref, prev_ref):
    c = RWKV_CHUNK
    gw = RWKV_GROUP * RWKV_HEAD
    n_groups = RWKV_HEADS // RWKV_GROUP

    @pl.when(pl.program_id(1) == 0)
    def _():
        state_ref[...] = jnp.zeros_like(state_ref)
        prev_ref[...] = jnp.zeros_like(prev_ref)

    p = p_ref[...]
    rows = lax.broadcasted_iota(jnp.int32, p.shape, 0)
    shifted = jnp.where(rows == 0, prev_ref[...], pltpu.roll(p, 1, 0))
    prev_ref[...] = p[c - 1:c, :]
    p = p + (shifted - p) * mu_ref[...]

    d = RWKV_DIM
    r, k, v = p[:, :d], p[:, d:2 * d], p[:, 2 * d:3 * d]
    lo = p[:, 3 * d:3 * d + DECAY_LORA + AAA_LORA]
    g_lo = p[:, 3 * d + DECAY_LORA + AAA_LORA:]
    w0, a0, k_k, k_a = vec_ref[0:1], vec_ref[1:2], vec_ref[2:3], vec_ref[3:4]
    r_k, ln_w, ln_b = vec_ref[4:5], vec_ref[5:6], vec_ref[6:7]

    z = -(w0 + _dot(jnp.tanh(lo).astype(BF16), w2_ref[...]))
    softplus = jnp.maximum(z, 0.0) + jnp.log(1.0 + jnp.exp(-jnp.abs(z)))
    lw = -jnp.exp(-softplus - 0.5)
    a = jax.nn.sigmoid(a0 + _dot(lo.astype(BF16), a2_ref[...]))
    g = _dot(jax.nn.sigmoid(g_lo).astype(BF16), g2_ref[...])

    ind = ind_ref[...]
    kk = k * k_k
    kk = kk / jnp.maximum(jnp.sqrt(_dot3_r(kk * kk, ind)), 1e-12)
    k2 = k * (1.0 + (a - 1.0) * k_a)
    alpha = kk * a

    tri = tri_ref[...]
    b = _dot3(tri, lw)
    b_prev = b - lw
    b_mid = b[c // 2 - 1:c // 2, :]
    b_end = b[c - 1:c, :]
    kap_t = kk * jnp.exp(b_prev - b_mid)
    r_t = r * jnp.exp(b - b_mid)
    inv_t = jnp.exp(b_mid - b)
    alp_t, k_t = alpha * inv_t, k2 * inv_t
    kap_h = kk * jnp.exp(b_prev)
    r_h = r * jnp.exp(b)
    tail = jnp.exp(b_end - b)
    alp_c, k_c = alpha * tail, k2 * tail
    gam_end = jnp.exp(b_end)

    sr = lax.broadcasted_iota(jnp.int32, (RWKV_GROUP * c, RWKV_GROUP * c), 0)
    sc = lax.broadcasted_iota(jnp.int32, (RWKV_GROUP * c, RWKV_GROUP * c), 1)
    strict = (sc % c) < (sr % c)
    incl = (sc % c) <= (sr % c)
    eye = (sr == sc).astype(F32)
    lane_head = lax.broadcasted_iota(jnp.int32, (c, gw), 1) // RWKV_HEAD

    ys = []
    for gi in range(n_groups):
        sl = slice(gi * gw, (gi + 1) * gw)
        st = lambda x: _stack_heads(x[:, sl], lane_head)
        kr = jnp.concatenate([st(kap_t), st(r_t)], axis=0).astype(BF16)
        ak = jnp.concatenate([st(alp_t), st(k_t)], axis=0).astype(BF16)
        amat = _dot_nt(kr, ak)
        gc = RWKV_GROUP * c
        n_mat = jnp.where(strict, amat[:gc, :gc], 0.0)
        a_kk = jnp.where(strict, amat[:gc, gc:], 0.0).astype(BF16)
        a_ra = jnp.where(incl, amat[gc:, :gc], 0.0).astype(BF16)
        a_rk = jnp.where(incl, amat[gc:, gc:], 0.0).astype(BF16)

        inv = eye - n_mat
        pw = n_mat
        for _ in range(int(np.log2(c)) - 1):
            pwb = pw.astype(BF16)
            pw = _dot(pwb, pwb)
            inv = inv + _dot(inv.astype(BF16), pw.astype(BF16))

        s0 = state_ref[gi]
        s0b = s0.astype(BF16)
        vs = st(v).astype(BF16)
        rhs = _dot_nt(st(kap_h).astype(BF16), s0b) + _dot(a_kk, vs)
        u = -_dot(inv.astype(BF16), rhs.astype(BF16))
        ub = u.astype(BF16)
        y_st = _dot_nt(st(r_h).astype(BF16), s0b) + _dot(a_ra, ub) + _dot(a_rk, vs)
        y = y_st[0:c]
        for h in range(1, RWKV_GROUP):
            y = y + y_st[h * c:(h + 1) * c]
        ys.append(y)
        uv = jnp.concatenate([ub, vs], axis=0)
        akc = jnp.concatenate([st(alp_c), st(k_c)], axis=0).astype(BF16)
        state_ref[gi] = s0 * gam_end[:, sl] + _dot_tn(uv, akc)

    y = jnp.concatenate(ys, axis=1)
    inv_n = 1.0 / RWKV_HEAD
    mean = _dot3_r(y, ind) * inv_n
    yc = y - mean
    var = _dot3_r(yc * yc, ind) * inv_n
    y = yc * lax.rsqrt(var + RWKV_GN_EPS) * ln_w + ln_b
    bonus = _dot3_r(r * k2 * r_k, ind) * v
    o_ref[...] = (y + bonus) * g


def _rwkv(p_rwkv, mu, vecs, w2, a2, g2, *, batch, seq):
    t = p_rwkv.shape[0]
    c = RWKV_CHUNK
    nc = seq // c
    gw = RWKV_GROUP * RWKV_HEAD
    head = np.arange(RWKV_DIM) // RWKV_HEAD
    ind = jnp.asarray(head[:, None] == head[None, :], BF16)
    tri = jnp.asarray(np.tril(np.ones((c, c))), BF16)
    return pl.pallas_call(
        _rwkv_kernel,
        grid=(batch, nc),
        in_specs=[
            pl.BlockSpec((c, RWKV_COLS), lambda b, j: (b * nc + j, 0)),
            _const_spec((1, RWKV_COLS)),
            _const_spec(vecs.shape),
            _const_spec(w2.shape),
            _const_spec(a2.shape),
            _const_spec(g2.shape),
            _const_spec(ind.shape),
            _const_spec(tri.shape),
        ],
        out_specs=pl.BlockSpec((c, RWKV_DIM), lambda b, j: (b * nc + j, 0)),
        out_shape=jax.ShapeDtypeStruct((t, RWKV_DIM), F32),
        scratch_shapes=[
            pltpu.VMEM((RWKV_HEADS // RWKV_GROUP, gw, gw), F32),
            pltpu.VMEM((1, RWKV_COLS), F32),
        ],
        compiler_params=_params("parallel", "arbitrary"),
        name="rwkv7",
    )(p_rwkv, mu, vecs, w2, a2, g2, ind, tri)


def _hgrn_tables():
    c = HG_CHUNK
    t = np.arange(c)
    sel = np.zeros((HG_LEVELS + 2, c, c), np.float32)
    sel[0] = t[None, :] <= t[:, None]
    sel[1] = t[None, :] > t[:, None]
    mask = np.zeros((HG_LEVELS + 1, c, c), np.float32)
    mask[0] = np.eye(c)
    for l in range(1, HG_LEVELS + 1):
        m = 1 << l
        pos = t % m
        ref = t - pos + m // 2 - 1
        is_right = pos >= m // 2
        j = t[None, :]
        sel[l + 1] = np.where(is_right[:, None], (j > ref[:, None]) & (j <= t[:, None]),
                              (j > t[:, None]) & (j <= ref[:, None]))
        same = (t[:, None] // m) == (t[None, :] // m)
        mask[l] = same & is_right[:, None] & ~is_right[None, :]
    return sel.reshape(-1, c), mask


def _hgrn_kernel(q_ref, f_ref, i_ref, g_ref, lbl_ref, gain_ref, sel_ref, mask_ref,
                 o_ref, state_ref, *, layer):
    c = HG_CHUNK

    @pl.when(pl.program_id(1) == 0)
    def _():
        state_ref[...] = jnp.zeros_like(state_ref)

    logits = lbl_ref[...]
    pe = jnp.exp(logits - jnp.max(logits, axis=0, keepdims=True))
    lb = jnp.sum(pe[1:layer + 1], axis=0, keepdims=True) / jnp.sum(pe, axis=0, keepdims=True)

    q = q_ref[...]
    q = q * jax.nn.sigmoid(q)
    forget = lb + (1.0 - lb) * jax.nn.sigmoid(f_ref[...])
    key = 1.0 - forget
    logf = jnp.log(forget)
    vals = i_ref[...]

    e = jnp.exp(_dot3(sel_ref[...], logf))
    e_cum = e[0:c]
    q_hat = (q * e_cum).astype(BF16)
    k_hat = (key * e[c:2 * c]).astype(BF16)
    decay_end = e_cum[c - 1:c, :]
    qb, kb = q.astype(BF16), key.astype(BF16)
    level_ops = []
    t_idx = lax.broadcasted_iota(jnp.int32, q.shape, 0)
    for l in range(HG_LEVELS):
        is_right = (t_idx & (1 << l)) != 0
        level_ops.append((jnp.where(is_right, q, key) * e[(l + 2) * c:(l + 3) * c]).astype(BF16))

    for h in range(HG_HEADS):
        sl = slice(h * HG_K, (h + 1) * HG_K)
        attn = mask_ref[0] * _dot_nt(qb[:, sl], kb[:, sl])
        for l in range(HG_LEVELS):
            ml = level_ops[l][:, sl]
            attn = attn + mask_ref[l + 1] * _dot_nt(ml, ml)
        vh = vals[:, sl].astype(BF16)
        st = state_ref[h]
        o = _dot_nt(q_hat[:, sl], st.astype(BF16)) + _dot(attn.astype(BF16), vh)
        state_ref[h] = st * decay_end[:, sl] + _dot_tn(vh, k_hat[:, sl])
        ms = jnp.mean(o * o, axis=-1, keepdims=True)
        gt = g_ref[:, sl]
        o_ref[:, sl] = o * lax.rsqrt(ms + RMS_EPS) * gain_ref[...] * (gt * jax.nn.sigmoid(gt))


def _hgrn(proj, lb_logits, gain, *, batch, seq, layer):
    t = proj.shape[0]
    c = HG_CHUNK
    nc = seq // c
    d = HG_HEADS * HG_K
    sel, mask = _hgrn_tables()
    sel, mask = jnp.asarray(sel, BF16), jnp.asarray(mask, F32)
    col = lambda n: pl.BlockSpec((c, d), lambda b, j: (b * nc + j, n))
    return pl.pallas_call(
        functools.partial(_hgrn_kernel, layer=layer),
        grid=(batch, nc),
        in_specs=[col(0), col(1), col(2), col(3),
                  _const_spec(lb_logits.shape), _const_spec((1, HG_V)),
                  _const_spec(sel.shape), _const_spec(mask.shape)],
        out_specs=pl.BlockSpec((c, d), lambda b, j: (b * nc + j, 0)),
        out_shape=jax.ShapeDtypeStruct((t, d), F32),
        scratch_shapes=[pltpu.VMEM((HG_HEADS, HG_V, HG_K), F32)],
        compiler_params=_params("parallel", "arbitrary"),
        name="hgrn2",
    )(proj, proj, proj, proj, lb_logits, gain, sel, mask)


def _pad_cols(w, n):
    return jnp.pad(w, ((0, 0), (0, n - w.shape[1])))


def _mla_weights(w_uq, w_ukv):
    hq = MLA_NOPE + MLA_ROPE
    wq = w_uq.reshape(Q_LORA, MLA_HEADS, hq)
    wq = jnp.pad(wq, ((0, 0), (0, 0), (0, MLA_HEAD_PAD - hq))).reshape(Q_LORA, -1)
    wkv = w_ukv.reshape(KV_LORA, MLA_HEADS, MLA_NOPE + MLA_V)
    wk = jnp.pad(wkv[:, :, :MLA_NOPE], ((0, 0), (0, 0), (0, MLA_HEAD_PAD - MLA_NOPE)))
    wk = wk.reshape(KV_LORA, -1)
    wv = wkv[:, :, MLA_NOPE:].reshape(KV_LORA, -1)
    place = np.zeros((MLA_COLS_PAD - Q_LORA - KV_LORA, MLA_HEADS, MLA_HEAD_PAD), np.float32)
    for r in range(MLA_ROPE):
        place[r, :, MLA_NOPE + r] = 1.0
    place = jnp.asarray(place.reshape(place.shape[0], -1))
    top = jnp.concatenate([wk, wv], axis=1)
    bottom = jnp.concatenate([place, jnp.zeros((place.shape[0], wv.shape[1]), F32)], axis=1)
    return wq.astype(BF16), jnp.concatenate([top, bottom], axis=0).astype(BF16)


def _rope_inv_freq():
    inv = 1.0 / (ROPE_BASE ** (jnp.arange(0, MLA_ROPE, 2, dtype=F32) / MLA_ROPE))
    half = MLA_ROPE // 2
    lane = jnp.zeros((MLA_HEAD_PAD,), F32)
    lane = lane.at[MLA_NOPE:MLA_NOPE + half].set(inv)
    lane = lane.at[MLA_NOPE + half:MLA_NOPE + MLA_ROPE].set(inv)
    return lane.reshape(1, MLA_HEAD_PAD)


def kernel(x, c, positions, ada_w, ada_b, norm_mix, norm_ffn, w_in_even, mla_q_norm, mla_w_uq, mla_kv_norm, mla_w_ukv, rwkv_mu, rwkv_w0, rwkv_w2, rwkv_a0, rwkv_a2, rwkv_g2, rwkv_k_k, rwkv_k_a, rwkv_r_k, rwkv_ln_w, rwkv_ln_b, w_out_even, w_in_odd, hg_lb_logits, hg_out_norm, w_out_odd, ffn_w_gate, ffn_w_up, ffn_w_down, final_norm):
    batch, seq, d = x.shape
    depth = ada_w.shape[0]
    t = batch * seq
    tm = min(512, seq)
    xt = x.reshape(t, d)
    pos = positions.reshape(t, 1)

    mod_all = _ada_mod(c, ada_w, ada_b).reshape(depth, batch, 6, d)

    for l in range(depth):
        mod = mod_all[l]
        j = l // 2
        if l % 2 == 0:
            w_in = w_in_even[j]
            w_mla = _pad_cols(w_in[:, :MLA_COLS], MLA_COLS_PAD).astype(BF16)
            w_rwkv = w_in[:, MLA_COLS:].astype(BF16)
            p_mla, p_rwkv = _in_proj(xt, mod, norm_mix[l], [w_mla, w_rwkv],
                                     seq=seq, sh=0, sc=1, tm=tm)
            wq, wkv = _mla_weights(mla_w_uq[j], mla_w_ukv[j])
            kv_gain = _pad_cols(mla_kv_norm[j].reshape(1, -1), MLA_COLS_PAD - Q_LORA)
            qh, kh, vh = _mla_prep(p_mla, pos, _rope_inv_freq(), mla_q_norm[j].reshape(1, -1),
                                   kv_gain, wq, wkv, tm=tm)
            y_a = _attention(qh, kh, vh, batch=batch, seq=seq, tq=min(256, seq))

            zeros_lo = jnp.zeros((DECAY_LORA, RWKV_DIM), F32)
            w2 = jnp.concatenate([rwkv_w2[j], zeros_lo], axis=0).astype(BF16)
            a2 = jnp.concatenate([zeros_lo, rwkv_a2[j]], axis=0).astype(BF16)
            vecs = jnp.stack([rwkv_w0[j], rwkv_a0[j], rwkv_k_k[j], rwkv_k_a[j],
                              rwkv_r_k[j].reshape(-1), rwkv_ln_w[j], rwkv_ln_b[j],
                              jnp.zeros((RWKV_DIM,), F32)])
            y_b = _rwkv(p_rwkv, rwkv_mu[j].reshape(1, -1), vecs, w2, a2,
                        rwkv_g2[j].astype(BF16), batch=batch, seq=seq)
            w_out = w_out_even[j].astype(BF16)
            n_a = MLA_HEADS * MLA_V
            xt = _out_proj(xt, mod, [y_a, y_b], [w_out[:n_a], w_out[n_a:]],
                           seq=seq, gate=2, tm=tm)
        else:
            (proj,) = _in_proj(xt, mod, norm_mix[l], [w_in_odd[j].astype(BF16)],
                               seq=seq, sh=0, sc=1, tm=tm)
            y = _hgrn(proj, hg_lb_logits, hg_out_norm[j].reshape(1, -1),
                      batch=batch, seq=seq, layer=l)
            xt = _out_proj(xt, mod, [y], [w_out_odd[j].astype(BF16)], seq=seq, gate=2, tm=tm)
        xt = _ffn(xt, mod, norm_ffn[l], ffn_w_gate[l].astype(BF16), ffn_w_up[l].astype(BF16),
                  ffn_w_down[l].astype(BF16), final_norm, seq=seq, tm=tm,
                  final=(l == depth - 1))
    return xt.reshape(batch, seq, d)
```

```python
import functools

import numpy as np
import jax
import jax.numpy as jnp
from jax import lax
from jax.experimental import pallas as pl
from jax.experimental.pallas import tpu as pltpu

F32 = jnp.float32
BF16 = jnp.bfloat16

RMS_EPS = 1e-6
LANES = 128
VMEM_LIMIT = 56 * 1024 * 1024

MLA_HEADS = 8
MLA_NOPE = 64
MLA_ROPE = 32
MLA_V = 64
Q_LORA = 384
KV_LORA = 256
ROPE_BASE = 10000.0
MLA_COLS = Q_LORA + KV_LORA + MLA_ROPE
MLA_COLS_PAD = 768
MLA_HEAD_PAD = 128
ATTN_TILE = 256
ATTN_HEADS = 4

RWKV_HEAD = 64
RWKV_DIM = 512
RWKV_HEADS = 8
DECAY_LORA = 64
AAA_LORA = 64
GATE_LORA = 128
RWKV_GN_EPS = RWKV_HEAD * 1e-5
RWKV_COLS = 3 * RWKV_DIM + DECAY_LORA + AAA_LORA + GATE_LORA
RWKV_CHUNK = 64
RWKV_BLOCK_CHUNKS = 4
RWKV_GROUP = 4

HG_K = 128
HG_HEADS = 8
HG_V = 128
HG_CHUNK = 128
HG_LEVELS = 7


def _params(*sem):
    return pltpu.CompilerParams(dimension_semantics=sem, vmem_limit_bytes=VMEM_LIMIT)


def _const_spec(shape):
    nd = len(shape)
    return pl.BlockSpec(shape, lambda *_: (0,) * nd, pipeline_mode=pl.Buffered(1))


def _dot(a, b):
    return jnp.dot(a, b, preferred_element_type=F32)


def _dot_nt(a, b):
    return lax.dot_general(a, b, (((1,), (1,)), ((), ())), preferred_element_type=F32)


def _dot_tn(a, b):
    return lax.dot_general(a, b, (((0,), (0,)), ((), ())), preferred_element_type=F32)


def _split3(x):
    hi = x.astype(BF16)
    r1 = x - hi.astype(F32)
    mid = r1.astype(BF16)
    lo = (r1 - mid.astype(F32)).astype(BF16)
    return hi, mid, lo


def _dot3(sel, x):
    hi, mid, lo = _split3(x)
    return _dot(sel, hi) + _dot(sel, mid) + _dot(sel, lo)


def _head_sum(x, ind):
    return _dot(x.astype(BF16), ind)


def _ada_kernel(c_ref, w_ref, b_ref, o_ref):
    c = c_ref[...]
    cond = c * jax.nn.sigmoid(c)
    o_ref[0] = _dot(cond.astype(BF16), w_ref[0].astype(BF16)) + b_ref[0]


def _ada_mod(c, ada_w, ada_b):
    depth, d, n = ada_w.shape
    b = c.shape[0]
    tn = n // 4
    return pl.pallas_call(
        _ada_kernel,
        grid=(depth, n // tn),
        in_specs=[
            pl.BlockSpec((b, d), lambda l, j: (0, 0)),
            pl.BlockSpec((1, d, tn), lambda l, j: (l, 0, j)),
            pl.BlockSpec((1, 1, tn), lambda l, j: (l, 0, j)),
        ],
        out_specs=pl.BlockSpec((1, b, tn), lambda l, j: (l, 0, j)),
        out_shape=jax.ShapeDtypeStruct((depth, b, n), F32),
        compiler_params=_params("arbitrary", "arbitrary"),
        name="ada_mod",
    )(c, ada_w, ada_b.reshape(depth, 1, n))


def _norm_mod(x, gain, mod, sh, sc):
    ms = jnp.mean(x * x, axis=-1, keepdims=True)
    y = x * lax.rsqrt(ms + RMS_EPS) * gain
    return y * (1.0 + mod[sc:sc + 1]) + mod[sh:sh + 1]


def _in_proj_kernel(x_ref, mod_ref, gain_ref, *refs, n_out, sh, sc):
    w_refs, o_refs = refs[:n_out], refs[n_out:]
    h = _norm_mod(x_ref[...], gain_ref[...], mod_ref[0], sh, sc).astype(BF16)
    for w_ref, o_ref in zip(w_refs, o_refs):
        o_ref[...] = _dot(h, w_ref[...]).astype(o_ref.dtype)


def _in_proj(x, mod, gain, weights, *, seq, sh, sc, tm):
    t, d = x.shape
    per_b = seq // tm
    n_out = len(weights)
    in_specs = [
        pl.BlockSpec((tm, d), lambda i: (i, 0)),
        pl.BlockSpec((1, 6, d), lambda i: (i // per_b, 0, 0)),
        _const_spec((1, d)),
    ] + [_const_spec(w.shape) for w in weights]
    out_specs = [pl.BlockSpec((tm, w.shape[1]), lambda i: (i, 0)) for w in weights]
    out_shape = [jax.ShapeDtypeStruct((t, w.shape[1]), F32) for w in weights]
    return pl.pallas_call(
        functools.partial(_in_proj_kernel, n_out=n_out, sh=sh, sc=sc),
        grid=(t // tm,),
        in_specs=in_specs,
        out_specs=out_specs,
        out_shape=out_shape,
        compiler_params=_params("parallel"),
        name="in_proj",
    )(x, mod, gain.reshape(1, d), *weights)


def _out_proj_kernel(x_ref, mod_ref, *refs, n_in, gate):
    y_refs, w_refs, o_ref = refs[:n_in], refs[n_in:2 * n_in], refs[2 * n_in]
    acc = _dot(y_refs[0][...].astype(BF16), w_refs[0][...])
    for y_ref, w_ref in zip(y_refs[1:], w_refs[1:]):
        acc += _dot(y_ref[...].astype(BF16), w_ref[...])
    g = mod_ref[0][gate:gate + 1]
    o_ref[...] = x_ref[...] + g * acc


def _out_proj(x, mod, ys, weights, *, seq, gate, tm):
    t, d = x.shape
    per_b = seq // tm
    n_in = len(ys)
    in_specs = [
        pl.BlockSpec((tm, d), lambda i: (i, 0)),
        pl.BlockSpec((1, 6, d), lambda i: (i // per_b, 0, 0)),
    ] + [pl.BlockSpec((tm, y.shape[1]), lambda i: (i, 0)) for y in ys] \
      + [_const_spec(w.shape) for w in weights]
    return pl.pallas_call(
        functools.partial(_out_proj_kernel, n_in=n_in, gate=gate),
        grid=(t // tm,),
        in_specs=in_specs,
        out_specs=pl.BlockSpec((tm, d), lambda i: (i, 0)),
        out_shape=jax.ShapeDtypeStruct((t, d), F32),
        compiler_params=_params("parallel"),
        name="out_proj",
    )(x, mod, *ys, *weights)


def _ffn_kernel(x_ref, mod_ref, gain_ref, wg_ref, wu_ref, wd_ref, fin_ref, o_ref, *, final):
    x = x_ref[...]
    mod = mod_ref[0]
    h = _norm_mod(x, gain_ref[...], mod, 3, 4).astype(BF16)
    gate = _dot(h, wg_ref[...])
    up = _dot(h, wu_ref[...])
    act = (gate * jax.nn.sigmoid(gate) * up).astype(BF16)
    y = x + mod[5:6] * _dot(act, wd_ref[...])
    if final:
        ms = jnp.mean(y * y, axis=-1, keepdims=True)
        y = y * lax.rsqrt(ms + RMS_EPS) * fin_ref[...]
    o_ref[...] = y


def _ffn(x, mod, gain, wg, wu, wd, fin, *, seq, tm, final):
    t, d = x.shape
    per_b = seq // tm
    return pl.pallas_call(
        functools.partial(_ffn_kernel, final=final),
        grid=(t // tm,),
        in_specs=[
            pl.BlockSpec((tm, d), lambda i: (i, 0)),
            pl.BlockSpec((1, 6, d), lambda i: (i // per_b, 0, 0)),
            _const_spec((1, d)),
            _const_spec(wg.shape),
            _const_spec(wu.shape),
            _const_spec(wd.shape),
            _const_spec((1, d)),
        ],
        out_specs=pl.BlockSpec((tm, d), lambda i: (i, 0)),
        out_shape=jax.ShapeDtypeStruct((t, d), F32),
        compiler_params=_params("parallel"),
        name="ffn",
    )(x, mod, gain.reshape(1, d), wg, wu, wd, fin.reshape(1, d))


def _mla_prep_kernel(p_ref, pos_ref, invf_ref, qn_ref, kvn_ref, wqt_ref, wk_ref, wvt_ref,
                     qt_ref, k_ref, vt_ref, *, scale, tk):
    p = p_ref[...]
    tm = p.shape[0]
    half = MLA_ROPE // 2
    ang = invf_ref[...] * pos_ref[0].astype(F32)
    cos_t, sin_t = jnp.cos(ang), jnp.sin(ang)

    c_q = p[:, :Q_LORA]
    ms = jnp.mean(c_q * c_q, axis=-1, keepdims=True)
    cqn = (c_q * lax.rsqrt(ms + RMS_EPS) * qn_ref[...]).astype(BF16)
    qt = _dot_nt(wqt_ref[...], cqn) * scale
    pieces = []
    for h in range(MLA_HEADS):
        base = h * MLA_HEAD_PAD
        x1 = qt[base + MLA_NOPE:base + MLA_NOPE + half]
        x2 = qt[base + MLA_NOPE + half:base + MLA_NOPE + MLA_ROPE]
        pieces += [qt[base:base + MLA_NOPE], x1 * cos_t - x2 * sin_t, x1 * sin_t + x2 * cos_t,
                   qt[base + MLA_NOPE + MLA_ROPE:base + MLA_HEAD_PAD]]
    qt_ref[0] = jnp.concatenate(pieces, axis=0).astype(qt_ref.dtype)

    ckv = p[:, Q_LORA:Q_LORA + KV_LORA]
    ss = jnp.mean(ckv * ckv, axis=-1, keepdims=True)
    ckvn = ckv * lax.rsqrt(ss + RMS_EPS) * kvn_ref[...]
    kr = p[:, Q_LORA + KV_LORA:]
    rest = kr.shape[1] - MLA_ROPE
    cos_k = jnp.concatenate([cos_t, cos_t, jnp.ones((rest, tm), F32)], axis=0).T
    sin_k = jnp.concatenate([sin_t, sin_t, jnp.zeros((rest, tm), F32)], axis=0).T
    lane = lax.broadcasted_iota(jnp.int32, kr.shape, 1)
    partner = jnp.where(lane < half, -pltpu.roll(kr, kr.shape[1] - half, 1), pltpu.roll(kr, half, 1))
    lhs = jnp.concatenate([ckvn, kr * cos_k + partner * sin_k], axis=1).astype(BF16)
    k_ref[...] = _dot(lhs, wk_ref[...]).astype(k_ref.dtype)
    vt = _dot_nt(wvt_ref[...], lhs).astype(vt_ref.dtype)
    for jj in range(tm // tk):
        vt_ref[0, jj] = vt[:, jj * tk:(jj + 1) * tk]


def _mla_prep(p_mla, pos, invf, q_norm, kv_norm, wqt, wk, wvt, *, batch, seq, tm, tk):
    t = p_mla.shape[0]
    n_q = MLA_HEADS * MLA_HEAD_PAD
    n_v = MLA_HEADS * MLA_V
    per_b = seq // tm
    scale = (MLA_NOPE + MLA_ROPE) ** -0.5
    return pl.pallas_call(
        functools.partial(_mla_prep_kernel, scale=scale, tk=tk),
        grid=(t // tm,),
        in_specs=[
            pl.BlockSpec((tm, MLA_COLS_PAD), lambda i: (i, 0)),
            pl.BlockSpec((1, 1, tm), lambda i: (i, 0, 0)),
            _const_spec(invf.shape),
            _const_spec((1, Q_LORA)),
            _const_spec((1, KV_LORA)),
            _const_spec(wqt.shape),
            _const_spec(wk.shape),
            _const_spec(wvt.shape),
        ],
        out_specs=[
            pl.BlockSpec((1, n_q, tm), lambda i: (i // per_b, 0, i % per_b)),
            pl.BlockSpec((tm, n_q), lambda i: (i, 0)),
            pl.BlockSpec((1, tm // tk, n_v, tk), lambda i: (i // per_b, i % per_b, 0, 0)),
        ],
        out_shape=[
            jax.ShapeDtypeStruct((batch, n_q, seq), BF16),
            jax.ShapeDtypeStruct((t, n_q), BF16),
            jax.ShapeDtypeStruct((batch, seq // tk, n_v, tk), BF16),
        ],
        compiler_params=_params("parallel"),
        name="mla_prep",
    )(p_mla, pos.reshape(t // tm, 1, tm), invf, q_norm, kv_norm, wqt, wk, wvt)


def _attn_kernel(qt_ref, k_ref, vt_ref, o_ref, *, tq, nh):
    i = pl.program_id(2)
    qts = [qt_ref[0, h * MLA_HEAD_PAD:(h + 1) * MLA_HEAD_PAD, :] for h in range(nh)]
    causal = (lax.broadcasted_iota(jnp.int32, (tq, tq), 0)
              <= lax.broadcasted_iota(jnp.int32, (tq, tq), 1))

    def tile(j, carry, masked):
        start = pl.multiple_of(j * tq, tq)
        scores = [_dot(k_ref[pl.ds(start, tq), h * MLA_HEAD_PAD:(h + 1) * MLA_HEAD_PAD], qts[h])
                  for h in range(nh)]
        new = []
        for h in range(nh):
            m, l, acc = carry[h]
            s = jnp.where(causal, scores[h], -jnp.inf) if masked else scores[h]
            m_new = jnp.maximum(m, jnp.max(s, axis=0, keepdims=True))
            pr = jnp.exp(s - m_new)
            alpha = jnp.exp(m - m_new)
            l = alpha * l + jnp.sum(pr, axis=0, keepdims=True)
            vt = vt_ref[0, j, h * MLA_V:(h + 1) * MLA_V, :]
            acc = alpha * acc + _dot(vt, pr.astype(BF16))
            new.append((m_new, l, acc))
        return tuple(new)

    init = tuple((jnp.full((1, tq), -jnp.inf, F32), jnp.zeros((1, tq), F32),
                  jnp.zeros((MLA_V, tq), F32)) for _ in range(nh))
    carry = lax.fori_loop(0, i, functools.partial(tile, masked=False), init)
    carry = tile(i, carry, True)
    out_t = jnp.concatenate([acc / l for _, l, acc in carry], axis=0)
    o_ref[...] = out_t.T


def _attention(qt, k, vt, *, batch, seq, tq, nh):
    t = k.shape[0]
    nq = seq // tq
    return pl.pallas_call(
        functools.partial(_attn_kernel, tq=tq, nh=nh),
        grid=(batch, MLA_HEADS // nh, nq),
        in_specs=[
            pl.BlockSpec((1, nh * MLA_HEAD_PAD, tq), lambda b, h, i: (b, h, i)),
            pl.BlockSpec((seq, nh * MLA_HEAD_PAD), lambda b, h, i: (b, h)),
            pl.BlockSpec((1, seq // tq, nh * MLA_V, tq), lambda b, h, i: (b, 0, h, 0)),
        ],
        out_specs=pl.BlockSpec((tq, nh * MLA_V), lambda b, h, i: (b * nq + i, h)),
        out_shape=jax.ShapeDtypeStruct((t, MLA_HEADS * MLA_V), F32),
        compiler_params=_params("parallel", "parallel", "arbitrary"),
        name="mla_attention",
    )(qt, k, vt)


def _stack_heads(x, lane_head):
    return jnp.concatenate(
        [jnp.where(lane_head == h, x, 0.0) for h in range(RWKV_GROUP)], axis=0)


def _rwkv_kernel(p_ref, mu_ref, vec_ref, w2_ref, a2_ref, g2_ref, ind_ref, tri_ref,
                 o_ref, state_ref, prev_ref, *, n_chunks):
    c = RWKV_CHUNK
    gw = RWKV_GROUP * RWKV_HEAD
    gc = RWKV_GROUP * c
    n_groups = RWKV_HEADS // RWKV_GROUP
    tb = n_chunks * c

    @pl.when(pl.program_id(1) == 0)
    def _():
        state_ref[...] = jnp.zeros_like(state_ref)
        prev_ref[...] = jnp.zeros_like(prev_ref)

    p = p_ref[...]
    rows = lax.broadcasted_iota(jnp.int32, p.shape, 0)
    shifted = jnp.where(rows == 0, prev_ref[...], pltpu.roll(p, 1, 0))
    prev_ref[...] = p[tb - 1:tb, :]
    p = p + (shifted - p) * mu_ref[...]

    d = RWKV_DIM
    r, k, v = p[:, :d], p[:, d:2 * d], p[:, 2 * d:3 * d]
    lo = p[:, 3 * d:3 * d + DECAY_LORA + AAA_LORA]
    g_lo = p[:, 3 * d + DECAY_LORA + AAA_LORA:]
    w0, a0, k_k, k_a = vec_ref[0:1], vec_ref[1:2], vec_ref[2:3], vec_ref[3:4]
    r_k, ln_w, ln_b = vec_ref[4:5], vec_ref[5:6], vec_ref[6:7]

    z = -(w0 + _dot(jnp.tanh(lo).astype(BF16), w2_ref[...]))
    softplus = jnp.maximum(z, 0.0) + jnp.log(1.0 + jnp.exp(-jnp.abs(z)))
    lw = -jnp.exp(-softplus - 0.5)
    a = jax.nn.sigmoid(a0 + _dot(lo.astype(BF16), a2_ref[...]))
    g = _dot(jax.nn.sigmoid(g_lo).astype(BF16), g2_ref[...])

    ind = ind_ref[...]
    kk = k * k_k
    kk = kk / jnp.maximum(jnp.sqrt(_head_sum(kk * kk, ind)), 1e-12)
    k2 = k * (1.0 + (a - 1.0) * k_a)
    alpha = kk * a

    b = _dot3(tri_ref[...], lw)
    b_prev = b - lw
    bcast = lambda row: jnp.broadcast_to(row, (c, d))
    b_mid = jnp.concatenate([bcast(b[ci * c + c // 2 - 1:ci * c + c // 2]) for ci in range(n_chunks)], axis=0)
    b_end = jnp.concatenate([bcast(b[ci * c + c - 1:ci * c + c]) for ci in range(n_chunks)], axis=0)
    kap_t = kk * jnp.exp(b_prev - b_mid)
    r_t = r * jnp.exp(b - b_mid)
    inv_t = jnp.exp(b_mid - b)
    alp_t, k_t = alpha * inv_t, k2 * inv_t
    kap_h = kk * jnp.exp(b_prev)
    r_h = r * jnp.exp(b)
    tail = jnp.exp(b_end - b)
    alp_c, k_c = alpha * tail, k2 * tail
    gam_end = jnp.exp(b_end)

    sr = lax.broadcasted_iota(jnp.int32, (gc, gc), 0)
    sc = lax.broadcasted_iota(jnp.int32, (gc, gc), 1)
    strict = (sc % c) < (sr % c)
    incl = (sc % c) <= (sr % c)
    eye = (sr == sc).astype(F32)
    lane_head = lax.broadcasted_iota(jnp.int32, (c, gw), 1) // RWKV_HEAD

    def stacked(x, ci, gi):
        return _stack_heads(x[ci * c:(ci + 1) * c, gi * gw:(gi + 1) * gw], lane_head)

    pairs = [(ci, gi) for ci in range(n_chunks) for gi in range(n_groups)]
    amat, vs, n_pow, inv = {}, {}, {}, {}
    for pr in pairs:
        st = lambda x: stacked(x, *pr)
        kr = jnp.concatenate([st(kap_t), st(r_t)], axis=0).astype(BF16)
        ak = jnp.concatenate([st(alp_t), st(k_t)], axis=0).astype(BF16)
        amat[pr] = _dot_nt(kr, ak)
        vs[pr] = st(v).astype(BF16)
    for pr in pairs:
        n_pow[pr] = jnp.where(strict, amat[pr][:gc, :gc], 0.0)
        inv[pr] = eye - n_pow[pr]
    for _ in range(int(np.log2(c)) - 1):
        for pr in pairs:
            pwb = n_pow[pr].astype(BF16)
            n_pow[pr] = _dot(pwb, pwb)
        for pr in pairs:
            inv[pr] = inv[pr] + _dot(inv[pr].astype(BF16), n_pow[pr].astype(BF16))
    akv = {pr: _dot(jnp.where(strict, amat[pr][:gc, gc:], 0.0).astype(BF16), vs[pr]).astype(BF16)
           for pr in pairs}
    pre = {}
    for pr in pairs:
        st = lambda x: stacked(x, *pr)
        invb = inv[pr].astype(BF16)
        kt = _dot(invb, st(kap_h).astype(BF16)).astype(BF16)
        w_mat = _dot(invb, akv[pr])
        a_ra = jnp.where(incl, amat[pr][gc:, :gc], 0.0).astype(BF16)
        a_rk = jnp.where(incl, amat[pr][gc:, gc:], 0.0).astype(BF16)
        akc = jnp.concatenate([st(alp_c), st(k_c)], axis=0).astype(BF16)
        pre[pr] = (kt, w_mat, a_ra, a_rk, st(r_h).astype(BF16), akc)

    y_rows = []
    for ci in range(n_chunks):
        s0 = [state_ref[gi] for gi in range(n_groups)]
        s0b = [x.astype(BF16) for x in s0]
        ub = [(-(_dot_nt(pre[ci, gi][0], s0b[gi]) + pre[ci, gi][1])).astype(BF16)
              for gi in range(n_groups)]
        ys = []
        for gi in range(n_groups):
            kt, w_mat, a_ra, a_rk, rh, akc = pre[ci, gi]
            uv = jnp.concatenate([ub[gi], vs[ci, gi]], axis=0)
            sl = slice(gi * gw, (gi + 1) * gw)
            state_ref[gi] = s0[gi] * gam_end[ci * c:ci * c + 1, sl] + _dot_tn(uv, akc)
            y_st = _dot_nt(rh, s0b[gi]) + _dot(a_ra, ub[gi]) + _dot(a_rk, vs[ci, gi])
            y = y_st[0:c]
            for h in range(1, RWKV_GROUP):
                y = y + y_st[h * c:(h + 1) * c]
            ys.append(y)
        y_rows.append(jnp.concatenate(ys, axis=1))
    y = jnp.concatenate(y_rows, axis=0)

    inv_n = 1.0 / RWKV_HEAD
    mean = _head_sum(y, ind) * inv_n
    yc = y - mean
    var = _head_sum(yc * yc, ind) * inv_n
    y = yc * lax.rsqrt(var + RWKV_GN_EPS) * ln_w + ln_b
    bonus = _head_sum(r * k2 * r_k, ind) * v
    o_ref[...] = (y + bonus) * g


def _rwkv(p_rwkv, mu, vecs, w2, a2, g2, *, batch, seq, n_chunks):
    t = p_rwkv.shape[0]
    c = RWKV_CHUNK
    tb = n_chunks * c
    nb = seq // tb
    gw = RWKV_GROUP * RWKV_HEAD
    head = np.arange(RWKV_DIM) // RWKV_HEAD
    ind = jnp.asarray(head[:, None] == head[None, :], BF16)
    tt = np.arange(tb)
    tri = jnp.asarray((tt[None, :] <= tt[:, None]) & (tt[None, :] // c == tt[:, None] // c), BF16)
    return pl.pallas_call(
        functools.partial(_rwkv_kernel, n_chunks=n_chunks),
        grid=(batch, nb),
        in_specs=[
            pl.BlockSpec((tb, RWKV_COLS), lambda b, j: (b * nb + j, 0)),
            _const_spec((1, RWKV_COLS)),
            _const_spec(vecs.shape),
            _const_spec(w2.shape),
            _const_spec(a2.shape),
            _const_spec(g2.shape),
            _const_spec(ind.shape),
            _const_spec(tri.shape),
        ],
        out_specs=pl.BlockSpec((tb, RWKV_DIM), lambda b, j: (b * nb + j, 0)),
        out_shape=jax.ShapeDtypeStruct((t, RWKV_DIM), F32),
        scratch_shapes=[
            pltpu.VMEM((RWKV_HEADS // RWKV_GROUP, gw, gw), F32),
            pltpu.VMEM((1, RWKV_COLS), F32),
        ],
        compiler_params=_params("parallel", "arbitrary"),
        name="rwkv7",
    )(p_rwkv, mu, vecs, w2, a2, g2, ind, tri)


def _hgrn_masks():
    c = HG_CHUNK
    t = np.arange(c)
    mask = np.zeros((HG_LEVELS + 1, c, c), np.float32)
    mask[0] = np.eye(c)
    for l in range(1, HG_LEVELS + 1):
        m = 1 << l
        is_right = (t % m) >= m // 2
        same = (t[:, None] // m) == (t[None, :] // m)
        mask[l] = same & is_right[:, None] & ~is_right[None, :]
    return mask


def _hgrn_kernel(q_ref, f_ref, i_ref, g_ref, lbl_ref, gain_ref, tri_ref, mask_ref,
                 o_ref, state_ref, *, layer):
    c = HG_CHUNK

    @pl.when(pl.program_id(1) == 0)
    def _():
        state_ref[...] = jnp.zeros_like(state_ref)

    logits = lbl_ref[...]
    pe = jnp.exp(logits - jnp.max(logits, axis=0, keepdims=True))
    lb = jnp.sum(pe[1:layer + 1], axis=0, keepdims=True) / jnp.sum(pe, axis=0, keepdims=True)

    q = q_ref[...]
    q = q * jax.nn.sigmoid(q)
    forget = lb + (1.0 - lb) * jax.nn.sigmoid(f_ref[...])
    key = 1.0 - forget
    logf = jnp.log(forget)
    vals = i_ref[...]
    n = q.shape[1]

    b = _dot3(tri_ref[...], logf)
    e_cum = jnp.exp(b)
    b_end = b[c - 1:c, :]
    q_hat = (q * e_cum).astype(BF16)
    k_hat = (key * jnp.exp(b_end - b)).astype(BF16)
    decay_end = e_cum[c - 1:c, :]
    qb, kb = q.astype(BF16), key.astype(BF16)

    t_idx = lax.broadcasted_iota(jnp.int32, q.shape, 0)
    f_prev = pltpu.roll(forget, 1, 0)
    f_next = pltpu.roll(forget, c - 1, 0)
    level_ops = []
    for l in range(HG_LEVELS):
        m = 2 << l
        is_right = (t_idx & (m // 2)) != 0
        if m == 2:
            decay = jnp.where(is_right, forget, 1.0)
        elif m == 4:
            pos = t_idx & 3
            decay = jnp.where(pos == 0, f_next,
                              jnp.where(pos == 1, 1.0, jnp.where(pos == 2, forget, forget * f_prev)))
        else:
            b3 = b.reshape(c // m, m, n)
            b_ref = jnp.broadcast_to(b3[:, m // 2 - 1:m // 2, :], (c // m, m, n)).reshape(c, n)
            decay = jnp.exp(-jnp.abs(b - b_ref))
        level_ops.append((jnp.where(is_right, q, key) * decay).astype(BF16))

    for h in range(HG_HEADS):
        sl = slice(h * HG_K, (h + 1) * HG_K)
        attn = mask_ref[0] * _dot_nt(qb[:, sl], kb[:, sl])
        for l in range(HG_LEVELS):
            ml = level_ops[l][:, sl]
            attn = attn + mask_ref[l + 1] * _dot_nt(ml, ml)
        vh = vals[:, sl].astype(BF16)
        st = state_ref[h]
        o = _dot_nt(q_hat[:, sl], st.astype(BF16)) + _dot(attn.astype(BF16), vh)
        state_ref[h] = st * decay_end[:, sl] + _dot_tn(vh, k_hat[:, sl])
        ms = jnp.mean(o * o, axis=-1, keepdims=True)
        gt = g_ref[:, sl]
        o_ref[:, sl] = o * lax.rsqrt(ms + RMS_EPS) * gain_ref[...] * (gt * jax.nn.sigmoid(gt))


def _hgrn(proj, lb_logits, gain, *, batch, seq, layer):
    t = proj.shape[0]
    c = HG_CHUNK
    nc = seq // c
    d = HG_HEADS * HG_K
    tri = jnp.asarray(np.tril(np.ones((c, c))), BF16)
    mask = jnp.asarray(_hgrn_masks(), F32)
    col = lambda n: pl.BlockSpec((c, d), lambda b, j: (b * nc + j, n))
    return pl.pallas_call(
        functools.partial(_hgrn_kernel, layer=layer),
        grid=(batch, nc),
        in_specs=[col(0), col(1), col(2), col(3),
                  _const_spec(lb_logits.shape), _const_spec((1, HG_V)),
                  _const_spec(tri.shape), _const_spec(mask.shape)],
        out_specs=pl.BlockSpec((c, d), lambda b, j: (b * nc + j, 0)),
        out_shape=jax.ShapeDtypeStruct((t, d), F32),
        scratch_shapes=[pltpu.VMEM((HG_HEADS, HG_V, HG_K), F32)],
        compiler_params=_params("parallel", "arbitrary"),
        name="hgrn2",
    )(proj, proj, proj, proj, lb_logits, gain, tri, mask)


def _pad_cols(w, n):
    return jnp.pad(w, ((0, 0), (0, n - w.shape[1])))


def _mla_weights(w_uq, w_ukv):
    hq = MLA_NOPE + MLA_ROPE
    wq = w_uq.reshape(Q_LORA, MLA_HEADS, hq)
    wq = jnp.pad(wq, ((0, 0), (0, 0), (0, MLA_HEAD_PAD - hq))).reshape(Q_LORA, -1)
    wkv = w_ukv.reshape(KV_LORA, MLA_HEADS, MLA_NOPE + MLA_V)
    wk = jnp.pad(wkv[:, :, :MLA_NOPE], ((0, 0), (0, 0), (0, MLA_HEAD_PAD - MLA_NOPE)))
    wk = wk.reshape(KV_LORA, -1)
    wv = wkv[:, :, MLA_NOPE:].reshape(KV_LORA, -1)
    n_extra = MLA_COLS_PAD - Q_LORA - KV_LORA
    place = np.zeros((n_extra, MLA_HEADS, MLA_HEAD_PAD), np.float32)
    for r in range(MLA_ROPE):
        place[r, :, MLA_NOPE + r] = 1.0
    wk_full = jnp.concatenate([wk, jnp.asarray(place.reshape(n_extra, -1))], axis=0)
    wv_full = jnp.concatenate([wv, jnp.zeros((n_extra, wv.shape[1]), F32)], axis=0)
    return wq.T.astype(BF16), wk_full.astype(BF16), wv_full.T.astype(BF16)


def _rope_inv_freq():
    inv = 1.0 / (ROPE_BASE ** (jnp.arange(0, MLA_ROPE, 2, dtype=F32) / MLA_ROPE))
    return inv.reshape(MLA_ROPE // 2, 1)


def kernel(x, c, positions, ada_w, ada_b, norm_mix, norm_ffn, w_in_even, mla_q_norm, mla_w_uq, mla_kv_norm, mla_w_ukv, rwkv_mu, rwkv_w0, rwkv_w2, rwkv_a0, rwkv_a2, rwkv_g2, rwkv_k_k, rwkv_k_a, rwkv_r_k, rwkv_ln_w, rwkv_ln_b, w_out_even, w_in_odd, hg_lb_logits, hg_out_norm, w_out_odd, ffn_w_gate, ffn_w_up, ffn_w_down, final_norm):
    batch, seq, d = x.shape
    depth = ada_w.shape[0]
    t = batch * seq
    tm = min(512, seq)
    xt = x.reshape(t, d)

    mod_all = _ada_mod(c, ada_w, ada_b).reshape(depth, batch, 6, d)

    for l in range(depth):
        mod = mod_all[l]
        j = l // 2
        if l % 2 == 0:
            w_in = w_in_even[j]
            w_mla = _pad_cols(w_in[:, :MLA_COLS], MLA_COLS_PAD).astype(BF16)
            w_rwkv = w_in[:, MLA_COLS:].astype(BF16)
            p_mla, p_rwkv = _in_proj(xt, mod, norm_mix[l], [w_mla, w_rwkv],
                                     seq=seq, sh=0, sc=1, tm=tm)
            wqt, wk, wvt = _mla_weights(mla_w_uq[j], mla_w_ukv[j])
            qt, kh, vt = _mla_prep(p_mla, positions, _rope_inv_freq(),
                                   mla_q_norm[j].reshape(1, -1), mla_kv_norm[j].reshape(1, -1),
                                   wqt, wk, wvt, batch=batch, seq=seq, tm=tm, tk=ATTN_TILE)
            y_a = _attention(qt, kh, vt, batch=batch, seq=seq, tq=ATTN_TILE, nh=ATTN_HEADS)

            zeros_lo = jnp.zeros((DECAY_LORA, RWKV_DIM), F32)
            w2 = jnp.concatenate([rwkv_w2[j], zeros_lo], axis=0).astype(BF16)
            a2 = jnp.concatenate([zeros_lo, rwkv_a2[j]], axis=0).astype(BF16)
            vecs = jnp.stack([rwkv_w0[j], rwkv_a0[j], rwkv_k_k[j], rwkv_k_a[j],
                              rwkv_r_k[j].reshape(-1), rwkv_ln_w[j], rwkv_ln_b[j],
                              jnp.zeros((RWKV_DIM,), F32)])
            y_b = _rwkv(p_rwkv, rwkv_mu[j].reshape(1, -1), vecs, w2, a2,
                        rwkv_g2[j].astype(BF16), batch=batch, seq=seq,
                        n_chunks=min(RWKV_BLOCK_CHUNKS, seq // RWKV_CHUNK))
            w_out = w_out_even[j].astype(BF16)
            n_a = MLA_HEADS * MLA_V
            xt = _out_proj(xt, mod, [y_a, y_b], [w_out[:n_a], w_out[n_a:]],
                           seq=seq, gate=2, tm=tm)
        else:
            (proj,) = _in_proj(xt, mod, norm_mix[l], [w_in_odd[j].astype(BF16)],
                               seq=seq, sh=0, sc=1, tm=tm)
            y = _hgrn(proj, hg_lb_logits, hg_out_norm[j].reshape(1, -1),
                      batch=batch, seq=seq, layer=l)
            xt = _out_proj(xt, mod, [y], [w_out_odd[j].astype(BF16)], seq=seq, gate=2, tm=tm)
        xt = _ffn(xt, mod, norm_ffn[l], ffn_w_gate[l].astype(BF16), ffn_w_up[l].astype(BF16),
                  ffn_w_down[l].astype(BF16), final_norm, seq=seq, tm=tm,
                  final=(l == depth - 1))
    return xt.reshape(batch, seq, d)
```

```python
import functools

import numpy as np
import jax
import jax.numpy as jnp
from jax import lax
from jax.experimental import pallas as pl
from jax.experimental.pallas import tpu as pltpu

F32 = jnp.float32
BF16 = jnp.bfloat16

RMS_EPS = 1e-6
LOG2_E = 1.4426950408889634
LANES = 128
VMEM_LIMIT = 56 * 1024 * 1024

MLA_HEADS = 8
MLA_NOPE = 64
MLA_ROPE = 32
MLA_V = 64
Q_LORA = 384
KV_LORA = 256
ROPE_BASE = 10000.0
MLA_COLS = Q_LORA + KV_LORA + MLA_ROPE
MLA_COLS_PAD = 768
MLA_HEAD_PAD = 128
ATTN_TILE = 256
ATTN_HEADS = 4

RWKV_HEAD = 64
RWKV_DIM = 512
RWKV_HEADS = 8
DECAY_LORA = 64
AAA_LORA = 64
GATE_LORA = 128
RWKV_GN_EPS = RWKV_HEAD * 1e-5
RWKV_COLS = 3 * RWKV_DIM + DECAY_LORA + AAA_LORA + GATE_LORA
RWKV_CHUNK = 64
RWKV_BLOCK_CHUNKS = 4
RWKV_GROUP = 4

HG_K = 128
HG_HEADS = 8
HG_V = 128
HG_CHUNK = 128
HG_LEVELS = 7


def _params(*sem):
    return pltpu.CompilerParams(dimension_semantics=sem, vmem_limit_bytes=VMEM_LIMIT)


def _const_spec(shape):
    nd = len(shape)
    return pl.BlockSpec(shape, lambda *_: (0,) * nd, pipeline_mode=pl.Buffered(1))


def _dot(a, b):
    return jnp.dot(a, b, preferred_element_type=F32)


def _dot_nt(a, b):
    return lax.dot_general(a, b, (((1,), (1,)), ((), ())), preferred_element_type=F32)


def _dot_tn(a, b):
    return lax.dot_general(a, b, (((0,), (0,)), ((), ())), preferred_element_type=F32)


def _dot_split(sel, x):
    hi = x.astype(BF16)
    lo = (x - hi.astype(F32)).astype(BF16)
    return _dot(sel, hi) + _dot(sel, lo)


def _head_sum(x, ind):
    return _dot(x.astype(BF16), ind)


def _ada_kernel(c_ref, w_ref, b_ref, o_ref):
    c = c_ref[...]
    cond = c * jax.nn.sigmoid(c)
    o_ref[0] = _dot(cond.astype(BF16), w_ref[0].astype(BF16)) + b_ref[0]


def _ada_mod(c, ada_w, ada_b):
    depth, d, n = ada_w.shape
    b = c.shape[0]
    tn = n // 4
    return pl.pallas_call(
        _ada_kernel,
        grid=(depth, n // tn),
        in_specs=[
            pl.BlockSpec((b, d), lambda l, j: (0, 0)),
            pl.BlockSpec((1, d, tn), lambda l, j: (l, 0, j)),
            pl.BlockSpec((1, 1, tn), lambda l, j: (l, 0, j)),
        ],
        out_specs=pl.BlockSpec((1, b, tn), lambda l, j: (l, 0, j)),
        out_shape=jax.ShapeDtypeStruct((depth, b, n), F32),
        compiler_params=_params("arbitrary", "arbitrary"),
        name="ada_mod",
    )(c, ada_w, ada_b.reshape(depth, 1, n))


def _norm_mod(x, gain, mod, sh, sc):
    ms = jnp.mean(x * x, axis=-1, keepdims=True)
    y = x * lax.rsqrt(ms + RMS_EPS) * gain
    return y * (1.0 + mod[sc:sc + 1]) + mod[sh:sh + 1]


def _in_proj_kernel(x_ref, mod_ref, gain_ref, *refs, n_out, sh, sc):
    w_refs, o_refs = refs[:n_out], refs[n_out:]
    h = _norm_mod(x_ref[...], gain_ref[...], mod_ref[0], sh, sc).astype(BF16)
    for w_ref, o_ref in zip(w_refs, o_refs):
        o_ref[...] = _dot(h, w_ref[...]).astype(o_ref.dtype)


def _in_proj(x, mod, gain, weights, *, seq, sh, sc, tm, out_dtype):
    t, d = x.shape
    per_b = seq // tm
    n_out = len(weights)
    in_specs = [
        pl.BlockSpec((tm, d), lambda i: (i, 0)),
        pl.BlockSpec((1, 6, d), lambda i: (i // per_b, 0, 0)),
        _const_spec((1, d)),
    ] + [_const_spec(w.shape) for w in weights]
    out_specs = [pl.BlockSpec((tm, w.shape[1]), lambda i: (i, 0)) for w in weights]
    out_shape = [jax.ShapeDtypeStruct((t, w.shape[1]), out_dtype) for w in weights]
    return pl.pallas_call(
        functools.partial(_in_proj_kernel, n_out=n_out, sh=sh, sc=sc),
        grid=(t // tm,),
        in_specs=in_specs,
        out_specs=out_specs,
        out_shape=out_shape,
        compiler_params=_params("parallel"),
        name="in_proj",
    )(x, mod, gain.reshape(1, d), *weights)


def _ffn_kernel(x_ref, mod_ref, gain_ref, wg_ref, wu_ref, wd_ref, fin_ref, *refs, n_in, final):
    y_refs, w_refs, o_ref = refs[:n_in], refs[n_in:2 * n_in], refs[2 * n_in]
    x = x_ref[...]
    mod = mod_ref[0]
    if n_in:
        mix = _dot(y_refs[0][...], w_refs[0][...])
        for y_ref, w_ref in zip(y_refs[1:], w_refs[1:]):
            mix += _dot(y_ref[...], w_ref[...])
        x = x + mod[2:3] * mix
    h = _norm_mod(x, gain_ref[...], mod, 3, 4).astype(BF16)
    gate = _dot(h, wg_ref[...])
    up = _dot(h, wu_ref[...])
    act = (gate * jax.nn.sigmoid(gate) * up).astype(BF16)
    y = x + mod[5:6] * _dot(act, wd_ref[...])
    if final:
        ms = jnp.mean(y * y, axis=-1, keepdims=True)
        y = y * lax.rsqrt(ms + RMS_EPS) * fin_ref[...]
    o_ref[...] = y


def _ffn(x, mod, gain, wg, wu, wd, fin, ys, w_outs, *, seq, tm, final):
    t, d = x.shape
    per_b = seq // tm
    return pl.pallas_call(
        functools.partial(_ffn_kernel, n_in=len(ys), final=final),
        grid=(t // tm,),
        in_specs=[
            pl.BlockSpec((tm, d), lambda i: (i, 0)),
            pl.BlockSpec((1, 6, d), lambda i: (i // per_b, 0, 0)),
            _const_spec((1, d)),
            _const_spec(wg.shape),
            _const_spec(wu.shape),
            _const_spec(wd.shape),
            _const_spec((1, d)),
        ] + [pl.BlockSpec((tm, y.shape[1]), lambda i: (i, 0)) for y in ys]
          + [_const_spec(w.shape) for w in w_outs],
        out_specs=pl.BlockSpec((tm, d), lambda i: (i, 0)),
        out_shape=jax.ShapeDtypeStruct((t, d), F32),
        compiler_params=_params("parallel"),
        name="ffn",
    )(x, mod, gain.reshape(1, d), wg, wu, wd, fin.reshape(1, d), *ys, *w_outs)


def _mla_prep_kernel(p_ref, pos_ref, invf_ref, qn_ref, kvn_ref, wqt_ref, wk_ref, wvt_ref,
                     qt_ref, k_ref, vt_ref, *, scale, tk):
    p = p_ref[...].astype(F32)
    tm = p.shape[0]
    half = MLA_ROPE // 2
    ang = invf_ref[...] * pos_ref[0].astype(F32)
    cos_t, sin_t = jnp.cos(ang), jnp.sin(ang)

    c_q = p[:, :Q_LORA]
    ms = jnp.mean(c_q * c_q, axis=-1, keepdims=True)
    cqn = (c_q * lax.rsqrt(ms + RMS_EPS) * qn_ref[...]).astype(BF16)
    qt = _dot_nt(wqt_ref[...], cqn) * scale
    pieces = []
    for h in range(MLA_HEADS):
        base = h * MLA_HEAD_PAD
        x1 = qt[base + MLA_NOPE:base + MLA_NOPE + half]
        x2 = qt[base + MLA_NOPE + half:base + MLA_NOPE + MLA_ROPE]
        pieces += [qt[base:base + MLA_NOPE], x1 * cos_t - x2 * sin_t, x1 * sin_t + x2 * cos_t,
                   qt[base + MLA_NOPE + MLA_ROPE:base + MLA_HEAD_PAD]]
    qt_ref[0] = jnp.concatenate(pieces, axis=0).astype(qt_ref.dtype)

    ckv = p[:, Q_LORA:Q_LORA + KV_LORA]
    ss = jnp.mean(ckv * ckv, axis=-1, keepdims=True)
    ckvn = ckv * lax.rsqrt(ss + RMS_EPS) * kvn_ref[...]
    kr = p[:, Q_LORA + KV_LORA:]
    rest = kr.shape[1] - MLA_ROPE
    cos_k = jnp.concatenate([cos_t, cos_t, jnp.ones((rest, tm), F32)], axis=0).T
    sin_k = jnp.concatenate([sin_t, sin_t, jnp.zeros((rest, tm), F32)], axis=0).T
    lane = lax.broadcasted_iota(jnp.int32, kr.shape, 1)
    partner = jnp.where(lane < half, -pltpu.roll(kr, kr.shape[1] - half, 1), pltpu.roll(kr, half, 1))
    lhs = jnp.concatenate([ckvn, kr * cos_k + partner * sin_k], axis=1).astype(BF16)
    k_ref[...] = _dot(lhs, wk_ref[...]).astype(k_ref.dtype)
    vt = _dot_nt(wvt_ref[...], lhs).astype(vt_ref.dtype)
    for jj in range(tm // tk):
        vt_ref[0, jj] = vt[:, jj * tk:(jj + 1) * tk]


def _mla_prep(p_mla, pos, invf, q_norm, kv_norm, wqt, wk, wvt, *, batch, seq, tm, tk):
    t = p_mla.shape[0]
    n_q = MLA_HEADS * MLA_HEAD_PAD
    n_v = MLA_HEADS * MLA_V
    per_b = seq // tm
    scale = (MLA_NOPE + MLA_ROPE) ** -0.5 * LOG2_E
    return pl.pallas_call(
        functools.partial(_mla_prep_kernel, scale=scale, tk=tk),
        grid=(t // tm,),
        in_specs=[
            pl.BlockSpec((tm, MLA_COLS_PAD), lambda i: (i, 0)),
            pl.BlockSpec((1, 1, tm), lambda i: (i, 0, 0)),
            _const_spec(invf.shape),
            _const_spec((1, Q_LORA)),
            _const_spec((1, KV_LORA)),
            _const_spec(wqt.shape),
            _const_spec(wk.shape),
            _const_spec(wvt.shape),
        ],
        out_specs=[
            pl.BlockSpec((1, n_q, tm), lambda i: (i // per_b, 0, i % per_b)),
            pl.BlockSpec((tm, n_q), lambda i: (i, 0)),
            pl.BlockSpec((1, tm // tk, n_v, tk), lambda i: (i // per_b, i % per_b, 0, 0)),
        ],
        out_shape=[
            jax.ShapeDtypeStruct((batch, n_q, seq), BF16),
            jax.ShapeDtypeStruct((t, n_q), BF16),
            jax.ShapeDtypeStruct((batch, seq // tk, n_v, tk), BF16),
        ],
        compiler_params=_params("parallel"),
        name="mla_prep",
    )(p_mla, pos.reshape(t // tm, 1, tm), invf, q_norm, kv_norm, wqt, wk, wvt)


def _attn_kernel(qt_ref, k_ref, vt_ref, o_ref, s_ref, p_ref, *, tq, nh):
    i = pl.program_id(2)
    qts = [qt_ref[0, h * MLA_HEAD_PAD:(h + 1) * MLA_HEAD_PAD, :] for h in range(nh)]
    causal = (lax.broadcasted_iota(jnp.int32, (tq, tq), 0)
              <= lax.broadcasted_iota(jnp.int32, (tq, tq), 1))

    def put_scores(j, slot):
        start = pl.multiple_of(j * tq, tq)
        for h in range(nh):
            s_ref[slot, h] = _dot(
                k_ref[pl.ds(start, tq), h * MLA_HEAD_PAD:(h + 1) * MLA_HEAD_PAD], qts[h])

    def value_dots(j, probs):
        return [_dot(vt_ref[0, j, h * MLA_V:(h + 1) * MLA_V, :], probs[h]) for h in range(nh)]

    def step(carry, slot, pv_prev, masked):
        new, probs = [], []
        for h in range(nh):
            m, l, acc = carry[h]
            s = s_ref[slot, h]
            if masked:
                s = jnp.where(causal, s, -jnp.inf)
            m_new = jnp.maximum(m, jnp.max(s, axis=0, keepdims=True))
            pr = jnp.exp2(s - m_new)
            alpha = jnp.exp2(m - m_new)
            probs.append(pr.astype(BF16))
            new.append((m_new, alpha * l + jnp.sum(pr, axis=0, keepdims=True),
                        alpha * (acc + pv_prev[h])))
        return tuple(new), probs

    def parked(slot):
        return [p_ref[slot, h] for h in range(nh)]

    def pair(t, carry):
        j = 2 * t
        put_scores(j + 1, 1)
        pv = value_dots(jnp.maximum(j - 1, 0), parked(1))
        carry, probs = step(carry, 0, pv, False)
        for h in range(nh):
            p_ref[0, h] = probs[h]
        put_scores(j + 2, 0)
        pv = value_dots(j, parked(0))
        carry, probs = step(carry, 1, pv, False)
        for h in range(nh):
            p_ref[1, h] = probs[h]
        return carry

    def finish(carry, slot, pv_prev, tile):
        carry, probs = step(carry, slot, pv_prev, True)
        pv = value_dots(tile, probs)
        outs = [(acc + pv[h]) / l for h, (_, l, acc) in enumerate(carry)]
        o_ref[...] = jnp.concatenate(outs, axis=0).T.astype(o_ref.dtype)

    p_ref[1] = jnp.zeros(p_ref.shape[1:], BF16)
    put_scores(0, 0)
    init = tuple((jnp.full((1, tq), -jnp.inf, F32), jnp.zeros((1, tq), F32),
                  jnp.zeros((MLA_V, tq), F32)) for _ in range(nh))
    carry = lax.fori_loop(0, i // 2, pair, init)

    @pl.when(i % 2 == 0)
    def _():
        finish(carry, 0, value_dots(jnp.maximum(i - 1, 0), parked(1)), i)

    @pl.when(i % 2 == 1)
    def _():
        put_scores(i, 1)
        pv = value_dots(jnp.maximum(i - 2, 0), parked(1))
        mid, probs = step(carry, 0, pv, False)
        finish(mid, 1, value_dots(i - 1, probs), i)


def _attention(qt, k, vt, *, batch, seq, tq, nh):
    t = k.shape[0]
    nq = seq // tq
    return pl.pallas_call(
        functools.partial(_attn_kernel, tq=tq, nh=nh),
        grid=(batch, MLA_HEADS // nh, nq),
        in_specs=[
            pl.BlockSpec((1, nh * MLA_HEAD_PAD, tq), lambda b, h, i: (b, h, i)),
            pl.BlockSpec((seq, nh * MLA_HEAD_PAD), lambda b, h, i: (b, h)),
            pl.BlockSpec((1, seq // tq, nh * MLA_V, tq), lambda b, h, i: (b, 0, h, 0)),
        ],
        out_specs=pl.BlockSpec((tq, nh * MLA_V), lambda b, h, i: (b * nq + i, h)),
        out_shape=jax.ShapeDtypeStruct((t, MLA_HEADS * MLA_V), BF16),
        scratch_shapes=[pltpu.VMEM((2, nh, tq, tq), F32), pltpu.VMEM((2, nh, tq, tq), BF16)],
        compiler_params=_params("parallel", "parallel", "arbitrary"),
        name="mla_attention",
    )(qt, k, vt)


def _stack_heads(x, lane_head):
    return jnp.concatenate(
        [jnp.where(lane_head == h, x, 0.0) for h in range(RWKV_GROUP)], axis=0)


def _rwkv_kernel(p_ref, mu_ref, vec_ref, w2_ref, a2_ref, g2_ref, ind_ref, tri_ref,
                 o_ref, state_ref, prev_ref, *, n_chunks):
    c = RWKV_CHUNK
    gw = RWKV_GROUP * RWKV_HEAD
    gc = RWKV_GROUP * c
    n_groups = RWKV_HEADS // RWKV_GROUP
    tb = n_chunks * c

    @pl.when(pl.program_id(1) == 0)
    def _():
        state_ref[...] = jnp.zeros_like(state_ref)
        prev_ref[...] = jnp.zeros_like(prev_ref)

    p = p_ref[...].astype(F32)
    rows = lax.broadcasted_iota(jnp.int32, p.shape, 0)
    shifted = jnp.where(rows == 0, prev_ref[...], pltpu.roll(p, 1, 0))
    prev_ref[...] = p[tb - 1:tb, :]
    p = p + (shifted - p) * mu_ref[...]

    d = RWKV_DIM
    r, k, v = p[:, :d], p[:, d:2 * d], p[:, 2 * d:3 * d]
    lo = p[:, 3 * d:3 * d + DECAY_LORA + AAA_LORA]
    g_lo = p[:, 3 * d + DECAY_LORA + AAA_LORA:]
    w0, a0, k_k, k_a = vec_ref[0:1], vec_ref[1:2], vec_ref[2:3], vec_ref[3:4]
    r_k, ln_w, ln_b = vec_ref[4:5], vec_ref[5:6], vec_ref[6:7]

    z = -(w0 + _dot(jnp.tanh(lo).astype(BF16), w2_ref[...]))
    softplus = jnp.maximum(z, 0.0) + jnp.log(1.0 + jnp.exp(-jnp.abs(z)))
    lw = -jnp.exp(-softplus - 0.5)
    a = jax.nn.sigmoid(a0 + _dot(lo.astype(BF16), a2_ref[...]))
    g = _dot(jax.nn.sigmoid(g_lo).astype(BF16), g2_ref[...])

    ind = ind_ref[...]
    kk = k * k_k
    kk = kk / jnp.maximum(jnp.sqrt(_head_sum(kk * kk, ind)), 1e-12)
    k2 = k * (1.0 + (a - 1.0) * k_a)
    alpha = kk * a

    b = _dot_split(tri_ref[...], lw)
    b_prev = b - lw
    bcast = lambda row: jnp.broadcast_to(row, (c, d))
    b_mid = jnp.concatenate([bcast(b[ci * c + c // 2 - 1:ci * c + c // 2]) for ci in range(n_chunks)], axis=0)
    b_end = jnp.concatenate([bcast(b[ci * c + c - 1:ci * c + c]) for ci in range(n_chunks)], axis=0)
    kap_t = kk * jnp.exp(b_prev - b_mid)
    r_t = r * jnp.exp(b - b_mid)
    inv_t = jnp.exp(b_mid - b)
    alp_t, k_t = alpha * inv_t, k2 * inv_t
    kap_h = kk * jnp.exp(b_prev)
    r_h = r * jnp.exp(b)
    tail = jnp.exp(b_end - b)
    alp_c, k_c = alpha * tail, k2 * tail
    gam_end = jnp.exp(b_end)

    sr = lax.broadcasted_iota(jnp.int32, (gc, gc), 0)
    sc = lax.broadcasted_iota(jnp.int32, (gc, gc), 1)
    strict = (sc % c) < (sr % c)
    incl = (sc % c) <= (sr % c)
    eye = (sr == sc).astype(F32)
    lane_head = lax.broadcasted_iota(jnp.int32, (c, gw), 1) // RWKV_HEAD

    def stacked(x, ci, gi):
        return _stack_heads(x[ci * c:(ci + 1) * c, gi * gw:(gi + 1) * gw], lane_head)

    pairs = [(ci, gi) for ci in range(n_chunks) for gi in range(n_groups)]
    amat, vs, n_pow, inv = {}, {}, {}, {}
    for pr in pairs:
        st = lambda x: stacked(x, *pr)
        kr = jnp.concatenate([st(kap_t), st(r_t)], axis=0).astype(BF16)
        ak = jnp.concatenate([st(alp_t), st(k_t)], axis=0).astype(BF16)
        amat[pr] = _dot_nt(kr, ak)
        vs[pr] = st(v).astype(BF16)
    for pr in pairs:
        n_pow[pr] = jnp.where(strict, amat[pr][:gc, :gc], 0.0)
        inv[pr] = eye - n_pow[pr]
    for _ in range(int(np.log2(c)) - 1):
        for pr in pairs:
            pwb = n_pow[pr].astype(BF16)
            n_pow[pr] = _dot(pwb, pwb)
        for pr in pairs:
            inv[pr] = inv[pr] + _dot(inv[pr].astype(BF16), n_pow[pr].astype(BF16))
    akv = {pr: _dot(jnp.where(strict, amat[pr][:gc, gc:], 0.0).astype(BF16), vs[pr]).astype(BF16)
           for pr in pairs}
    pre = {}
    for pr in pairs:
        st = lambda x: stacked(x, *pr)
        invb = inv[pr].astype(BF16)
        kt = _dot(invb, st(kap_h).astype(BF16)).astype(BF16)
        w_mat = _dot(invb, akv[pr])
        a_ra = jnp.where(incl, amat[pr][gc:, :gc], 0.0).astype(BF16)
        a_rk = jnp.where(incl, amat[pr][gc:, gc:], 0.0).astype(BF16)
        akc = jnp.concatenate([st(alp_c), st(k_c)], axis=0).astype(BF16)
        pre[pr] = (kt, w_mat, a_ra, a_rk, st(r_h).astype(BF16), akc)

    y_rows = []
    for ci in range(n_chunks):
        s0 = [state_ref[gi] for gi in range(n_groups)]
        s0b = [x.astype(BF16) for x in s0]
        ub = [(-(_dot_nt(pre[ci, gi][0], s0b[gi]) + pre[ci, gi][1])).astype(BF16)
              for gi in range(n_groups)]
        ys = []
        for gi in range(n_groups):
            kt, w_mat, a_ra, a_rk, rh, akc = pre[ci, gi]
            uv = jnp.concatenate([ub[gi], vs[ci, gi]], axis=0)
            sl = slice(gi * gw, (gi + 1) * gw)
            state_ref[gi] = s0[gi] * gam_end[ci * c:ci * c + 1, sl] + _dot_tn(uv, akc)
            y_st = _dot_nt(rh, s0b[gi]) + _dot(a_ra, ub[gi]) + _dot(a_rk, vs[ci, gi])
            y = y_st[0:c]
            for h in range(1, RWKV_GROUP):
                y = y + y_st[h * c:(h + 1) * c]
            ys.append(y)
        y_rows.append(jnp.concatenate(ys, axis=1))
    y = jnp.concatenate(y_rows, axis=0)

    inv_n = 1.0 / RWKV_HEAD
    mean = _head_sum(y, ind) * inv_n
    yc = y - mean
    var = _head_sum(yc * yc, ind) * inv_n
    y = yc * lax.rsqrt(var + RWKV_GN_EPS) * ln_w + ln_b
    bonus = _head_sum(r * k2 * r_k, ind) * v
    o_ref[...] = ((y + bonus) * g).astype(o_ref.dtype)


def _rwkv(p_rwkv, mu, vecs, w2, a2, g2, *, batch, seq, n_chunks):
    t = p_rwkv.shape[0]
    c = RWKV_CHUNK
    tb = n_chunks * c
    nb = seq // tb
    gw = RWKV_GROUP * RWKV_HEAD
    head = np.arange(RWKV_DIM) // RWKV_HEAD
    ind = jnp.asarray(head[:, None] == head[None, :], BF16)
    tt = np.arange(tb)
    tri = jnp.asarray((tt[None, :] <= tt[:, None]) & (tt[None, :] // c == tt[:, None] // c), BF16)
    return pl.pallas_call(
        functools.partial(_rwkv_kernel, n_chunks=n_chunks),
        grid=(batch, nb),
        in_specs=[
            pl.BlockSpec((tb, RWKV_COLS), lambda b, j: (b * nb + j, 0)),
            _const_spec((1, RWKV_COLS)),
            _const_spec(vecs.shape),
            _const_spec(w2.shape),
            _const_spec(a2.shape),
            _const_spec(g2.shape),
            _const_spec(ind.shape),
            _const_spec(tri.shape),
        ],
        out_specs=pl.BlockSpec((tb, RWKV_DIM), lambda b, j: (b * nb + j, 0)),
        out_shape=jax.ShapeDtypeStruct((t, RWKV_DIM), BF16),
        scratch_shapes=[
            pltpu.VMEM((RWKV_HEADS // RWKV_GROUP, gw, gw), F32),
            pltpu.VMEM((1, RWKV_COLS), F32),
        ],
        compiler_params=_params("parallel", "arbitrary"),
        name="rwkv7",
    )(p_rwkv, mu, vecs, w2, a2, g2, ind, tri)


def _hgrn_masks():
    c = HG_CHUNK
    t = np.arange(c)
    mask = np.zeros((HG_LEVELS + 1, c, c), np.float32)
    mask[0] = np.eye(c)
    for l in range(1, HG_LEVELS + 1):
        m = 1 << l
        is_right = (t % m) >= m // 2
        same = (t[:, None] // m) == (t[None, :] // m)
        mask[l] = same & is_right[:, None] & ~is_right[None, :]
    return mask


def _hgrn_layer_kernel(x_ref, mod_ref, ngain_ref, win_ref, lbl_ref, gain_ref, tri_ref, mask_ref,
                       wout_ref, o_ref, state_ref, *, layer):
    c = HG_CHUNK
    n = HG_HEADS * HG_K

    @pl.when(pl.program_id(1) == 0)
    def _():
        state_ref[...] = jnp.zeros_like(state_ref)

    x = x_ref[...]
    mod = mod_ref[0]
    hin = _norm_mod(x, ngain_ref[...], mod, 0, 1).astype(BF16)
    q = _dot(hin, win_ref[:, 0:n])
    f = _dot(hin, win_ref[:, n:2 * n])
    vals = _dot(hin, win_ref[:, 2 * n:3 * n])
    g = _dot(hin, win_ref[:, 3 * n:4 * n])

    logits = lbl_ref[...]
    pe = jnp.exp(logits - jnp.max(logits, axis=0, keepdims=True))
    lb = jnp.sum(pe[1:layer + 1], axis=0, keepdims=True) / jnp.sum(pe, axis=0, keepdims=True)

    q = q * jax.nn.sigmoid(q)
    forget = lb + (1.0 - lb) * jax.nn.sigmoid(f)
    key = 1.0 - forget
    logf = jnp.log(forget)

    b = _dot_split(tri_ref[...], logf)
    e_cum = jnp.exp(b)
    b_end = b[c - 1:c, :]
    q_hat = (q * e_cum).astype(BF16)
    k_hat = (key * jnp.exp(b_end - b)).astype(BF16)
    decay_end = e_cum[c - 1:c, :]
    qb, kb = q.astype(BF16), key.astype(BF16)

    t_idx = lax.broadcasted_iota(jnp.int32, q.shape, 0)
    f_prev = pltpu.roll(forget, 1, 0)
    f_next = pltpu.roll(forget, c - 1, 0)
    level_ops = []
    for l in range(HG_LEVELS):
        m = 2 << l
        is_right = (t_idx & (m // 2)) != 0
        if m == 2:
            decay = jnp.where(is_right, forget, 1.0)
        elif m == 4:
            pos = t_idx & 3
            decay = jnp.where(pos == 0, f_next,
                              jnp.where(pos == 1, 1.0, jnp.where(pos == 2, forget, forget * f_prev)))
        else:
            b3 = b.reshape(c // m, m, n)
            b_ref = jnp.broadcast_to(b3[:, m // 2 - 1:m // 2, :], (c // m, m, n)).reshape(c, n)
            decay = jnp.exp(-jnp.abs(b - b_ref))
        level_ops.append((jnp.where(is_right, q, key) * decay).astype(BF16))

    outs = []
    for h in range(HG_HEADS):
        sl = slice(h * HG_K, (h + 1) * HG_K)
        attn = mask_ref[0] * _dot_nt(qb[:, sl], kb[:, sl])
        for l in range(HG_LEVELS):
            ml = level_ops[l][:, sl]
            attn = attn + mask_ref[l + 1] * _dot_nt(ml, ml)
        vh = vals[:, sl].astype(BF16)
        st = state_ref[h]
        o = _dot_nt(q_hat[:, sl], st.astype(BF16)) + _dot(attn.astype(BF16), vh)
        state_ref[h] = st * decay_end[:, sl] + _dot_tn(vh, k_hat[:, sl])
        ms = jnp.mean(o * o, axis=-1, keepdims=True)
        gt = g[:, sl]
        outs.append((o * lax.rsqrt(ms + RMS_EPS) * gain_ref[...]
                     * (gt * jax.nn.sigmoid(gt))).astype(BF16))
    y = jnp.concatenate(outs, axis=1)
    o_ref[...] = x + mod[2:3] * _dot(y, wout_ref[...])


def _hgrn_layer(x, mod, norm_gain, w_in, lb_logits, gain, w_out, *, batch, seq, layer):
    t, d = x.shape
    c = HG_CHUNK
    nc = seq // c
    tri = jnp.asarray(np.tril(np.ones((c, c))), BF16)
    mask = jnp.asarray(_hgrn_masks(), F32)
    return pl.pallas_call(
        functools.partial(_hgrn_layer_kernel, layer=layer),
        grid=(batch, nc),
        in_specs=[pl.BlockSpec((c, d), lambda b, j: (b * nc + j, 0)),
                  pl.BlockSpec((1, 6, d), lambda b, j: (b, 0, 0)),
                  _const_spec((1, d)), _const_spec(w_in.shape),
                  _const_spec(lb_logits.shape), _const_spec((1, HG_V)),
                  _const_spec(tri.shape), _const_spec(mask.shape), _const_spec(w_out.shape)],
        out_specs=pl.BlockSpec((c, d), lambda b, j: (b * nc + j, 0)),
        out_shape=jax.ShapeDtypeStruct((t, d), F32),
        scratch_shapes=[pltpu.VMEM((HG_HEADS, HG_V, HG_K), F32)],
        compiler_params=_params("parallel", "arbitrary"),
        name="hgrn2_layer",
    )(x, mod, norm_gain.reshape(1, d), w_in, lb_logits, gain, tri, mask, w_out)


def _pad_cols(w, n):
    return jnp.pad(w, ((0, 0), (0, n - w.shape[1])))


def _mla_weights(w_uq, w_ukv):
    hq = MLA_NOPE + MLA_ROPE
    wq = w_uq.reshape(Q_LORA, MLA_HEADS, hq)
    wq = jnp.pad(wq, ((0, 0), (0, 0), (0, MLA_HEAD_PAD - hq))).reshape(Q_LORA, -1)
    wkv = w_ukv.reshape(KV_LORA, MLA_HEADS, MLA_NOPE + MLA_V)
    wk = jnp.pad(wkv[:, :, :MLA_NOPE], ((0, 0), (0, 0), (0, MLA_HEAD_PAD - MLA_NOPE)))
    wk = wk.reshape(KV_LORA, -1)
    wv = wkv[:, :, MLA_NOPE:].reshape(KV_LORA, -1)
    n_extra = MLA_COLS_PAD - Q_LORA - KV_LORA
    place = np.zeros((n_extra, MLA_HEADS, MLA_HEAD_PAD), np.float32)
    for r in range(MLA_ROPE):
        place[r, :, MLA_NOPE + r] = 1.0
    wk_full = jnp.concatenate([wk, jnp.asarray(place.reshape(n_extra, -1))], axis=0)
    wv_full = jnp.concatenate([wv, jnp.zeros((n_extra, wv.shape[1]), F32)], axis=0)
    return wq.T.astype(BF16), wk_full.astype(BF16), wv_full.T.astype(BF16)


def _rope_inv_freq():
    inv = 1.0 / (ROPE_BASE ** (jnp.arange(0, MLA_ROPE, 2, dtype=F32) / MLA_ROPE))
    return inv.reshape(MLA_ROPE // 2, 1)


def kernel(x, c, positions, ada_w, ada_b, norm_mix, norm_ffn, w_in_even, mla_q_norm, mla_w_uq, mla_kv_norm, mla_w_ukv, rwkv_mu, rwkv_w0, rwkv_w2, rwkv_a0, rwkv_a2, rwkv_g2, rwkv_k_k, rwkv_k_a, rwkv_r_k, rwkv_ln_w, rwkv_ln_b, w_out_even, w_in_odd, hg_lb_logits, hg_out_norm, w_out_odd, ffn_w_gate, ffn_w_up, ffn_w_down, final_norm):
    batch, seq, d = x.shape
    depth = ada_w.shape[0]
    t = batch * seq
    tm = min(512, seq)
    xt = x.reshape(t, d)

    mod_all = _ada_mod(c, ada_w, ada_b).reshape(depth, batch, 6, d)

    for l in range(depth):
        mod = mod_all[l]
        j = l // 2
        if l % 2 == 0:
            w_in = w_in_even[j]
            w_mla = _pad_cols(w_in[:, :MLA_COLS], MLA_COLS_PAD).astype(BF16)
            w_rwkv = w_in[:, MLA_COLS:].astype(BF16)
            p_mla, p_rwkv = _in_proj(xt, mod, norm_mix[l], [w_mla, w_rwkv],
                                     seq=seq, sh=0, sc=1, tm=tm, out_dtype=BF16)
            wqt, wk, wvt = _mla_weights(mla_w_uq[j], mla_w_ukv[j])
            qt, kh, vt = _mla_prep(p_mla, positions, _rope_inv_freq(),
                                   mla_q_norm[j].reshape(1, -1), mla_kv_norm[j].reshape(1, -1),
                                   wqt, wk, wvt, batch=batch, seq=seq, tm=tm, tk=ATTN_TILE)
            y_a = _attention(qt, kh, vt, batch=batch, seq=seq, tq=ATTN_TILE, nh=ATTN_HEADS)

            zeros_lo = jnp.zeros((DECAY_LORA, RWKV_DIM), F32)
            w2 = jnp.concatenate([rwkv_w2[j], zeros_lo], axis=0).astype(BF16)
            a2 = jnp.concatenate([zeros_lo, rwkv_a2[j]], axis=0).astype(BF16)
            vecs = jnp.stack([rwkv_w0[j], rwkv_a0[j], rwkv_k_k[j], rwkv_k_a[j],
                              rwkv_r_k[j].reshape(-1), rwkv_ln_w[j], rwkv_ln_b[j],
                              jnp.zeros((RWKV_DIM,), F32)])
            y_b = _rwkv(p_rwkv, rwkv_mu[j].reshape(1, -1), vecs, w2, a2,
                        rwkv_g2[j].astype(BF16), batch=batch, seq=seq,
                        n_chunks=min(RWKV_BLOCK_CHUNKS, seq // RWKV_CHUNK))
            w_out = w_out_even[j].astype(BF16)
            n_a = MLA_HEADS * MLA_V
            ys, w_outs = [y_a, y_b], [w_out[:n_a], w_out[n_a:]]
        else:
            xt = _hgrn_layer(xt, mod, norm_mix[l], w_in_odd[j].astype(BF16), hg_lb_logits,
                             hg_out_norm[j].reshape(1, -1), w_out_odd[j].astype(BF16),
                             batch=batch, seq=seq, layer=l)
            ys, w_outs = [], []
        xt = _ffn(xt, mod, norm_ffn[l], ffn_w_gate[l].astype(BF16), ffn_w_up[l].astype(BF16),
                  ffn_w_down[l].astype(BF16), final_norm, ys, w_outs, seq=seq, tm=tm,
                  final=(l == depth - 1))
    return xt.reshape(batch, seq, d)
```

```python
import functools

import numpy as np
import jax
import jax.numpy as jnp
from jax import lax
from jax.experimental import pallas as pl
from jax.experimental.pallas import tpu as pltpu

F32 = jnp.float32
BF16 = jnp.bfloat16

RMS_EPS = 1e-6
LOG2_E = 1.4426950408889634
LANES = 128
VMEM_LIMIT = 56 * 1024 * 1024

MLA_HEADS = 8
MLA_NOPE = 64
MLA_ROPE = 32
MLA_V = 64
Q_LORA = 384
KV_LORA = 256
ROPE_BASE = 10000.0
MLA_COLS = Q_LORA + KV_LORA + MLA_ROPE
MLA_COLS_PAD = 768
MLA_HEAD_PAD = 128
ATTN_TILE = 256
ATTN_HEADS = 4

RWKV_HEAD = 64
RWKV_DIM = 512
RWKV_HEADS = 8
DECAY_LORA = 64
AAA_LORA = 64
GATE_LORA = 128
RWKV_GN_EPS = RWKV_HEAD * 1e-5
RWKV_COLS = 3 * RWKV_DIM + DECAY_LORA + AAA_LORA + GATE_LORA
RWKV_CHUNK = 64
RWKV_BLOCK_CHUNKS = 4
RWKV_GROUP = 4

HG_K = 128
HG_HEADS = 8
HG_V = 128
HG_CHUNK = 128
HG_LEVELS = 7
HG_PROJ_PIECE = 256


def _params(*sem):
    return pltpu.CompilerParams(dimension_semantics=sem, vmem_limit_bytes=VMEM_LIMIT)


def _const_spec(shape):
    nd = len(shape)
    return pl.BlockSpec(shape, lambda *_: (0,) * nd, pipeline_mode=pl.Buffered(1))


def _dot(a, b):
    return jnp.dot(a, b, preferred_element_type=F32)


def _dot_nt(a, b):
    return lax.dot_general(a, b, (((1,), (1,)), ((), ())), preferred_element_type=F32)


def _dot_tn(a, b):
    return lax.dot_general(a, b, (((0,), (0,)), ((), ())), preferred_element_type=F32)


def _dot_split(sel, x):
    hi = x.astype(BF16)
    lo = (x - hi.astype(F32)).astype(BF16)
    return _dot(sel, hi) + _dot(sel, lo)


def _head_sum(x, ind):
    return _dot(x.astype(BF16), ind)


def _ada_kernel(c_ref, w_ref, b_ref, o_ref):
    c = c_ref[...]
    cond = c * jax.nn.sigmoid(c)
    o_ref[0] = _dot(cond.astype(BF16), w_ref[0].astype(BF16)) + b_ref[0]


def _ada_mod(c, ada_w, ada_b):
    depth, d, n = ada_w.shape
    b = c.shape[0]
    tn = n // 4
    return pl.pallas_call(
        _ada_kernel,
        grid=(depth, n // tn),
        in_specs=[
            pl.BlockSpec((b, d), lambda l, j: (0, 0)),
            pl.BlockSpec((1, d, tn), lambda l, j: (l, 0, j)),
            pl.BlockSpec((1, 1, tn), lambda l, j: (l, 0, j)),
        ],
        out_specs=pl.BlockSpec((1, b, tn), lambda l, j: (l, 0, j)),
        out_shape=jax.ShapeDtypeStruct((depth, b, n), F32),
        compiler_params=_params("arbitrary", "arbitrary"),
        name="ada_mod",
    )(c, ada_w, ada_b.reshape(depth, 1, n))


def _norm_mod(x, gain, mod, sh, sc):
    ms = jnp.mean(x * x, axis=-1, keepdims=True)
    y = x * lax.rsqrt(ms + RMS_EPS) * gain
    return y * (1.0 + mod[sc:sc + 1]) + mod[sh:sh + 1]


def _in_proj_kernel(x_ref, mod_ref, gain_ref, *refs, n_out, sh, sc):
    w_refs, o_refs = refs[:n_out], refs[n_out:]
    h = _norm_mod(x_ref[...], gain_ref[...], mod_ref[0], sh, sc).astype(BF16)
    for w_ref, o_ref in zip(w_refs, o_refs):
        o_ref[...] = _dot(h, w_ref[...]).astype(o_ref.dtype)


def _in_proj(x, mod, gain, weights, *, seq, sh, sc, tm, out_dtype):
    t, d = x.shape
    per_b = seq // tm
    n_out = len(weights)
    in_specs = [
        pl.BlockSpec((tm, d), lambda i: (i, 0)),
        pl.BlockSpec((1, 6, d), lambda i: (i // per_b, 0, 0)),
        _const_spec((1, d)),
    ] + [_const_spec(w.shape) for w in weights]
    out_specs = [pl.BlockSpec((tm, w.shape[1]), lambda i: (i, 0)) for w in weights]
    out_shape = [jax.ShapeDtypeStruct((t, w.shape[1]), out_dtype) for w in weights]
    return pl.pallas_call(
        functools.partial(_in_proj_kernel, n_out=n_out, sh=sh, sc=sc),
        grid=(t // tm,),
        in_specs=in_specs,
        out_specs=out_specs,
        out_shape=out_shape,
        compiler_params=_params("parallel"),
        name="in_proj",
    )(x, mod, gain.reshape(1, d), *weights)


def _ffn_kernel(x_ref, mod_ref, gain_ref, wg_ref, wu_ref, wd_ref, fin_ref, *refs, n_in, final):
    y_refs, w_refs, o_ref = refs[:n_in], refs[n_in:2 * n_in], refs[2 * n_in]
    x = x_ref[...]
    mod = mod_ref[0]
    if n_in:
        mix = _dot(y_refs[0][...], w_refs[0][...])
        for y_ref, w_ref in zip(y_refs[1:], w_refs[1:]):
            mix += _dot(y_ref[...], w_ref[...])
        x = x + mod[2:3] * mix
    h = _norm_mod(x, gain_ref[...], mod, 3, 4).astype(BF16)
    gate = _dot(h, wg_ref[...])
    up = _dot(h, wu_ref[...])
    act = (gate * jax.nn.sigmoid(gate) * up).astype(BF16)
    y = x + mod[5:6] * _dot(act, wd_ref[...])
    if final:
        ms = jnp.mean(y * y, axis=-1, keepdims=True)
        y = y * lax.rsqrt(ms + RMS_EPS) * fin_ref[...]
    o_ref[...] = y


def _ffn(x, mod, gain, wg, wu, wd, fin, ys, w_outs, *, seq, tm, final):
    t, d = x.shape
    per_b = seq // tm
    return pl.pallas_call(
        functools.partial(_ffn_kernel, n_in=len(ys), final=final),
        grid=(t // tm,),
        in_specs=[
            pl.BlockSpec((tm, d), lambda i: (i, 0)),
            pl.BlockSpec((1, 6, d), lambda i: (i // per_b, 0, 0)),
            _const_spec((1, d)),
            _const_spec(wg.shape),
            _const_spec(wu.shape),
            _const_spec(wd.shape),
            _const_spec((1, d)),
        ] + [pl.BlockSpec((tm, y.shape[1]), lambda i: (i, 0)) for y in ys]
          + [_const_spec(w.shape) for w in w_outs],
        out_specs=pl.BlockSpec((tm, d), lambda i: (i, 0)),
        out_shape=jax.ShapeDtypeStruct((t, d), F32),
        compiler_params=_params("parallel"),
        name="ffn",
    )(x, mod, gain.reshape(1, d), wg, wu, wd, fin.reshape(1, d), *ys, *w_outs)


def _mla_prep_kernel(p_ref, pos_ref, invf_ref, qn_ref, kvn_ref, wqt_ref, wk_ref, wvt_ref,
                     qt_ref, k_ref, vt_ref, *, scale, tk):
    p = p_ref[...].astype(F32)
    tm = p.shape[0]
    half = MLA_ROPE // 2
    ang = invf_ref[...] * pos_ref[0].astype(F32)
    cos_t, sin_t = jnp.cos(ang), jnp.sin(ang)

    c_q = p[:, :Q_LORA]
    ms = jnp.mean(c_q * c_q, axis=-1, keepdims=True)
    cqn = (c_q * lax.rsqrt(ms + RMS_EPS) * qn_ref[...]).astype(BF16)
    qt = _dot_nt(wqt_ref[...], cqn) * scale
    pieces = []
    for h in range(MLA_HEADS):
        base = h * MLA_HEAD_PAD
        x1 = qt[base + MLA_NOPE:base + MLA_NOPE + half]
        x2 = qt[base + MLA_NOPE + half:base + MLA_NOPE + MLA_ROPE]
        pieces += [qt[base:base + MLA_NOPE], x1 * cos_t - x2 * sin_t, x1 * sin_t + x2 * cos_t,
                   qt[base + MLA_NOPE + MLA_ROPE:base + MLA_HEAD_PAD]]
    qt_ref[0] = jnp.concatenate(pieces, axis=0).astype(qt_ref.dtype)

    ckv = p[:, Q_LORA:Q_LORA + KV_LORA]
    ss = jnp.mean(ckv * ckv, axis=-1, keepdims=True)
    ckvn = ckv * lax.rsqrt(ss + RMS_EPS) * kvn_ref[...]
    kr = p[:, Q_LORA + KV_LORA:]
    rest = kr.shape[1] - MLA_ROPE
    cos_k = jnp.concatenate([cos_t, cos_t, jnp.ones((rest, tm), F32)], axis=0).T
    sin_k = jnp.concatenate([sin_t, sin_t, jnp.zeros((rest, tm), F32)], axis=0).T
    lane = lax.broadcasted_iota(jnp.int32, kr.shape, 1)
    partner = jnp.where(lane < half, -pltpu.roll(kr, kr.shape[1] - half, 1), pltpu.roll(kr, half, 1))
    lhs = jnp.concatenate([ckvn, kr * cos_k + partner * sin_k], axis=1).astype(BF16)
    k_ref[...] = _dot(lhs, wk_ref[...]).astype(k_ref.dtype)
    vt = _dot_nt(wvt_ref[...], lhs).astype(vt_ref.dtype)
    for jj in range(tm // tk):
        vt_ref[0, jj] = vt[:, jj * tk:(jj + 1) * tk]


def _mla_prep(p_mla, pos, invf, q_norm, kv_norm, wqt, wk, wvt, *, batch, seq, tm, tk):
    t = p_mla.shape[0]
    n_q = MLA_HEADS * MLA_HEAD_PAD
    n_v = MLA_HEADS * MLA_V
    per_b = seq // tm
    scale = (MLA_NOPE + MLA_ROPE) ** -0.5 * LOG2_E
    return pl.pallas_call(
        functools.partial(_mla_prep_kernel, scale=scale, tk=tk),
        grid=(t // tm,),
        in_specs=[
            pl.BlockSpec((tm, MLA_COLS_PAD), lambda i: (i, 0)),
            pl.BlockSpec((1, 1, tm), lambda i: (i, 0, 0)),
            _const_spec(invf.shape),
            _const_spec((1, Q_LORA)),
            _const_spec((1, KV_LORA)),
            _const_spec(wqt.shape),
            _const_spec(wk.shape),
            _const_spec(wvt.shape),
        ],
        out_specs=[
            pl.BlockSpec((1, n_q, tm), lambda i: (i // per_b, 0, i % per_b)),
            pl.BlockSpec((tm, n_q), lambda i: (i, 0)),
            pl.BlockSpec((1, tm // tk, n_v, tk), lambda i: (i // per_b, i % per_b, 0, 0)),
        ],
        out_shape=[
            jax.ShapeDtypeStruct((batch, n_q, seq), BF16),
            jax.ShapeDtypeStruct((t, n_q), BF16),
            jax.ShapeDtypeStruct((batch, seq // tk, n_v, tk), BF16),
        ],
        compiler_params=_params("parallel"),
        name="mla_prep",
    )(p_mla, pos.reshape(t // tm, 1, tm), invf, q_norm, kv_norm, wqt, wk, wvt)


def _attn_kernel(qt_ref, k_ref, vt_ref, o_ref, s_ref, p_ref, *, tq, nh):
    i = pl.program_id(2)
    qts = [qt_ref[0, h * MLA_HEAD_PAD:(h + 1) * MLA_HEAD_PAD, :] for h in range(nh)]
    causal = (lax.broadcasted_iota(jnp.int32, (tq, tq), 0)
              <= lax.broadcasted_iota(jnp.int32, (tq, tq), 1))

    def put_scores(j, slot):
        start = pl.multiple_of(j * tq, tq)
        for h in range(nh):
            s_ref[slot, h] = _dot(
                k_ref[pl.ds(start, tq), h * MLA_HEAD_PAD:(h + 1) * MLA_HEAD_PAD], qts[h])

    def value_dots(j, probs):
        return [_dot(vt_ref[0, j, h * MLA_V:(h + 1) * MLA_V, :], probs[h]) for h in range(nh)]

    def step(carry, slot, pv_prev, masked):
        new, probs = [], []
        for h in range(nh):
            m, l, acc = carry[h]
            s = s_ref[slot, h]
            if masked:
                s = jnp.where(causal, s, -jnp.inf)
            m_new = jnp.maximum(m, jnp.max(s, axis=0, keepdims=True))
            pr = jnp.exp2(s - m_new)
            alpha = jnp.exp2(m - m_new)
            probs.append(pr.astype(BF16))
            new.append((m_new, alpha * l + jnp.sum(pr, axis=0, keepdims=True),
                        alpha * (acc + pv_prev[h])))
        return tuple(new), probs

    def parked(slot):
        return [p_ref[slot, h] for h in range(nh)]

    def pair(t, carry):
        j = 2 * t
        put_scores(j + 1, 1)
        pv = value_dots(jnp.maximum(j - 1, 0), parked(1))
        carry, probs = step(carry, 0, pv, False)
        for h in range(nh):
            p_ref[0, h] = probs[h]
        put_scores(j + 2, 0)
        pv = value_dots(j, parked(0))
        carry, probs = step(carry, 1, pv, False)
        for h in range(nh):
            p_ref[1, h] = probs[h]
        return carry

    def finish(carry, slot, pv_prev, tile):
        carry, probs = step(carry, slot, pv_prev, True)
        pv = value_dots(tile, probs)
        outs = [(acc + pv[h]) / l for h, (_, l, acc) in enumerate(carry)]
        o_ref[...] = jnp.concatenate(outs, axis=0).T.astype(o_ref.dtype)

    p_ref[1] = jnp.zeros(p_ref.shape[1:], BF16)
    put_scores(0, 0)
    init = tuple((jnp.full((1, tq), -jnp.inf, F32), jnp.zeros((1, tq), F32),
                  jnp.zeros((MLA_V, tq), F32)) for _ in range(nh))
    carry = lax.fori_loop(0, i // 2, pair, init)

    @pl.when(i % 2 == 0)
    def _():
        finish(carry, 0, value_dots(jnp.maximum(i - 1, 0), parked(1)), i)

    @pl.when(i % 2 == 1)
    def _():
        put_scores(i, 1)
        pv = value_dots(jnp.maximum(i - 2, 0), parked(1))
        mid, probs = step(carry, 0, pv, False)
        finish(mid, 1, value_dots(i - 1, probs), i)


def _attention(qt, k, vt, *, batch, seq, tq, nh):
    t = k.shape[0]
    nq = seq // tq
    return pl.pallas_call(
        functools.partial(_attn_kernel, tq=tq, nh=nh),
        grid=(batch, MLA_HEADS // nh, nq),
        in_specs=[
            pl.BlockSpec((1, nh * MLA_HEAD_PAD, tq), lambda b, h, i: (b, h, i)),
            pl.BlockSpec((seq, nh * MLA_HEAD_PAD), lambda b, h, i: (b, h)),
            pl.BlockSpec((1, seq // tq, nh * MLA_V, tq), lambda b, h, i: (b, 0, h, 0)),
        ],
        out_specs=pl.BlockSpec((tq, nh * MLA_V), lambda b, h, i: (b * nq + i, h)),
        out_shape=jax.ShapeDtypeStruct((t, MLA_HEADS * MLA_V), BF16),
        scratch_shapes=[pltpu.VMEM((2, nh, tq, tq), F32), pltpu.VMEM((2, nh, tq, tq), BF16)],
        compiler_params=_params("parallel", "parallel", "arbitrary"),
        name="mla_attention",
    )(qt, k, vt)


def _stack_heads(x, lane_head):
    return jnp.concatenate(
        [jnp.where(lane_head == h, x, 0.0) for h in range(RWKV_GROUP)], axis=0)


def _rwkv_kernel(p_ref, mu_ref, vec_ref, w2_ref, a2_ref, g2_ref, ind_ref, tri_ref,
                 o_ref, state_ref, prev_ref, *, n_chunks):
    c = RWKV_CHUNK
    gw = RWKV_GROUP * RWKV_HEAD
    gc = RWKV_GROUP * c
    n_groups = RWKV_HEADS // RWKV_GROUP
    tb = n_chunks * c

    @pl.when(pl.program_id(1) == 0)
    def _():
        state_ref[...] = jnp.zeros_like(state_ref)
        prev_ref[...] = jnp.zeros_like(prev_ref)

    p = p_ref[...].astype(F32)
    rows = lax.broadcasted_iota(jnp.int32, p.shape, 0)
    shifted = jnp.where(rows == 0, prev_ref[...], pltpu.roll(p, 1, 0))
    prev_ref[...] = p[tb - 1:tb, :]
    p = p + (shifted - p) * mu_ref[...]

    d = RWKV_DIM
    r, k, v = p[:, :d], p[:, d:2 * d], p[:, 2 * d:3 * d]
    lo = p[:, 3 * d:3 * d + DECAY_LORA + AAA_LORA]
    g_lo = p[:, 3 * d + DECAY_LORA + AAA_LORA:]
    w0, a0, k_k, k_a = vec_ref[0:1], vec_ref[1:2], vec_ref[2:3], vec_ref[3:4]
    r_k, ln_w, ln_b = vec_ref[4:5], vec_ref[5:6], vec_ref[6:7]

    z = -(w0 + _dot(jnp.tanh(lo).astype(BF16), w2_ref[...]))
    softplus = jnp.maximum(z, 0.0) + jnp.log(1.0 + jnp.exp(-jnp.abs(z)))
    lw = -jnp.exp(-softplus - 0.5)
    a = jax.nn.sigmoid(a0 + _dot(lo.astype(BF16), a2_ref[...]))
    g = _dot(jax.nn.sigmoid(g_lo).astype(BF16), g2_ref[...])

    ind = ind_ref[...]
    kk = k * k_k
    kk = kk / jnp.maximum(jnp.sqrt(_head_sum(kk * kk, ind)), 1e-12)
    k2 = k * (1.0 + (a - 1.0) * k_a)
    alpha = kk * a

    b = _dot_split(tri_ref[...], lw)
    b_prev = b - lw
    bcast = lambda row: jnp.broadcast_to(row, (c, d))
    b_mid = jnp.concatenate([bcast(b[ci * c + c // 2 - 1:ci * c + c // 2]) for ci in range(n_chunks)], axis=0)
    b_end = jnp.concatenate([bcast(b[ci * c + c - 1:ci * c + c]) for ci in range(n_chunks)], axis=0)
    kap_t = kk * jnp.exp(b_prev - b_mid)
    r_t = r * jnp.exp(b - b_mid)
    inv_t = jnp.exp(b_mid - b)
    alp_t, k_t = alpha * inv_t, k2 * inv_t
    kap_h = kk * jnp.exp(b_prev)
    r_h = r * jnp.exp(b)
    tail = jnp.exp(b_end - b)
    alp_c, k_c = alpha * tail, k2 * tail
    gam_end = jnp.exp(b_end)

    sr = lax.broadcasted_iota(jnp.int32, (gc, gc), 0)
    sc = lax.broadcasted_iota(jnp.int32, (gc, gc), 1)
    strict = (sc % c) < (sr % c)
    incl = (sc % c) <= (sr % c)
    eye = (sr == sc).astype(F32)
    lane_head = lax.broadcasted_iota(jnp.int32, (c, gw), 1) // RWKV_HEAD

    def stacked(x, ci, gi):
        return _stack_heads(x[ci * c:(ci + 1) * c, gi * gw:(gi + 1) * gw], lane_head)

    pairs = [(ci, gi) for ci in range(n_chunks) for gi in range(n_groups)]
    amat, vs, n_pow, inv = {}, {}, {}, {}
    for pr in pairs:
        st = lambda x: stacked(x, *pr)
        kr = jnp.concatenate([st(kap_t), st(r_t)], axis=0).astype(BF16)
        ak = jnp.concatenate([st(alp_t), st(k_t)], axis=0).astype(BF16)
        amat[pr] = _dot_nt(kr, ak)
        vs[pr] = st(v).astype(BF16)
    for pr in pairs:
        n_pow[pr] = jnp.where(strict, amat[pr][:gc, :gc], 0.0)
        inv[pr] = eye - n_pow[pr]
    for _ in range(int(np.log2(c)) - 1):
        for pr in pairs:
            pwb = n_pow[pr].astype(BF16)
            n_pow[pr] = _dot(pwb, pwb)
        for pr in pairs:
            inv[pr] = inv[pr] + _dot(inv[pr].astype(BF16), n_pow[pr].astype(BF16))
    akv = {pr: _dot(jnp.where(strict, amat[pr][:gc, gc:], 0.0).astype(BF16), vs[pr]).astype(BF16)
           for pr in pairs}
    pre = {}
    for pr in pairs:
        st = lambda x: stacked(x, *pr)
        invb = inv[pr].astype(BF16)
        kt = _dot(invb, st(kap_h).astype(BF16)).astype(BF16)
        w_mat = _dot(invb, akv[pr])
        a_ra = jnp.where(incl, amat[pr][gc:, :gc], 0.0).astype(BF16)
        a_rk = jnp.where(incl, amat[pr][gc:, gc:], 0.0).astype(BF16)
        akc = jnp.concatenate([st(alp_c), st(k_c)], axis=0).astype(BF16)
        pre[pr] = (kt, w_mat, a_ra, a_rk, st(r_h).astype(BF16), akc)

    y_rows = []
    for ci in range(n_chunks):
        s0 = [state_ref[gi] for gi in range(n_groups)]
        s0b = [x.astype(BF16) for x in s0]
        ub = [(-(_dot_nt(pre[ci, gi][0], s0b[gi]) + pre[ci, gi][1])).astype(BF16)
              for gi in range(n_groups)]
        ys = []
        for gi in range(n_groups):
            kt, w_mat, a_ra, a_rk, rh, akc = pre[ci, gi]
            uv = jnp.concatenate([ub[gi], vs[ci, gi]], axis=0)
            sl = slice(gi * gw, (gi + 1) * gw)
            state_ref[gi] = s0[gi] * gam_end[ci * c:ci * c + 1, sl] + _dot_tn(uv, akc)
            y_st = _dot_nt(rh, s0b[gi]) + _dot(a_ra, ub[gi]) + _dot(a_rk, vs[ci, gi])
            y = y_st[0:c]
            for h in range(1, RWKV_GROUP):
                y = y + y_st[h * c:(h + 1) * c]
            ys.append(y)
        y_rows.append(jnp.concatenate(ys, axis=1))
    y = jnp.concatenate(y_rows, axis=0)

    inv_n = 1.0 / RWKV_HEAD
    mean = _head_sum(y, ind) * inv_n
    yc = y - mean
    var = _head_sum(yc * yc, ind) * inv_n
    y = yc * lax.rsqrt(var + RWKV_GN_EPS) * ln_w + ln_b
    bonus = _head_sum(r * k2 * r_k, ind) * v
    o_ref[...] = ((y + bonus) * g).astype(o_ref.dtype)


def _rwkv(p_rwkv, mu, vecs, w2, a2, g2, *, batch, seq, n_chunks):
    t = p_rwkv.shape[0]
    c = RWKV_CHUNK
    tb = n_chunks * c
    nb = seq // tb
    gw = RWKV_GROUP * RWKV_HEAD
    head = np.arange(RWKV_DIM) // RWKV_HEAD
    ind = jnp.asarray(head[:, None] == head[None, :], BF16)
    tt = np.arange(tb)
    tri = jnp.asarray((tt[None, :] <= tt[:, None]) & (tt[None, :] // c == tt[:, None] // c), BF16)
    return pl.pallas_call(
        functools.partial(_rwkv_kernel, n_chunks=n_chunks),
        grid=(batch, nb),
        in_specs=[
            pl.BlockSpec((tb, RWKV_COLS), lambda b, j: (b * nb + j, 0)),
            _const_spec((1, RWKV_COLS)),
            _const_spec(vecs.shape),
            _const_spec(w2.shape),
            _const_spec(a2.shape),
            _const_spec(g2.shape),
            _const_spec(ind.shape),
            _const_spec(tri.shape),
        ],
        out_specs=pl.BlockSpec((tb, RWKV_DIM), lambda b, j: (b * nb + j, 0)),
        out_shape=jax.ShapeDtypeStruct((t, RWKV_DIM), BF16),
        scratch_shapes=[
            pltpu.VMEM((RWKV_HEADS // RWKV_GROUP, gw, gw), F32),
            pltpu.VMEM((1, RWKV_COLS), F32),
        ],
        compiler_params=_params("parallel", "arbitrary"),
        name="rwkv7",
    )(p_rwkv, mu, vecs, w2, a2, g2, ind, tri)


def _hgrn_masks():
    c = HG_CHUNK
    t = np.arange(c)
    mask = np.zeros((HG_LEVELS + 1, c, c), np.float32)
    mask[0] = np.eye(c)
    for l in range(1, HG_LEVELS + 1):
        m = 1 << l
        is_right = (t % m) >= m // 2
        same = (t[:, None] // m) == (t[None, :] // m)
        mask[l] = same & is_right[:, None] & ~is_right[None, :]
    return mask


def _hgrn_chunk(proj, lb, gain, tri, mask_ref, state_ref, side_jobs):
    c = HG_CHUNK
    jobs = iter(side_jobs)
    run_job = lambda: next(jobs, lambda: None)()
    q, f, vals, g = proj
    n = q.shape[1]
    q = q * jax.nn.sigmoid(q)
    forget = lb + (1.0 - lb) * jax.nn.sigmoid(f)
    key = 1.0 - forget
    logf = jnp.log(forget)

    b = _dot_split(tri, logf)
    e_cum = jnp.exp(b)
    b_end = b[c - 1:c, :]
    q_hat = (q * e_cum).astype(BF16)
    k_hat = (key * jnp.exp(b_end - b)).astype(BF16)
    decay_end = e_cum[c - 1:c, :]
    qb, kb = q.astype(BF16), key.astype(BF16)
    run_job()

    t_idx = lax.broadcasted_iota(jnp.int32, q.shape, 0)
    f_prev = pltpu.roll(forget, 1, 0)
    f_next = pltpu.roll(forget, c - 1, 0)
    level_ops = []
    for l in range(HG_LEVELS):
        m = 2 << l
        is_right = (t_idx & (m // 2)) != 0
        if m == 2:
            decay = jnp.where(is_right, forget, 1.0)
        elif m == 4:
            pos = t_idx & 3
            decay = jnp.where(pos == 0, f_next,
                              jnp.where(pos == 1, 1.0, jnp.where(pos == 2, forget, forget * f_prev)))
        else:
            b3 = b.reshape(c // m, m, n)
            b_ref = jnp.broadcast_to(b3[:, m // 2 - 1:m // 2, :], (c // m, m, n)).reshape(c, n)
            decay = jnp.exp(-jnp.abs(b - b_ref))
        level_ops.append((jnp.where(is_right, q, key) * decay).astype(BF16))
        run_job()

    heads = [slice(h * HG_K, (h + 1) * HG_K) for h in range(HG_HEADS)]
    attns = []
    for sl in heads:
        attn = mask_ref[0] * _dot_nt(qb[:, sl], kb[:, sl])
        for l in range(HG_LEVELS):
            ml = level_ops[l][:, sl]
            attn = attn + mask_ref[l + 1] * _dot_nt(ml, ml)
        attns.append(attn.astype(BF16))
        run_job()
    vhs = [vals[:, sl].astype(BF16) for sl in heads]
    sts = [state_ref[h] for h in range(HG_HEADS)]
    os = [_dot_nt(q_hat[:, sl], sts[h].astype(BF16)) + _dot(attns[h], vhs[h])
          for h, sl in enumerate(heads)]
    for h, sl in enumerate(heads):
        state_ref[h] = sts[h] * decay_end[:, sl] + _dot_tn(vhs[h], k_hat[:, sl])
    outs = []
    for h, sl in enumerate(heads):
        o = os[h]
        ms = jnp.mean(o * o, axis=-1, keepdims=True)
        gt = g[:, sl]
        outs.append((o * lax.rsqrt(ms + RMS_EPS) * gain * (gt * jax.nn.sigmoid(gt))).astype(BF16))
    for job in jobs:
        job()
    return jnp.concatenate(outs, axis=1)


def _hgrn_layer_kernel(x_ref, xn_ref, mod_ref, ngain_ref, win_ref, lbl_ref, gain_ref, tri_ref,
                       mask_ref, wout_ref, o_ref, state_ref, proj_ref, *, layer):
    c = HG_CHUNK
    mod = mod_ref[0]
    per_proj = proj_ref.shape[3] // HG_PROJ_PIECE

    def projection_jobs(x, slot):
        hin = _norm_mod(x, ngain_ref[...], mod, 0, 1).astype(BF16)

        def piece(k, p):
            def job():
                lo = p * HG_PROJ_PIECE
                proj_ref[slot, k, :, lo:lo + HG_PROJ_PIECE] = _dot(hin, win_ref[k * per_proj + p])
            return job

        return [piece(k, p) for k in range(4) for p in range(per_proj)]

    @pl.when(pl.program_id(1) == 0)
    def _():
        state_ref[...] = jnp.zeros_like(state_ref)
        for job in projection_jobs(x_ref[0:c, :], 0):
            job()

    logits = lbl_ref[...]
    pe = jnp.exp(logits - jnp.max(logits, axis=0, keepdims=True))
    lb = jnp.sum(pe[1:layer + 1], axis=0, keepdims=True) / jnp.sum(pe, axis=0, keepdims=True)

    def recur(slot, side_jobs):
        proj = tuple(proj_ref[slot, k] for k in range(4))
        return _hgrn_chunk(proj, lb, gain_ref[...], tri_ref[...], mask_ref, state_ref, side_jobs)

    y0 = recur(0, projection_jobs(x_ref[c:2 * c, :], 1))
    y1 = recur(1, projection_jobs(xn_ref[...], 0))
    y = jnp.concatenate([y0, y1], axis=0)
    o_ref[...] = x_ref[...] + mod[2:3] * _dot(y, wout_ref[...])


def _hgrn_layer(x, mod, norm_gain, w_in, lb_logits, gain, w_out, *, batch, seq, layer):
    t, d = x.shape
    c = HG_CHUNK
    n = HG_HEADS * HG_K
    nb = seq // (2 * c)
    tri = jnp.asarray(np.tril(np.ones((c, c))), BF16)
    mask = jnp.asarray(_hgrn_masks(), F32)
    w_pieces = w_in.reshape(d, 4 * n // HG_PROJ_PIECE, HG_PROJ_PIECE).transpose(1, 0, 2)
    next_chunk = lambda b, j: (b * 2 * nb + jnp.minimum(2 * j + 2, 2 * nb - 1), 0)
    return pl.pallas_call(
        functools.partial(_hgrn_layer_kernel, layer=layer),
        grid=(batch, nb),
        in_specs=[pl.BlockSpec((2 * c, d), lambda b, j: (b * nb + j, 0)),
                  pl.BlockSpec((c, d), next_chunk),
                  pl.BlockSpec((1, 6, d), lambda b, j: (b, 0, 0)),
                  _const_spec((1, d)), _const_spec(w_pieces.shape),
                  _const_spec(lb_logits.shape), _const_spec((1, HG_V)),
                  _const_spec(tri.shape), _const_spec(mask.shape), _const_spec(w_out.shape)],
        out_specs=pl.BlockSpec((2 * c, d), lambda b, j: (b * nb + j, 0)),
        out_shape=jax.ShapeDtypeStruct((t, d), F32),
        scratch_shapes=[pltpu.VMEM((HG_HEADS, HG_V, HG_K), F32),
                        pltpu.VMEM((2, 4, c, n), F32)],
        compiler_params=_params("parallel", "arbitrary"),
        name="hgrn2_layer",
    )(x, x, mod, norm_gain.reshape(1, d), w_pieces, lb_logits, gain, tri, mask, w_out)


def _pad_cols(w, n):
    return jnp.pad(w, ((0, 0), (0, n - w.shape[1])))


def _mla_weights(w_uq, w_ukv):
    hq = MLA_NOPE + MLA_ROPE
    wq = w_uq.reshape(Q_LORA, MLA_HEADS, hq)
    wq = jnp.pad(wq, ((0, 0), (0, 0), (0, MLA_HEAD_PAD - hq))).reshape(Q_LORA, -1)
    wkv = w_ukv.reshape(KV_LORA, MLA_HEADS, MLA_NOPE + MLA_V)
    wk = jnp.pad(wkv[:, :, :MLA_NOPE], ((0, 0), (0, 0), (0, MLA_HEAD_PAD - MLA_NOPE)))
    wk = wk.reshape(KV_LORA, -1)
    wv = wkv[:, :, MLA_NOPE:].reshape(KV_LORA, -1)
    n_extra = MLA_COLS_PAD - Q_LORA - KV_LORA
    place = np.zeros((n_extra, MLA_HEADS, MLA_HEAD_PAD), np.float32)
    for r in range(MLA_ROPE):
        place[r, :, MLA_NOPE + r] = 1.0
    wk_full = jnp.concatenate([wk, jnp.asarray(place.reshape(n_extra, -1))], axis=0)
    wv_full = jnp.concatenate([wv, jnp.zeros((n_extra, wv.shape[1]), F32)], axis=0)
    return wq.T.astype(BF16), wk_full.astype(BF16), wv_full.T.astype(BF16)


def _rope_inv_freq():
    inv = 1.0 / (ROPE_BASE ** (jnp.arange(0, MLA_ROPE, 2, dtype=F32) / MLA_ROPE))
    return inv.reshape(MLA_ROPE // 2, 1)


def kernel(x, c, positions, ada_w, ada_b, norm_mix, norm_ffn, w_in_even, mla_q_norm, mla_w_uq, mla_kv_norm, mla_w_ukv, rwkv_mu, rwkv_w0, rwkv_w2, rwkv_a0, rwkv_a2, rwkv_g2, rwkv_k_k, rwkv_k_a, rwkv_r_k, rwkv_ln_w, rwkv_ln_b, w_out_even, w_in_odd, hg_lb_logits, hg_out_norm, w_out_odd, ffn_w_gate, ffn_w_up, ffn_w_down, final_norm):
    batch, seq, d = x.shape
    depth = ada_w.shape[0]
    t = batch * seq
    tm = min(512, seq)
    xt = x.reshape(t, d)

    mod_all = _ada_mod(c, ada_w, ada_b).reshape(depth, batch, 6, d)

    for l in range(depth):
        mod = mod_all[l]
        j = l // 2
        if l % 2 == 0:
            w_in = w_in_even[j]
            w_mla = _pad_cols(w_in[:, :MLA_COLS], MLA_COLS_PAD).astype(BF16)
            w_rwkv = w_in[:, MLA_COLS:].astype(BF16)
            p_mla, p_rwkv = _in_proj(xt, mod, norm_mix[l], [w_mla, w_rwkv],
                                     seq=seq, sh=0, sc=1, tm=tm, out_dtype=BF16)
            wqt, wk, wvt = _mla_weights(mla_w_uq[j], mla_w_ukv[j])
            qt, kh, vt = _mla_prep(p_mla, positions, _rope_inv_freq(),
                                   mla_q_norm[j].reshape(1, -1), mla_kv_norm[j].reshape(1, -1),
                                   wqt, wk, wvt, batch=batch, seq=seq, tm=tm, tk=ATTN_TILE)
            y_a = _attention(qt, kh, vt, batch=batch, seq=seq, tq=ATTN_TILE, nh=ATTN_HEADS)

            zeros_lo = jnp.zeros((DECAY_LORA, RWKV_DIM), F32)
            w2 = jnp.concatenate([rwkv_w2[j], zeros_lo], axis=0).astype(BF16)
            a2 = jnp.concatenate([zeros_lo, rwkv_a2[j]], axis=0).astype(BF16)
            vecs = jnp.stack([rwkv_w0[j], rwkv_a0[j], rwkv_k_k[j], rwkv_k_a[j],
                              rwkv_r_k[j].reshape(-1), rwkv_ln_w[j], rwkv_ln_b[j],
                              jnp.zeros((RWKV_DIM,), F32)])
            y_b = _rwkv(p_rwkv, rwkv_mu[j].reshape(1, -1), vecs, w2, a2,
                        rwkv_g2[j].astype(BF16), batch=batch, seq=seq,
                        n_chunks=min(RWKV_BLOCK_CHUNKS, seq // RWKV_CHUNK))
            w_out = w_out_even[j].astype(BF16)
            n_a = MLA_HEADS * MLA_V
            ys, w_outs = [y_a, y_b], [w_out[:n_a], w_out[n_a:]]
        else:
            xt = _hgrn_layer(xt, mod, norm_mix[l], w_in_odd[j].astype(BF16), hg_lb_logits,
                             hg_out_norm[j].reshape(1, -1), w_out_odd[j].astype(BF16),
                             batch=batch, seq=seq, layer=l)
            ys, w_outs = [], []
        xt = _ffn(xt, mod, norm_ffn[l], ffn_w_gate[l].astype(BF16), ffn_w_up[l].astype(BF16),
                  ffn_w_down[l].astype(BF16), final_norm, ys, w_outs, seq=seq, tm=tm,
                  final=(l == depth - 1))
    return xt.reshape(batch, seq, d)
```

```python
import functools

import numpy as np
import jax
import jax.numpy as jnp
from jax import lax
from jax.experimental import pallas as pl
from jax.experimental.pallas import tpu as pltpu

F32 = jnp.float32
BF16 = jnp.bfloat16

RMS_EPS = 1e-6
LOG2_E = 1.4426950408889634
EXP_NEG_HALF = 0.6065306597126334
LANES = 128
VMEM_LIMIT = 56 * 1024 * 1024

MLA_HEADS = 8
MLA_NOPE = 64
MLA_ROPE = 32
MLA_V = 64
Q_LORA = 384
KV_LORA = 256
ROPE_BASE = 10000.0
MLA_COLS = Q_LORA + KV_LORA + MLA_ROPE
MLA_COLS_PAD = 768
MLA_HEAD_PAD = 128
ATTN_TILE = 256
ATTN_HEADS = 4

RWKV_HEAD = 64
RWKV_DIM = 512
RWKV_HEADS = 8
DECAY_LORA = 64
AAA_LORA = 64
GATE_LORA = 128
RWKV_GN_EPS = RWKV_HEAD * 1e-5
RWKV_COLS = 3 * RWKV_DIM + DECAY_LORA + AAA_LORA + GATE_LORA
RWKV_CHUNK = 64
RWKV_BLOCK_CHUNKS = 4
RWKV_GROUP = 4

HG_K = 128
HG_HEADS = 8
HG_V = 128
HG_CHUNK = 128
HG_LEVELS = 7
HG_PROJ_PIECE = 256


def _params(*sem):
    return pltpu.CompilerParams(dimension_semantics=sem, vmem_limit_bytes=VMEM_LIMIT)


def _const_spec(shape):
    nd = len(shape)
    return pl.BlockSpec(shape, lambda *_: (0,) * nd, pipeline_mode=pl.Buffered(1))


def _dot(a, b):
    return jnp.dot(a, b, preferred_element_type=F32)


def _dot_nt(a, b):
    return lax.dot_general(a, b, (((1,), (1,)), ((), ())), preferred_element_type=F32)


def _dot_tn(a, b):
    return lax.dot_general(a, b, (((0,), (0,)), ((), ())), preferred_element_type=F32)


def _dot_split(sel, x):
    hi = x.astype(BF16)
    lo = (x - hi.astype(F32)).astype(BF16)
    return _dot(sel, hi) + _dot(sel, lo)


def _head_sum(x, ind):
    return _dot(x.astype(BF16), ind)


def _ada_kernel(c_ref, w_ref, b_ref, o_ref):
    c = c_ref[...]
    cond = c * jax.nn.sigmoid(c)
    o_ref[0] = _dot(cond.astype(BF16), w_ref[0].astype(BF16)) + b_ref[0]


def _ada_mod(c, ada_w, ada_b):
    depth, d, n = ada_w.shape
    b = c.shape[0]
    tn = n // 4
    return pl.pallas_call(
        _ada_kernel,
        grid=(depth, n // tn),
        in_specs=[
            pl.BlockSpec((b, d), lambda l, j: (0, 0)),
            pl.BlockSpec((1, d, tn), lambda l, j: (l, 0, j)),
            pl.BlockSpec((1, 1, tn), lambda l, j: (l, 0, j)),
        ],
        out_specs=pl.BlockSpec((1, b, tn), lambda l, j: (l, 0, j)),
        out_shape=jax.ShapeDtypeStruct((depth, b, n), F32),
        compiler_params=_params("arbitrary", "arbitrary"),
        name="ada_mod",
    )(c, ada_w, ada_b.reshape(depth, 1, n))


def _norm_mod(x, gain, mod, sh, sc):
    ms = jnp.mean(x * x, axis=-1, keepdims=True)
    y = x * lax.rsqrt(ms + RMS_EPS) * gain
    return y * (1.0 + mod[sc:sc + 1]) + mod[sh:sh + 1]


def _in_proj_kernel(x_ref, mod_ref, gain_ref, *refs, n_out, sh, sc):
    w_refs, o_refs = refs[:n_out], refs[n_out:]
    h = _norm_mod(x_ref[...], gain_ref[...], mod_ref[0], sh, sc).astype(BF16)
    for w_ref, o_ref in zip(w_refs, o_refs):
        o_ref[...] = _dot(h, w_ref[...]).astype(o_ref.dtype)


def _in_proj(x, mod, gain, weights, *, seq, sh, sc, tm, out_dtype):
    t, d = x.shape
    per_b = seq // tm
    n_out = len(weights)
    in_specs = [
        pl.BlockSpec((tm, d), lambda i: (i, 0)),
        pl.BlockSpec((1, 6, d), lambda i: (i // per_b, 0, 0)),
        _const_spec((1, d)),
    ] + [_const_spec(w.shape) for w in weights]
    out_specs = [pl.BlockSpec((tm, w.shape[1]), lambda i: (i, 0)) for w in weights]
    out_shape = [jax.ShapeDtypeStruct((t, w.shape[1]), out_dtype) for w in weights]
    return pl.pallas_call(
        functools.partial(_in_proj_kernel, n_out=n_out, sh=sh, sc=sc),
        grid=(t // tm,),
        in_specs=in_specs,
        out_specs=out_specs,
        out_shape=out_shape,
        compiler_params=_params("parallel"),
        name="in_proj",
    )(x, mod, gain.reshape(1, d), *weights)


def _ffn_kernel(x_ref, mod_ref, gain_ref, wg_ref, wu_ref, wd_ref, fin_ref, *refs, n_in, final):
    y_refs, w_refs, o_ref = refs[:n_in], refs[n_in:2 * n_in], refs[2 * n_in]
    x = x_ref[...]
    mod = mod_ref[0]
    if n_in:
        mix = _dot(y_refs[0][...], w_refs[0][...])
        for y_ref, w_ref in zip(y_refs[1:], w_refs[1:]):
            mix += _dot(y_ref[...], w_ref[...])
        x = x + mod[2:3] * mix
    h = _norm_mod(x, gain_ref[...], mod, 3, 4).astype(BF16)
    gate = _dot(h, wg_ref[...])
    up = _dot(h, wu_ref[...])
    act = (gate * jax.nn.sigmoid(gate) * up).astype(BF16)
    y = x + mod[5:6] * _dot(act, wd_ref[...])
    if final:
        ms = jnp.mean(y * y, axis=-1, keepdims=True)
        y = y * lax.rsqrt(ms + RMS_EPS) * fin_ref[...]
    o_ref[...] = y


def _ffn(x, mod, gain, wg, wu, wd, fin, ys, w_outs, *, seq, tm, final):
    t, d = x.shape
    per_b = seq // tm
    return pl.pallas_call(
        functools.partial(_ffn_kernel, n_in=len(ys), final=final),
        grid=(t // tm,),
        in_specs=[
            pl.BlockSpec((tm, d), lambda i: (i, 0)),
            pl.BlockSpec((1, 6, d), lambda i: (i // per_b, 0, 0)),
            _const_spec((1, d)),
            _const_spec(wg.shape),
            _const_spec(wu.shape),
            _const_spec(wd.shape),
            _const_spec((1, d)),
        ] + [pl.BlockSpec((tm, y.shape[1]), lambda i: (i, 0)) for y in ys]
          + [_const_spec(w.shape) for w in w_outs],
        out_specs=pl.BlockSpec((tm, d), lambda i: (i, 0)),
        out_shape=jax.ShapeDtypeStruct((t, d), F32),
        compiler_params=_params("parallel"),
        name="ffn",
    )(x, mod, gain.reshape(1, d), wg, wu, wd, fin.reshape(1, d), *ys, *w_outs)


def _mla_prep_kernel(p_ref, pos_ref, invf_ref, qn_ref, kvn_ref, wqt_ref, wk_ref, wvt_ref,
                     qt_ref, k_ref, vt_ref, *, scale, tk):
    p = p_ref[...].astype(F32)
    tm = p.shape[0]
    half = MLA_ROPE // 2
    ang = invf_ref[...] * pos_ref[0].astype(F32)
    cos_t, sin_t = jnp.cos(ang), jnp.sin(ang)

    c_q = p[:, :Q_LORA]
    ms = jnp.mean(c_q * c_q, axis=-1, keepdims=True)
    cqn = (c_q * lax.rsqrt(ms + RMS_EPS) * qn_ref[...]).astype(BF16)
    qt = _dot_nt(wqt_ref[...], cqn) * scale
    pieces = []
    for h in range(MLA_HEADS):
        base = h * MLA_HEAD_PAD
        x1 = qt[base + MLA_NOPE:base + MLA_NOPE + half]
        x2 = qt[base + MLA_NOPE + half:base + MLA_NOPE + MLA_ROPE]
        pieces += [qt[base:base + MLA_NOPE], x1 * cos_t - x2 * sin_t, x1 * sin_t + x2 * cos_t,
                   qt[base + MLA_NOPE + MLA_ROPE:base + MLA_HEAD_PAD]]
    qt_ref[0] = jnp.concatenate(pieces, axis=0).astype(qt_ref.dtype)

    ckv = p[:, Q_LORA:Q_LORA + KV_LORA]
    ss = jnp.mean(ckv * ckv, axis=-1, keepdims=True)
    ckvn = ckv * lax.rsqrt(ss + RMS_EPS) * kvn_ref[...]
    kr = p[:, Q_LORA + KV_LORA:]
    rest = kr.shape[1] - MLA_ROPE
    cos_k = jnp.concatenate([cos_t, cos_t, jnp.ones((rest, tm), F32)], axis=0).T
    sin_k = jnp.concatenate([sin_t, sin_t, jnp.zeros((rest, tm), F32)], axis=0).T
    lane = lax.broadcasted_iota(jnp.int32, kr.shape, 1)
    partner = jnp.where(lane < half, -pltpu.roll(kr, kr.shape[1] - half, 1), pltpu.roll(kr, half, 1))
    lhs = jnp.concatenate([ckvn, kr * cos_k + partner * sin_k], axis=1).astype(BF16)
    k_ref[...] = _dot(lhs, wk_ref[...]).astype(k_ref.dtype)
    vt = _dot_nt(wvt_ref[...], lhs).astype(vt_ref.dtype)
    for jj in range(tm // tk):
        vt_ref[0, jj] = vt[:, jj * tk:(jj + 1) * tk]


def _mla_prep(p_mla, pos, invf, q_norm, kv_norm, wqt, wk, wvt, *, batch, seq, tm, tk):
    t = p_mla.shape[0]
    n_q = MLA_HEADS * MLA_HEAD_PAD
    n_v = MLA_HEADS * MLA_V
    per_b = seq // tm
    scale = (MLA_NOPE + MLA_ROPE) ** -0.5 * LOG2_E
    return pl.pallas_call(
        functools.partial(_mla_prep_kernel, scale=scale, tk=tk),
        grid=(t // tm,),
        in_specs=[
            pl.BlockSpec((tm, MLA_COLS_PAD), lambda i: (i, 0)),
            pl.BlockSpec((1, 1, tm), lambda i: (i, 0, 0)),
            _const_spec(invf.shape),
            _const_spec((1, Q_LORA)),
            _const_spec((1, KV_LORA)),
            _const_spec(wqt.shape),
            _const_spec(wk.shape),
            _const_spec(wvt.shape),
        ],
        out_specs=[
            pl.BlockSpec((1, n_q, tm), lambda i: (i // per_b, 0, i % per_b)),
            pl.BlockSpec((tm, n_q), lambda i: (i, 0)),
            pl.BlockSpec((1, tm // tk, n_v, tk), lambda i: (i // per_b, i % per_b, 0, 0)),
        ],
        out_shape=[
            jax.ShapeDtypeStruct((batch, n_q, seq), BF16),
            jax.ShapeDtypeStruct((t, n_q), BF16),
            jax.ShapeDtypeStruct((batch, seq // tk, n_v, tk), BF16),
        ],
        compiler_params=_params("parallel"),
        name="mla_prep",
    )(p_mla, pos.reshape(t // tm, 1, tm), invf, q_norm, kv_norm, wqt, wk, wvt)


def _attn_kernel(qt_ref, k_ref, vt_ref, o_ref, s_ref, p_ref, *, tq, nh):
    i = pl.program_id(2)
    qts = [qt_ref[0, h * MLA_HEAD_PAD:(h + 1) * MLA_HEAD_PAD, :] for h in range(nh)]
    causal = (lax.broadcasted_iota(jnp.int32, (tq, tq), 0)
              <= lax.broadcasted_iota(jnp.int32, (tq, tq), 1))

    def put_scores(j, slot):
        start = pl.multiple_of(j * tq, tq)
        for h in range(nh):
            s_ref[slot, h] = _dot(
                k_ref[pl.ds(start, tq), h * MLA_HEAD_PAD:(h + 1) * MLA_HEAD_PAD], qts[h])

    def value_dots(j, probs):
        return [_dot(vt_ref[0, j, h * MLA_V:(h + 1) * MLA_V, :], probs[h]) for h in range(nh)]

    def step(carry, slot, pv_prev, masked):
        new, probs = [], []
        for h in range(nh):
            m, l, acc = carry[h]
            s = s_ref[slot, h]
            if masked:
                s = jnp.where(causal, s, -jnp.inf)
            m_new = jnp.maximum(m, jnp.max(s, axis=0, keepdims=True))
            pr = jnp.exp2(s - m_new)
            alpha = jnp.exp2(m - m_new)
            probs.append(pr.astype(BF16))
            new.append((m_new, alpha * l + jnp.sum(pr, axis=0, keepdims=True),
                        alpha * (acc + pv_prev[h])))
        return tuple(new), probs

    def parked(slot):
        return [p_ref[slot, h] for h in range(nh)]

    def pair(t, carry):
        j = 2 * t
        put_scores(j + 1, 1)
        pv = value_dots(jnp.where(t == 0, i, j - 1), parked(1))
        carry, probs = step(carry, 0, pv, False)
        for h in range(nh):
            p_ref[0, h] = probs[h]
        put_scores(j + 2, 0)
        pv = value_dots(j, parked(0))
        carry, probs = step(carry, 1, pv, False)
        for h in range(nh):
            p_ref[1, h] = probs[h]
        return carry

    def emit(carry, pv_last):
        outs = [(acc + pv_last[h]) * (1.0 / l) for h, (_, l, acc) in enumerate(carry)]
        o_ref[...] = jnp.concatenate(outs, axis=0).T.astype(o_ref.dtype)

    put_scores(i, 1)
    put_scores(0, 0)
    zeros = [jnp.zeros((MLA_V, tq), F32)] * nh
    init = tuple((jnp.full((1, tq), -jnp.inf, F32), jnp.zeros((1, tq), F32), zeros[h])
                 for h in range(nh))
    carry, probs = step(init, 1, zeros, True)
    for h in range(nh):
        p_ref[1, h] = probs[h]
    carry = lax.fori_loop(0, i // 2, pair, carry)
    n_done = 2 * (i // 2)
    last = jnp.where(n_done == 0, i, n_done - 1)

    @pl.when(i % 2 == 0)
    def _():
        emit(carry, value_dots(last, parked(1)))

    @pl.when(i % 2 == 1)
    def _():
        final, probs = step(carry, 0, value_dots(last, parked(1)), False)
        emit(final, value_dots(i - 1, probs))


def _attention(qt, k, vt, *, batch, seq, tq, nh):
    t = k.shape[0]
    nq = seq // tq
    return pl.pallas_call(
        functools.partial(_attn_kernel, tq=tq, nh=nh),
        grid=(batch, MLA_HEADS // nh, nq),
        in_specs=[
            pl.BlockSpec((1, nh * MLA_HEAD_PAD, tq), lambda b, h, i: (b, h, i)),
            pl.BlockSpec((seq, nh * MLA_HEAD_PAD), lambda b, h, i: (b, h)),
            pl.BlockSpec((1, seq // tq, nh * MLA_V, tq), lambda b, h, i: (b, 0, h, 0)),
        ],
        out_specs=pl.BlockSpec((tq, nh * MLA_V), lambda b, h, i: (b * nq + i, h)),
        out_shape=jax.ShapeDtypeStruct((t, MLA_HEADS * MLA_V), BF16),
        scratch_shapes=[pltpu.VMEM((2, nh, tq, tq), F32), pltpu.VMEM((2, nh, tq, tq), BF16)],
        compiler_params=_params("parallel", "parallel", "arbitrary"),
        name="mla_attention",
    )(qt, k, vt)


def _stack_heads(x, lane_head):
    return jnp.concatenate(
        [jnp.where(lane_head == h, x, 0.0) for h in range(RWKV_GROUP)], axis=0)


def _rwkv_kernel(p_ref, mu_ref, vec_ref, w2_ref, a2_ref, g2_ref, ind_ref, tri_ref,
                 o_ref, state_ref, prev_ref, *, n_chunks):
    c = RWKV_CHUNK
    gw = RWKV_GROUP * RWKV_HEAD
    gc = RWKV_GROUP * c
    n_groups = RWKV_HEADS // RWKV_GROUP
    tb = n_chunks * c

    @pl.when(pl.program_id(1) == 0)
    def _():
        state_ref[...] = jnp.zeros_like(state_ref)
        prev_ref[...] = jnp.zeros_like(prev_ref)

    p = p_ref[...].astype(F32)
    rows = lax.broadcasted_iota(jnp.int32, p.shape, 0)
    shifted = jnp.where(rows == 0, prev_ref[...], pltpu.roll(p, 1, 0))
    prev_ref[...] = p[tb - 1:tb, :]
    p = p + (shifted - p) * mu_ref[...]

    d = RWKV_DIM
    r, k, v = p[:, :d], p[:, d:2 * d], p[:, 2 * d:3 * d]
    lo = p[:, 3 * d:3 * d + DECAY_LORA + AAA_LORA]
    g_lo = p[:, 3 * d + DECAY_LORA + AAA_LORA:]
    w0, a0, k_k, k_a = vec_ref[0:1], vec_ref[1:2], vec_ref[2:3], vec_ref[3:4]
    r_k, ln_w, ln_b = vec_ref[4:5], vec_ref[5:6], vec_ref[6:7]

    x_w = w0 + _dot(jnp.tanh(lo).astype(BF16), w2_ref[...])
    lw = -EXP_NEG_HALF * jax.nn.sigmoid(x_w)
    a = jax.nn.sigmoid(a0 + _dot(lo.astype(BF16), a2_ref[...]))
    g = _dot(jax.nn.sigmoid(g_lo).astype(BF16), g2_ref[...])

    ind = ind_ref[...]
    kk = k * k_k
    kk = kk * lax.rsqrt(jnp.maximum(_head_sum(kk * kk, ind), 1e-24))
    k2 = k * (1.0 + (a - 1.0) * k_a)
    alpha = kk * a

    b = _dot_split(tri_ref[...], lw)
    b_prev = b - lw
    bcast = lambda row: jnp.broadcast_to(row, (c, d))
    b_mid = jnp.concatenate([bcast(b[ci * c + c // 2 - 1:ci * c + c // 2]) for ci in range(n_chunks)], axis=0)
    b_end = jnp.concatenate([bcast(b[ci * c + c - 1:ci * c + c]) for ci in range(n_chunks)], axis=0)
    kap_t = kk * jnp.exp(b_prev - b_mid)
    r_t = r * jnp.exp(b - b_mid)
    inv_t = jnp.exp(b_mid - b)
    alp_t, k_t = alpha * inv_t, k2 * inv_t
    kap_h = kk * jnp.exp(b_prev)
    r_h = r * jnp.exp(b)
    tail = jnp.exp(b_end - b)
    alp_c, k_c = alpha * tail, k2 * tail
    gam_end = jnp.exp(b_end)

    sr = lax.broadcasted_iota(jnp.int32, (gc, gc), 0)
    sc = lax.broadcasted_iota(jnp.int32, (gc, gc), 1)
    strict = (sc % c) < (sr % c)
    incl = (sc % c) <= (sr % c)
    eye = (sr == sc).astype(F32)
    lane_head = lax.broadcasted_iota(jnp.int32, (c, gw), 1) // RWKV_HEAD

    def stacked(x, ci, gi):
        return _stack_heads(x[ci * c:(ci + 1) * c, gi * gw:(gi + 1) * gw], lane_head)

    pairs = [(ci, gi) for ci in range(n_chunks) for gi in range(n_groups)]
    amat, vs, n_pow, inv = {}, {}, {}, {}
    for pr in pairs:
        st = lambda x: stacked(x, *pr)
        kr = jnp.concatenate([st(kap_t), st(r_t)], axis=0).astype(BF16)
        ak = jnp.concatenate([st(alp_t), st(k_t)], axis=0).astype(BF16)
        amat[pr] = _dot_nt(kr, ak)
        vs[pr] = st(v).astype(BF16)
    for pr in pairs:
        n_pow[pr] = jnp.where(strict, amat[pr][:gc, :gc], 0.0)
        inv[pr] = eye - n_pow[pr]
    for _ in range(int(np.log2(c)) - 1):
        for pr in pairs:
            pwb = n_pow[pr].astype(BF16)
            n_pow[pr] = _dot(pwb, pwb)
        for pr in pairs:
            inv[pr] = inv[pr] + _dot(inv[pr].astype(BF16), n_pow[pr].astype(BF16))
    akv = {pr: _dot(jnp.where(strict, amat[pr][:gc, gc:], 0.0).astype(BF16), vs[pr]).astype(BF16)
           for pr in pairs}
    pre = {}
    for pr in pairs:
        st = lambda x: stacked(x, *pr)
        invb = inv[pr].astype(BF16)
        kt = _dot(invb, st(kap_h).astype(BF16)).astype(BF16)
        w_mat = _dot(invb, akv[pr])
        a_ra = jnp.where(incl, amat[pr][gc:, :gc], 0.0).astype(BF16)
        a_rk = jnp.where(incl, amat[pr][gc:, gc:], 0.0).astype(BF16)
        akc = jnp.concatenate([st(alp_c), st(k_c)], axis=0).astype(BF16)
        pre[pr] = (kt, w_mat, a_ra, a_rk, st(r_h).astype(BF16), akc)

    y_rows = []
    for ci in range(n_chunks):
        s0 = [state_ref[gi] for gi in range(n_groups)]
        s0b = [x.astype(BF16) for x in s0]
        ub = [(-(_dot_nt(pre[ci, gi][0], s0b[gi]) + pre[ci, gi][1])).astype(BF16)
              for gi in range(n_groups)]
        ys = []
        for gi in range(n_groups):
            kt, w_mat, a_ra, a_rk, rh, akc = pre[ci, gi]
            uv = jnp.concatenate([ub[gi], vs[ci, gi]], axis=0)
            sl = slice(gi * gw, (gi + 1) * gw)
            state_ref[gi] = s0[gi] * gam_end[ci * c:ci * c + 1, sl] + _dot_tn(uv, akc)
            y_st = _dot_nt(rh, s0b[gi]) + _dot(a_ra, ub[gi]) + _dot(a_rk, vs[ci, gi])
            y = y_st[0:c]
            for h in range(1, RWKV_GROUP):
                y = y + y_st[h * c:(h + 1) * c]
            ys.append(y)
        y_rows.append(jnp.concatenate(ys, axis=1))
    y = jnp.concatenate(y_rows, axis=0)

    inv_n = 1.0 / RWKV_HEAD
    mean = _head_sum(y, ind) * inv_n
    yc = y - mean
    var = _head_sum(yc * yc, ind) * inv_n
    y = yc * lax.rsqrt(var + RWKV_GN_EPS) * ln_w + ln_b
    bonus = _head_sum(r * k2 * r_k, ind) * v
    o_ref[...] = ((y + bonus) * g).astype(o_ref.dtype)


def _rwkv(p_rwkv, mu, vecs, w2, a2, g2, *, batch, seq, n_chunks):
    t = p_rwkv.shape[0]
    c = RWKV_CHUNK
    tb = n_chunks * c
    nb = seq // tb
    gw = RWKV_GROUP * RWKV_HEAD
    head = np.arange(RWKV_DIM) // RWKV_HEAD
    ind = jnp.asarray(head[:, None] == head[None, :], BF16)
    tt = np.arange(tb)
    tri = jnp.asarray((tt[None, :] <= tt[:, None]) & (tt[None, :] // c == tt[:, None] // c), BF16)
    return pl.pallas_call(
        functools.partial(_rwkv_kernel, n_chunks=n_chunks),
        grid=(batch, nb),
        in_specs=[
            pl.BlockSpec((tb, RWKV_COLS), lambda b, j: (b * nb + j, 0)),
            _const_spec((1, RWKV_COLS)),
            _const_spec(vecs.shape),
            _const_spec(w2.shape),
            _const_spec(a2.shape),
            _const_spec(g2.shape),
            _const_spec(ind.shape),
            _const_spec(tri.shape),
        ],
        out_specs=pl.BlockSpec((tb, RWKV_DIM), lambda b, j: (b * nb + j, 0)),
        out_shape=jax.ShapeDtypeStruct((t, RWKV_DIM), BF16),
        scratch_shapes=[
            pltpu.VMEM((RWKV_HEADS // RWKV_GROUP, gw, gw), F32),
            pltpu.VMEM((1, RWKV_COLS), F32),
        ],
        compiler_params=_params("parallel", "arbitrary"),
        name="rwkv7",
    )(p_rwkv, mu, vecs, w2, a2, g2, ind, tri)


def _hgrn_masks():
    c = HG_CHUNK
    t = np.arange(c)
    mask = np.zeros((HG_LEVELS + 1, c, c), np.float32)
    mask[0] = np.eye(c)
    for l in range(1, HG_LEVELS + 1):
        m = 1 << l
        is_right = (t % m) >= m // 2
        same = (t[:, None] // m) == (t[None, :] // m)
        mask[l] = same & is_right[:, None] & ~is_right[None, :]
    return mask


def _hgrn_chunk(proj, lb, gain, tri, mask_ref, state_ref, side_jobs):
    c = HG_CHUNK
    jobs = iter(side_jobs)
    run_job = lambda: next(jobs, lambda: None)()
    q, f, vals, g = proj
    n = q.shape[1]
    q = q * jax.nn.sigmoid(q)
    forget = lb + (1.0 - lb) * jax.nn.sigmoid(f)
    key = 1.0 - forget
    logf = jnp.log(forget)

    b = _dot_split(tri, logf)
    e_cum = jnp.exp(b)
    b_end = b[c - 1:c, :]
    q_hat = (q * e_cum).astype(BF16)
    k_hat = (key * jnp.exp(b_end - b)).astype(BF16)
    decay_end = e_cum[c - 1:c, :]
    qb, kb = q.astype(BF16), key.astype(BF16)
    run_job()

    t_idx = lax.broadcasted_iota(jnp.int32, q.shape, 0)
    f_prev = pltpu.roll(forget, 1, 0)
    f_next = pltpu.roll(forget, c - 1, 0)
    level_ops = []
    for l in range(HG_LEVELS):
        m = 2 << l
        is_right = (t_idx & (m // 2)) != 0
        if m == 2:
            decay = jnp.where(is_right, forget, 1.0)
        elif m == 4:
            pos = t_idx & 3
            decay = jnp.where(pos == 0, f_next,
                              jnp.where(pos == 1, 1.0, jnp.where(pos == 2, forget, forget * f_prev)))
        else:
            b3 = b.reshape(c // m, m, n)
            b_ref = jnp.broadcast_to(b3[:, m // 2 - 1:m // 2, :], (c // m, m, n)).reshape(c, n)
            decay = jnp.exp(-jnp.abs(b - b_ref))
        level_ops.append((jnp.where(is_right, q, key) * decay).astype(BF16))
        run_job()

    heads = [slice(h * HG_K, (h + 1) * HG_K) for h in range(HG_HEADS)]
    attns = []
    for sl in heads:
        attn = mask_ref[0] * _dot_nt(qb[:, sl], kb[:, sl])
        for l in range(HG_LEVELS):
            ml = level_ops[l][:, sl]
            attn = attn + mask_ref[l + 1] * _dot_nt(ml, ml)
        attns.append(attn.astype(BF16))
        run_job()
    vhs = [vals[:, sl].astype(BF16) for sl in heads]
    sts = [state_ref[h] for h in range(HG_HEADS)]
    os = [_dot_nt(q_hat[:, sl], sts[h].astype(BF16)) + _dot(attns[h], vhs[h])
          for h, sl in enumerate(heads)]
    for h, sl in enumerate(heads):
        state_ref[h] = sts[h] * decay_end[:, sl] + _dot_tn(vhs[h], k_hat[:, sl])
    outs = []
    for h, sl in enumerate(heads):
        o = os[h]
        ms = jnp.mean(o * o, axis=-1, keepdims=True)
        gt = g[:, sl]
        outs.append((o * lax.rsqrt(ms + RMS_EPS) * gain * (gt * jax.nn.sigmoid(gt))).astype(BF16))
    for job in jobs:
        job()
    return jnp.concatenate(outs, axis=1)


def _hgrn_layer_kernel(x_ref, xn_ref, mod_ref, ngain_ref, win_ref, lbl_ref, gain_ref, tri_ref,
                       mask_ref, wout_ref, o_ref, state_ref, proj_ref, *, layer):
    c = HG_CHUNK
    mod = mod_ref[0]
    per_proj = proj_ref.shape[3] // HG_PROJ_PIECE

    def projection_jobs(x, slot):
        hin = _norm_mod(x, ngain_ref[...], mod, 0, 1).astype(BF16)

        def piece(k, p):
            def job():
                lo = p * HG_PROJ_PIECE
                proj_ref[slot, k, :, lo:lo + HG_PROJ_PIECE] = _dot(hin, win_ref[k * per_proj + p])
            return job

        return [piece(k, p) for k in range(4) for p in range(per_proj)]

    @pl.when(pl.program_id(1) == 0)
    def _():
        state_ref[...] = jnp.zeros_like(state_ref)
        for job in projection_jobs(x_ref[0:c, :], 0):
            job()

    logits = lbl_ref[...]
    pe = jnp.exp(logits - jnp.max(logits, axis=0, keepdims=True))
    lb = jnp.sum(pe[1:layer + 1], axis=0, keepdims=True) / jnp.sum(pe, axis=0, keepdims=True)

    def recur(slot, side_jobs):
        proj = tuple(proj_ref[slot, k] for k in range(4))
        return _hgrn_chunk(proj, lb, gain_ref[...], tri_ref[...], mask_ref, state_ref, side_jobs)

    y0 = recur(0, projection_jobs(x_ref[c:2 * c, :], 1))
    y1 = recur(1, projection_jobs(xn_ref[...], 0))
    y = jnp.concatenate([y0, y1], axis=0)
    o_ref[...] = x_ref[...] + mod[2:3] * _dot(y, wout_ref[...])


def _hgrn_layer(x, mod, norm_gain, w_in, lb_logits, gain, w_out, *, batch, seq, layer):
    t, d = x.shape
    c = HG_CHUNK
    n = HG_HEADS * HG_K
    nb = seq // (2 * c)
    tri = jnp.asarray(np.tril(np.ones((c, c))), BF16)
    mask = jnp.asarray(_hgrn_masks(), F32)
    w_pieces = w_in.reshape(d, 4 * n // HG_PROJ_PIECE, HG_PROJ_PIECE).transpose(1, 0, 2)
    next_chunk = lambda b, j: (b * 2 * nb + jnp.minimum(2 * j + 2, 2 * nb - 1), 0)
    return pl.pallas_call(
        functools.partial(_hgrn_layer_kernel, layer=layer),
        grid=(batch, nb),
        in_specs=[pl.BlockSpec((2 * c, d), lambda b, j: (b * nb + j, 0)),
                  pl.BlockSpec((c, d), next_chunk),
                  pl.BlockSpec((1, 6, d), lambda b, j: (b, 0, 0)),
                  _const_spec((1, d)), _const_spec(w_pieces.shape),
                  _const_spec(lb_logits.shape), _const_spec((1, HG_V)),
                  _const_spec(tri.shape), _const_spec(mask.shape), _const_spec(w_out.shape)],
        out_specs=pl.BlockSpec((2 * c, d), lambda b, j: (b * nb + j, 0)),
        out_shape=jax.ShapeDtypeStruct((t, d), F32),
        scratch_shapes=[pltpu.VMEM((HG_HEADS, HG_V, HG_K), F32),
                        pltpu.VMEM((2, 4, c, n), F32)],
        compiler_params=_params("parallel", "arbitrary"),
        name="hgrn2_layer",
    )(x, x, mod, norm_gain.reshape(1, d), w_pieces, lb_logits, gain, tri, mask, w_out)


def _pad_cols(w, n):
    return jnp.pad(w, ((0, 0), (0, n - w.shape[1])))


def _mla_weights(w_uq, w_ukv):
    hq = MLA_NOPE + MLA_ROPE
    wq = w_uq.reshape(Q_LORA, MLA_HEADS, hq)
    wq = jnp.pad(wq, ((0, 0), (0, 0), (0, MLA_HEAD_PAD - hq))).reshape(Q_LORA, -1)
    wkv = w_ukv.reshape(KV_LORA, MLA_HEADS, MLA_NOPE + MLA_V)
    wk = jnp.pad(wkv[:, :, :MLA_NOPE], ((0, 0), (0, 0), (0, MLA_HEAD_PAD - MLA_NOPE)))
    wk = wk.reshape(KV_LORA, -1)
    wv = wkv[:, :, MLA_NOPE:].reshape(KV_LORA, -1)
    n_extra = MLA_COLS_PAD - Q_LORA - KV_LORA
    place = np.zeros((n_extra, MLA_HEADS, MLA_HEAD_PAD), np.float32)
    for r in range(MLA_ROPE):
        place[r, :, MLA_NOPE + r] = 1.0
    wk_full = jnp.concatenate([wk, jnp.asarray(place.reshape(n_extra, -1))], axis=0)
    wv_full = jnp.concatenate([wv, jnp.zeros((n_extra, wv.shape[1]), F32)], axis=0)
    return wq.T.astype(BF16), wk_full.astype(BF16), wv_full.T.astype(BF16)


def _rope_inv_freq():
    inv = 1.0 / (ROPE_BASE ** (jnp.arange(0, MLA_ROPE, 2, dtype=F32) / MLA_ROPE))
    return inv.reshape(MLA_ROPE // 2, 1)


def kernel(x, c, positions, ada_w, ada_b, norm_mix, norm_ffn, w_in_even, mla_q_norm, mla_w_uq, mla_kv_norm, mla_w_ukv, rwkv_mu, rwkv_w0, rwkv_w2, rwkv_a0, rwkv_a2, rwkv_g2, rwkv_k_k, rwkv_k_a, rwkv_r_k, rwkv_ln_w, rwkv_ln_b, w_out_even, w_in_odd, hg_lb_logits, hg_out_norm, w_out_odd, ffn_w_gate, ffn_w_up, ffn_w_down, final_norm):
    batch, seq, d = x.shape
    depth = ada_w.shape[0]
    t = batch * seq
    tm = min(512, seq)
    xt = x.reshape(t, d)

    mod_all = _ada_mod(c, ada_w, ada_b).reshape(depth, batch, 6, d)

    for l in range(depth):
        mod = mod_all[l]
        j = l // 2
        if l % 2 == 0:
            w_in = w_in_even[j]
            w_mla = _pad_cols(w_in[:, :MLA_COLS], MLA_COLS_PAD).astype(BF16)
            w_rwkv = w_in[:, MLA_COLS:].astype(BF16)
            p_mla, p_rwkv = _in_proj(xt, mod, norm_mix[l], [w_mla, w_rwkv],
                                     seq=seq, sh=0, sc=1, tm=tm, out_dtype=BF16)
            wqt, wk, wvt = _mla_weights(mla_w_uq[j], mla_w_ukv[j])
            qt, kh, vt = _mla_prep(p_mla, positions, _rope_inv_freq(),
                                   mla_q_norm[j].reshape(1, -1), mla_kv_norm[j].reshape(1, -1),
                                   wqt, wk, wvt, batch=batch, seq=seq, tm=tm, tk=ATTN_TILE)
            y_a = _attention(qt, kh, vt, batch=batch, seq=seq, tq=ATTN_TILE, nh=ATTN_HEADS)

            zeros_lo = jnp.zeros((DECAY_LORA, RWKV_DIM), F32)
            w2 = jnp.concatenate([rwkv_w2[j], zeros_lo], axis=0).astype(BF16)
            a2 = jnp.concatenate([zeros_lo, rwkv_a2[j]], axis=0).astype(BF16)
            vecs = jnp.stack([rwkv_w0[j], rwkv_a0[j], rwkv_k_k[j], rwkv_k_a[j],
                              rwkv_r_k[j].reshape(-1), rwkv_ln_w[j], rwkv_ln_b[j],
                              jnp.zeros((RWKV_DIM,), F32)])
            y_b = _rwkv(p_rwkv, rwkv_mu[j].reshape(1, -1), vecs, w2, a2,
                        rwkv_g2[j].astype(BF16), batch=batch, seq=seq,
                        n_chunks=min(RWKV_BLOCK_CHUNKS, seq // RWKV_CHUNK))
            w_out = w_out_even[j].astype(BF16)
            n_a = MLA_HEADS * MLA_V
            ys, w_outs = [y_a, y_b], [w_out[:n_a], w_out[n_a:]]
        else:
            xt = _hgrn_layer(xt, mod, norm_mix[l], w_in_odd[j].astype(BF16), hg_lb_logits,
                             hg_out_norm[j].reshape(1, -1), w_out_odd[j].astype(BF16),
                             batch=batch, seq=seq, layer=l)
            ys, w_outs = [], []
        xt = _ffn(xt, mod, norm_ffn[l], ffn_w_gate[l].astype(BF16), ffn_w_up[l].astype(BF16),
                  ffn_w_down[l].astype(BF16), final_norm, ys, w_outs, seq=seq, tm=tm,
                  final=(l == depth - 1))
    return xt.reshape(batch, seq, d)
```

```python
import functools

import numpy as np
import jax
import jax.numpy as jnp
from jax import lax
from jax.experimental import pallas as pl
from jax.experimental.pallas import tpu as pltpu

F32 = jnp.float32
BF16 = jnp.bfloat16

RMS_EPS = 1e-6
LOG2_E = 1.4426950408889634
EXP_NEG_HALF = 0.6065306597126334
LANES = 128
VMEM_LIMIT = 56 * 1024 * 1024

MLA_HEADS = 8
MLA_NOPE = 64
MLA_ROPE = 32
MLA_V = 64
Q_LORA = 384
KV_LORA = 256
ROPE_BASE = 10000.0
MLA_COLS = Q_LORA + KV_LORA + MLA_ROPE
MLA_COLS_PAD = 768
MLA_HEAD_PAD = 128
ATTN_TILE = 256
ATTN_HEADS = 4

RWKV_HEAD = 64
RWKV_DIM = 512
RWKV_HEADS = 8
DECAY_LORA = 64
AAA_LORA = 64
GATE_LORA = 128
RWKV_GN_EPS = RWKV_HEAD * 1e-5
RWKV_COLS = 3 * RWKV_DIM + DECAY_LORA + AAA_LORA + GATE_LORA
RWKV_CHUNK = 64
RWKV_BLOCK_CHUNKS = 4
RWKV_GROUP = 4

HG_K = 128
HG_HEADS = 8
HG_V = 128
HG_CHUNK = 128
HG_LEVELS = 7
HG_PROJ_PIECE = 256


def _params(*sem):
    return pltpu.CompilerParams(dimension_semantics=sem, vmem_limit_bytes=VMEM_LIMIT)


def _const_spec(shape):
    nd = len(shape)
    return pl.BlockSpec(shape, lambda *_: (0,) * nd, pipeline_mode=pl.Buffered(1))


def _dot(a, b):
    return jnp.dot(a, b, preferred_element_type=F32)


def _dot_nt(a, b):
    return lax.dot_general(a, b, (((1,), (1,)), ((), ())), preferred_element_type=F32)


def _dot_tn(a, b):
    return lax.dot_general(a, b, (((0,), (0,)), ((), ())), preferred_element_type=F32)


def _dot_split(sel, x):
    hi = x.astype(BF16)
    lo = (x - hi.astype(F32)).astype(BF16)
    return _dot(sel, hi) + _dot(sel, lo)


def _head_sum(x, ind):
    return _dot(x.astype(BF16), ind)


def _ada_kernel(c_ref, w_ref, b_ref, o_ref):
    c = c_ref[...]
    cond = c * jax.nn.sigmoid(c)
    o_ref[0] = _dot(cond.astype(BF16), w_ref[0].astype(BF16)) + b_ref[0]


def _ada_mod(c, ada_w, ada_b):
    depth, d, n = ada_w.shape
    b = c.shape[0]
    tn = n // 4
    return pl.pallas_call(
        _ada_kernel,
        grid=(depth, n // tn),
        in_specs=[
            pl.BlockSpec((b, d), lambda l, j: (0, 0)),
            pl.BlockSpec((1, d, tn), lambda l, j: (l, 0, j)),
            pl.BlockSpec((1, 1, tn), lambda l, j: (l, 0, j)),
        ],
        out_specs=pl.BlockSpec((1, b, tn), lambda l, j: (l, 0, j)),
        out_shape=jax.ShapeDtypeStruct((depth, b, n), F32),
        compiler_params=_params("arbitrary", "arbitrary"),
        name="ada_mod",
    )(c, ada_w, ada_b.reshape(depth, 1, n))


def _norm_mod(x, gain, mod, sh, sc):
    ms = jnp.mean(x * x, axis=-1, keepdims=True)
    y = x * lax.rsqrt(ms + RMS_EPS) * gain
    return y * (1.0 + mod[sc:sc + 1]) + mod[sh:sh + 1]


def _in_proj_kernel(x_ref, mod_ref, gain_ref, *refs, n_out, sh, sc):
    w_refs, o_refs = refs[:n_out], refs[n_out:]
    h = _norm_mod(x_ref[...], gain_ref[...], mod_ref[0], sh, sc).astype(BF16)
    for w_ref, o_ref in zip(w_refs, o_refs):
        o_ref[...] = _dot(h, w_ref[...]).astype(o_ref.dtype)


def _in_proj(x, mod, gain, weights, *, seq, sh, sc, tm, out_dtype):
    t, d = x.shape
    per_b = seq // tm
    n_out = len(weights)
    in_specs = [
        pl.BlockSpec((tm, d), lambda i: (i, 0)),
        pl.BlockSpec((1, 6, d), lambda i: (i // per_b, 0, 0)),
        _const_spec((1, d)),
    ] + [_const_spec(w.shape) for w in weights]
    out_specs = [pl.BlockSpec((tm, w.shape[1]), lambda i: (i, 0)) for w in weights]
    out_shape = [jax.ShapeDtypeStruct((t, w.shape[1]), out_dtype) for w in weights]
    return pl.pallas_call(
        functools.partial(_in_proj_kernel, n_out=n_out, sh=sh, sc=sc),
        grid=(t // tm,),
        in_specs=in_specs,
        out_specs=out_specs,
        out_shape=out_shape,
        compiler_params=_params("parallel"),
        name="in_proj",
    )(x, mod, gain.reshape(1, d), *weights)


def _ffn_kernel(x_ref, mod_ref, gain_ref, wg_ref, wu_ref, wd_ref, fin_ref, *refs, n_in, final):
    y_refs, w_refs, o_ref = refs[:n_in], refs[n_in:2 * n_in], refs[2 * n_in]
    x = x_ref[...]
    mod = mod_ref[0]
    if n_in:
        mix = _dot(y_refs[0][...], w_refs[0][...])
        for y_ref, w_ref in zip(y_refs[1:], w_refs[1:]):
            mix += _dot(y_ref[...], w_ref[...])
        x = x + mod[2:3] * mix
    h = _norm_mod(x, gain_ref[...], mod, 3, 4).astype(BF16)
    gate = _dot(h, wg_ref[...])
    up = _dot(h, wu_ref[...])
    act = (gate * jax.nn.sigmoid(gate) * up).astype(BF16)
    y = x + mod[5:6] * _dot(act, wd_ref[...])
    if final:
        ms = jnp.mean(y * y, axis=-1, keepdims=True)
        y = y * lax.rsqrt(ms + RMS_EPS) * fin_ref[...]
    o_ref[...] = y


def _ffn(x, mod, gain, wg, wu, wd, fin, ys, w_outs, *, seq, tm, final):
    t, d = x.shape
    per_b = seq // tm
    return pl.pallas_call(
        functools.partial(_ffn_kernel, n_in=len(ys), final=final),
        grid=(t // tm,),
        in_specs=[
            pl.BlockSpec((tm, d), lambda i: (i, 0)),
            pl.BlockSpec((1, 6, d), lambda i: (i // per_b, 0, 0)),
            _const_spec((1, d)),
            _const_spec(wg.shape),
            _const_spec(wu.shape),
            _const_spec(wd.shape),
            _const_spec((1, d)),
        ] + [pl.BlockSpec((tm, y.shape[1]), lambda i: (i, 0)) for y in ys]
          + [_const_spec(w.shape) for w in w_outs],
        out_specs=pl.BlockSpec((tm, d), lambda i: (i, 0)),
        out_shape=jax.ShapeDtypeStruct((t, d), F32),
        compiler_params=_params("parallel"),
        name="ffn",
    )(x, mod, gain.reshape(1, d), wg, wu, wd, fin.reshape(1, d), *ys, *w_outs)


def _mla_prep_kernel(p_ref, pos_ref, invf_ref, qn_ref, kvn_ref, wqt_ref, wk_ref, wvt_ref,
                     qt_ref, k_ref, vt_ref, *, scale, tk):
    p = p_ref[...].astype(F32)
    tm = p.shape[0]
    half = MLA_ROPE // 2
    ang = invf_ref[...] * pos_ref[0].astype(F32)
    cos_t, sin_t = jnp.cos(ang), jnp.sin(ang)

    c_q = p[:, :Q_LORA]
    ms = jnp.mean(c_q * c_q, axis=-1, keepdims=True)
    cqn = (c_q * lax.rsqrt(ms + RMS_EPS) * qn_ref[...]).astype(BF16)
    qt = _dot_nt(wqt_ref[...], cqn) * scale
    pieces = []
    for h in range(MLA_HEADS):
        base = h * MLA_HEAD_PAD
        x1 = qt[base + MLA_NOPE:base + MLA_NOPE + half]
        x2 = qt[base + MLA_NOPE + half:base + MLA_NOPE + MLA_ROPE]
        pieces += [qt[base:base + MLA_NOPE], x1 * cos_t - x2 * sin_t, x1 * sin_t + x2 * cos_t,
                   qt[base + MLA_NOPE + MLA_ROPE:base + MLA_HEAD_PAD]]
    qt_ref[0] = jnp.concatenate(pieces, axis=0).astype(qt_ref.dtype)

    ckv = p[:, Q_LORA:Q_LORA + KV_LORA]
    ss = jnp.mean(ckv * ckv, axis=-1, keepdims=True)
    ckvn = ckv * lax.rsqrt(ss + RMS_EPS) * kvn_ref[...]
    kr = p[:, Q_LORA + KV_LORA:]
    rest = kr.shape[1] - MLA_ROPE
    cos_k = jnp.concatenate([cos_t, cos_t, jnp.ones((rest, tm), F32)], axis=0).T
    sin_k = jnp.concatenate([sin_t, sin_t, jnp.zeros((rest, tm), F32)], axis=0).T
    lane = lax.broadcasted_iota(jnp.int32, kr.shape, 1)
    partner = jnp.where(lane < half, -pltpu.roll(kr, kr.shape[1] - half, 1), pltpu.roll(kr, half, 1))
    lhs = jnp.concatenate([ckvn, kr * cos_k + partner * sin_k], axis=1).astype(BF16)
    k_ref[...] = _dot(lhs, wk_ref[...]).astype(k_ref.dtype)
    vt = _dot_nt(wvt_ref[...], lhs).astype(vt_ref.dtype)
    for jj in range(tm // tk):
        vt_ref[0, jj] = vt[:, jj * tk:(jj + 1) * tk]


def _mla_prep(p_mla, pos, invf, q_norm, kv_norm, wqt, wk, wvt, *, batch, seq, tm, tk):
    t = p_mla.shape[0]
    n_q = MLA_HEADS * MLA_HEAD_PAD
    n_v = MLA_HEADS * MLA_V
    per_b = seq // tm
    scale = (MLA_NOPE + MLA_ROPE) ** -0.5 * LOG2_E
    return pl.pallas_call(
        functools.partial(_mla_prep_kernel, scale=scale, tk=tk),
        grid=(t // tm,),
        in_specs=[
            pl.BlockSpec((tm, MLA_COLS_PAD), lambda i: (i, 0)),
            pl.BlockSpec((1, 1, tm), lambda i: (i, 0, 0)),
            _const_spec(invf.shape),
            _const_spec((1, Q_LORA)),
            _const_spec((1, KV_LORA)),
            _const_spec(wqt.shape),
            _const_spec(wk.shape),
            _const_spec(wvt.shape),
        ],
        out_specs=[
            pl.BlockSpec((1, n_q, tm), lambda i: (i // per_b, 0, i % per_b)),
            pl.BlockSpec((tm, n_q), lambda i: (i, 0)),
            pl.BlockSpec((1, tm // tk, n_v, tk), lambda i: (i // per_b, i % per_b, 0, 0)),
        ],
        out_shape=[
            jax.ShapeDtypeStruct((batch, n_q, seq), BF16),
            jax.ShapeDtypeStruct((t, n_q), BF16),
            jax.ShapeDtypeStruct((batch, seq // tk, n_v, tk), BF16),
        ],
        compiler_params=_params("parallel"),
        name="mla_prep",
    )(p_mla, pos.reshape(t // tm, 1, tm), invf, q_norm, kv_norm, wqt, wk, wvt)


def _attn_kernel(qt_ref, k_ref, vt_ref, o_ref, s_ref, p_ref, *, tq, nh):
    i = pl.program_id(2)
    qts = [qt_ref[0, h * MLA_HEAD_PAD:(h + 1) * MLA_HEAD_PAD, :] for h in range(nh)]
    causal = (lax.broadcasted_iota(jnp.int32, (tq, tq), 0)
              <= lax.broadcasted_iota(jnp.int32, (tq, tq), 1))

    def put_scores(j, slot):
        start = pl.multiple_of(j * tq, tq)
        for h in range(nh):
            s_ref[slot, h] = _dot(
                k_ref[pl.ds(start, tq), h * MLA_HEAD_PAD:(h + 1) * MLA_HEAD_PAD], qts[h])

    def value_dots(j, probs):
        return [_dot(vt_ref[0, j, h * MLA_V:(h + 1) * MLA_V, :], probs[h]) for h in range(nh)]

    def step(carry, slot, pv_prev, masked):
        new, probs = [], []
        for h in range(nh):
            m, l, acc = carry[h]
            s = s_ref[slot, h]
            if masked:
                s = jnp.where(causal, s, -jnp.inf)
            m_new = jnp.maximum(m, jnp.max(s, axis=0, keepdims=True))
            pr = jnp.exp2(s - m_new)
            alpha = jnp.exp2(m - m_new)
            probs.append(pr.astype(BF16))
            new.append((m_new, alpha * l + jnp.sum(pr, axis=0, keepdims=True),
                        alpha * (acc + pv_prev[h])))
        return tuple(new), probs

    def parked(slot):
        return [p_ref[slot, h] for h in range(nh)]

    def pair(t, carry):
        j = 2 * t
        put_scores(j + 1, 1)
        pv = value_dots(jnp.where(t == 0, i, j - 1), parked(1))
        carry, probs = step(carry, 0, pv, False)
        for h in range(nh):
            p_ref[0, h] = probs[h]
        put_scores(j + 2, 0)
        pv = value_dots(j, parked(0))
        carry, probs = step(carry, 1, pv, False)
        for h in range(nh):
            p_ref[1, h] = probs[h]
        return carry

    def emit(carry, pv_last):
        outs = [(acc + pv_last[h]) * (1.0 / l) for h, (_, l, acc) in enumerate(carry)]
        o_ref[...] = jnp.concatenate(outs, axis=0).T.astype(o_ref.dtype)

    put_scores(i, 1)
    put_scores(0, 0)
    zeros = [jnp.zeros((MLA_V, tq), F32)] * nh
    init = tuple((jnp.full((1, tq), -jnp.inf, F32), jnp.zeros((1, tq), F32), zeros[h])
                 for h in range(nh))
    carry, probs = step(init, 1, zeros, True)
    for h in range(nh):
        p_ref[1, h] = probs[h]
    carry = lax.fori_loop(0, i // 2, pair, carry)
    n_done = 2 * (i // 2)
    last = jnp.where(n_done == 0, i, n_done - 1)

    @pl.when(i % 2 == 0)
    def _():
        emit(carry, value_dots(last, parked(1)))

    @pl.when(i % 2 == 1)
    def _():
        final, probs = step(carry, 0, value_dots(last, parked(1)), False)
        emit(final, value_dots(i - 1, probs))


def _attention(qt, k, vt, *, batch, seq, tq, nh):
    t = k.shape[0]
    nq = seq // tq
    return pl.pallas_call(
        functools.partial(_attn_kernel, tq=tq, nh=nh),
        grid=(batch, MLA_HEADS // nh, nq),
        in_specs=[
            pl.BlockSpec((1, nh * MLA_HEAD_PAD, tq), lambda b, h, i: (b, h, i)),
            pl.BlockSpec((seq, nh * MLA_HEAD_PAD), lambda b, h, i: (b, h)),
            pl.BlockSpec((1, seq // tq, nh * MLA_V, tq), lambda b, h, i: (b, 0, h, 0)),
        ],
        out_specs=pl.BlockSpec((tq, nh * MLA_V), lambda b, h, i: (b * nq + i, h)),
        out_shape=jax.ShapeDtypeStruct((t, MLA_HEADS * MLA_V), BF16),
        scratch_shapes=[pltpu.VMEM((2, nh, tq, tq), F32), pltpu.VMEM((2, nh, tq, tq), BF16)],
        compiler_params=_params("parallel", "parallel", "arbitrary"),
        name="mla_attention",
    )(qt, k, vt)


def _stack_heads(x, lane_head):
    return jnp.concatenate(
        [jnp.where(lane_head == h, x, 0.0) for h in range(RWKV_GROUP)], axis=0)


def _rwkv_kernel(p_ref, mu_ref, vec_ref, w2_ref, a2_ref, g2_ref, ind_ref, tri_ref,
                 o_ref, state_ref, prev_ref, *, n_chunks):
    c = RWKV_CHUNK
    gw = RWKV_GROUP * RWKV_HEAD
    gc = RWKV_GROUP * c
    n_groups = RWKV_HEADS // RWKV_GROUP
    tb = n_chunks * c

    @pl.when(pl.program_id(1) == 0)
    def _():
        state_ref[...] = jnp.zeros_like(state_ref)
        prev_ref[...] = jnp.zeros_like(prev_ref)

    p = p_ref[...].astype(F32)
    rows = lax.broadcasted_iota(jnp.int32, p.shape, 0)
    shifted = jnp.where(rows == 0, prev_ref[...], pltpu.roll(p, 1, 0))
    prev_ref[...] = p[tb - 1:tb, :]
    p = p + (shifted - p) * mu_ref[...]

    d = RWKV_DIM
    r, k, v = p[:, :d], p[:, d:2 * d], p[:, 2 * d:3 * d]
    lo = p[:, 3 * d:3 * d + DECAY_LORA + AAA_LORA]
    g_lo = p[:, 3 * d + DECAY_LORA + AAA_LORA:]
    w0, a0, k_k, k_a = vec_ref[0:1], vec_ref[1:2], vec_ref[2:3], vec_ref[3:4]
    r_k, ln_w, ln_b = vec_ref[4:5], vec_ref[5:6], vec_ref[6:7]

    x_w = w0 + _dot(jnp.tanh(lo).astype(BF16), w2_ref[...])
    lw = -EXP_NEG_HALF * jax.nn.sigmoid(x_w)
    a = jax.nn.sigmoid(a0 + _dot(lo.astype(BF16), a2_ref[...]))
    g = _dot(jax.nn.sigmoid(g_lo).astype(BF16), g2_ref[...])

    ind = ind_ref[...]
    kk = k * k_k
    kk = kk * lax.rsqrt(jnp.maximum(_head_sum(kk * kk, ind), 1e-24))
    k2 = k * (1.0 + (a - 1.0) * k_a)
    alpha = kk * a

    b = _dot_split(tri_ref[...], lw)
    b_prev = b - lw
    bcast = lambda row: jnp.broadcast_to(row, (c, d))
    b_mid = jnp.concatenate([bcast(b[ci * c + c // 2 - 1:ci * c + c // 2]) for ci in range(n_chunks)], axis=0)
    b_end = jnp.concatenate([bcast(b[ci * c + c - 1:ci * c + c]) for ci in range(n_chunks)], axis=0)
    kap_t = kk * jnp.exp(b_prev - b_mid)
    r_t = r * jnp.exp(b - b_mid)
    inv_t = jnp.exp(b_mid - b)
    alp_t, k_t = alpha * inv_t, k2 * inv_t
    kap_h = kk * jnp.exp(b_prev)
    r_h = r * jnp.exp(b)
    tail = jnp.exp(b_end - b)
    alp_c, k_c = alpha * tail, k2 * tail
    gam_end = jnp.exp(b_end)

    eye = (lax.broadcasted_iota(jnp.int32, (gc, gc), 0)
           == lax.broadcasted_iota(jnp.int32, (gc, gc), 1)).astype(F32)
    lane_head = lax.broadcasted_iota(jnp.int32, (c, gw), 1) // RWKV_HEAD

    def stacked(x, ci, gi):
        return _stack_heads(x[ci * c:(ci + 1) * c, gi * gw:(gi + 1) * gw], lane_head)

    pairs = [(ci, gi) for ci in range(n_chunks) for gi in range(n_groups)]
    half = gc // 2
    lane_pair = lax.broadcasted_iota(jnp.int32, (c, half), 1) // RWKV_HEAD
    tri_r = lax.broadcasted_iota(jnp.int32, (half, half), 0) % c
    tri_c = lax.broadcasted_iota(jnp.int32, (half, half), 1) % c
    zero_blk = jnp.zeros((half, half), F32)

    def pair_stack(x, ci, pi):
        xs = x[ci * c:(ci + 1) * c, pi * half:(pi + 1) * half]
        return jnp.concatenate([jnp.where(lane_pair == k, xs, 0.0) for k in range(2)], axis=0)

    def block_diag(a, b_):
        return jnp.concatenate([jnp.concatenate([a, zero_blk], axis=1),
                                jnp.concatenate([zero_blk, b_], axis=1)], axis=0)

    a_ka, a_kk, a_ra, a_rk, vs, n_pow, inv = {}, {}, {}, {}, {}, {}, {}
    for pr in pairs:
        ci, gi = pr
        blocks = []
        for pi in (2 * gi, 2 * gi + 1):
            ps = lambda x: pair_stack(x, ci, pi)
            kr = jnp.concatenate([ps(kap_t), ps(r_t)], axis=0).astype(BF16)
            ak = jnp.concatenate([ps(alp_t), ps(k_t)], axis=0).astype(BF16)
            blocks.append(_dot_nt(kr, ak))
        quad = lambda r0, c0, keep: block_diag(
            *[jnp.where(keep, blk[r0:r0 + half, c0:c0 + half], 0.0) for blk in blocks])
        a_ka[pr] = quad(0, 0, tri_c < tri_r)
        a_kk[pr] = quad(0, half, tri_c < tri_r).astype(BF16)
        a_ra[pr] = quad(half, 0, tri_c <= tri_r).astype(BF16)
        a_rk[pr] = quad(half, half, tri_c <= tri_r).astype(BF16)
        vs[pr] = stacked(v, *pr).astype(BF16)
    for pr in pairs:
        n_pow[pr] = a_ka[pr]
        inv[pr] = eye - n_pow[pr]
    for _ in range(int(np.log2(c)) - 1):
        for pr in pairs:
            pwb = n_pow[pr].astype(BF16)
            n_pow[pr] = _dot(pwb, pwb)
        for pr in pairs:
            inv[pr] = inv[pr] + _dot(inv[pr].astype(BF16), n_pow[pr].astype(BF16))
    akv = {pr: _dot(a_kk[pr], vs[pr]).astype(BF16) for pr in pairs}
    pre = {}
    for pr in pairs:
        st = lambda x: stacked(x, *pr)
        invb = inv[pr].astype(BF16)
        kt = _dot(invb, st(kap_h).astype(BF16)).astype(BF16)
        w_mat = _dot(invb, akv[pr])
        akc = jnp.concatenate([st(alp_c), st(k_c)], axis=0).astype(BF16)
        pre[pr] = (kt, w_mat, a_ra[pr], a_rk[pr], st(r_h).astype(BF16), akc)

    y_rows = []
    for ci in range(n_chunks):
        s0 = [state_ref[gi] for gi in range(n_groups)]
        s0b = [x.astype(BF16) for x in s0]
        ub = [(-(_dot_nt(pre[ci, gi][0], s0b[gi]) + pre[ci, gi][1])).astype(BF16)
              for gi in range(n_groups)]
        ys = []
        for gi in range(n_groups):
            kt, w_mat, a_ra, a_rk, rh, akc = pre[ci, gi]
            uv = jnp.concatenate([ub[gi], vs[ci, gi]], axis=0)
            sl = slice(gi * gw, (gi + 1) * gw)
            state_ref[gi] = s0[gi] * gam_end[ci * c:ci * c + 1, sl] + _dot_tn(uv, akc)
            y_st = _dot_nt(rh, s0b[gi]) + _dot(a_ra, ub[gi]) + _dot(a_rk, vs[ci, gi])
            y = y_st[0:c]
            for h in range(1, RWKV_GROUP):
                y = y + y_st[h * c:(h + 1) * c]
            ys.append(y)
        y_rows.append(jnp.concatenate(ys, axis=1))
    y = jnp.concatenate(y_rows, axis=0)

    inv_n = 1.0 / RWKV_HEAD
    mean = _head_sum(y, ind) * inv_n
    yc = y - mean
    var = _head_sum(yc * yc, ind) * inv_n
    y = yc * lax.rsqrt(var + RWKV_GN_EPS) * ln_w + ln_b
    bonus = _head_sum(r * k2 * r_k, ind) * v
    o_ref[...] = ((y + bonus) * g).astype(o_ref.dtype)


def _rwkv(p_rwkv, mu, vecs, w2, a2, g2, *, batch, seq, n_chunks):
    t = p_rwkv.shape[0]
    c = RWKV_CHUNK
    tb = n_chunks * c
    nb = seq // tb
    gw = RWKV_GROUP * RWKV_HEAD
    head = np.arange(RWKV_DIM) // RWKV_HEAD
    ind = jnp.asarray(head[:, None] == head[None, :], BF16)
    tt = np.arange(tb)
    tri = jnp.asarray((tt[None, :] <= tt[:, None]) & (tt[None, :] // c == tt[:, None] // c), BF16)
    return pl.pallas_call(
        functools.partial(_rwkv_kernel, n_chunks=n_chunks),
        grid=(batch, nb),
        in_specs=[
            pl.BlockSpec((tb, RWKV_COLS), lambda b, j: (b * nb + j, 0)),
            _const_spec((1, RWKV_COLS)),
            _const_spec(vecs.shape),
            _const_spec(w2.shape),
            _const_spec(a2.shape),
            _const_spec(g2.shape),
            _const_spec(ind.shape),
            _const_spec(tri.shape),
        ],
        out_specs=pl.BlockSpec((tb, RWKV_DIM), lambda b, j: (b * nb + j, 0)),
        out_shape=jax.ShapeDtypeStruct((t, RWKV_DIM), BF16),
        scratch_shapes=[
            pltpu.VMEM((RWKV_HEADS // RWKV_GROUP, gw, gw), F32),
            pltpu.VMEM((1, RWKV_COLS), F32),
        ],
        compiler_params=_params("parallel", "arbitrary"),
        name="rwkv7",
    )(p_rwkv, mu, vecs, w2, a2, g2, ind, tri)


def _hgrn_masks():
    c = HG_CHUNK
    t = np.arange(c)
    mask = np.zeros((HG_LEVELS + 1, c, c), np.float32)
    mask[0] = np.eye(c)
    for l in range(1, HG_LEVELS + 1):
        m = 1 << l
        is_right = (t % m) >= m // 2
        same = (t[:, None] // m) == (t[None, :] // m)
        mask[l] = same & is_right[:, None] & ~is_right[None, :]
    return mask


def _hgrn_chunk(proj, lb, gain, tri, mask_ref, state_ref, side_jobs):
    c = HG_CHUNK
    jobs = iter(side_jobs)
    run_job = lambda: next(jobs, lambda: None)()
    q, f, vals, g = proj
    n = q.shape[1]
    q = q * jax.nn.sigmoid(q)
    forget = lb + (1.0 - lb) * jax.nn.sigmoid(f)
    key = 1.0 - forget
    logf = jnp.log(forget)

    b = _dot_split(tri, logf)
    e_cum = jnp.exp(b)
    b_end = b[c - 1:c, :]
    q_hat = (q * e_cum).astype(BF16)
    k_hat = (key * jnp.exp(b_end - b)).astype(BF16)
    decay_end = e_cum[c - 1:c, :]
    qb, kb = q.astype(BF16), key.astype(BF16)
    run_job()

    t_idx = lax.broadcasted_iota(jnp.int32, q.shape, 0)
    f_prev = pltpu.roll(forget, 1, 0)
    f_next = pltpu.roll(forget, c - 1, 0)
    level_ops = []
    for l in range(HG_LEVELS):
        m = 2 << l
        is_right = (t_idx & (m // 2)) != 0
        if m == 2:
            decay = jnp.where(is_right, forget, 1.0)
        elif m == 4:
            pos = t_idx & 3
            decay = jnp.where(pos == 0, f_next,
                              jnp.where(pos == 1, 1.0, jnp.where(pos == 2, forget, forget * f_prev)))
        else:
            b3 = b.reshape(c // m, m, n)
            b_ref = jnp.broadcast_to(b3[:, m // 2 - 1:m // 2, :], (c // m, m, n)).reshape(c, n)
            decay = jnp.exp(-jnp.abs(b - b_ref))
        level_ops.append((jnp.where(is_right, q, key) * decay).astype(BF16))
        run_job()

    heads = [slice(h * HG_K, (h + 1) * HG_K) for h in range(HG_HEADS)]
    attns = []
    for sl in heads:
        attn = mask_ref[0] * _dot_nt(qb[:, sl], kb[:, sl])
        for l in range(HG_LEVELS):
            ml = level_ops[l][:, sl]
            attn = attn + mask_ref[l + 1] * _dot_nt(ml, ml)
        attns.append(attn.astype(BF16))
        run_job()
    vhs = [vals[:, sl].astype(BF16) for sl in heads]
    sts = [state_ref[h] for h in range(HG_HEADS)]
    os = [_dot_nt(q_hat[:, sl], sts[h].astype(BF16)) + _dot(attns[h], vhs[h])
          for h, sl in enumerate(heads)]
    for h, sl in enumerate(heads):
        state_ref[h] = sts[h] * decay_end[:, sl] + _dot_tn(vhs[h], k_hat[:, sl])
    outs = []
    for h, sl in enumerate(heads):
        o = os[h]
        ms = jnp.mean(o * o, axis=-1, keepdims=True)
        gt = g[:, sl]
        outs.append((o * lax.rsqrt(ms + RMS_EPS) * gain * (gt * jax.nn.sigmoid(gt))).astype(BF16))
    for job in jobs:
        job()
    return jnp.concatenate(outs, axis=1)


def _hgrn_layer_kernel(x_ref, xn_ref, mod_ref, ngain_ref, win_ref, lbl_ref, gain_ref, tri_ref,
                       mask_ref, wout_ref, o_ref, state_ref, proj_ref, *, layer):
    c = HG_CHUNK
    mod = mod_ref[0]
    per_proj = proj_ref.shape[3] // HG_PROJ_PIECE

    def projection_jobs(x, slot):
        hin = _norm_mod(x, ngain_ref[...], mod, 0, 1).astype(BF16)

        def piece(k, p):
            def job():
                lo = p * HG_PROJ_PIECE
                proj_ref[slot, k, :, lo:lo + HG_PROJ_PIECE] = _dot(hin, win_ref[k * per_proj + p])
            return job

        return [piece(k, p) for k in range(4) for p in range(per_proj)]

    @pl.when(pl.program_id(1) == 0)
    def _():
        state_ref[...] = jnp.zeros_like(state_ref)
        for job in projection_jobs(x_ref[0:c, :], 0):
            job()

    logits = lbl_ref[...]
    pe = jnp.exp(logits - jnp.max(logits, axis=0, keepdims=True))
    lb = jnp.sum(pe[1:layer + 1], axis=0, keepdims=True) / jnp.sum(pe, axis=0, keepdims=True)

    def recur(slot, side_jobs):
        proj = tuple(proj_ref[slot, k] for k in range(4))
        return _hgrn_chunk(proj, lb, gain_ref[...], tri_ref[...], mask_ref, state_ref, side_jobs)

    y0 = recur(0, projection_jobs(x_ref[c:2 * c, :], 1))
    y1 = recur(1, projection_jobs(xn_ref[...], 0))
    y = jnp.concatenate([y0, y1], axis=0)
    o_ref[...] = x_ref[...] + mod[2:3] * _dot(y, wout_ref[...])


def _hgrn_layer(x, mod, norm_gain, w_in, lb_logits, gain, w_out, *, batch, seq, layer):
    t, d = x.shape
    c = HG_CHUNK
    n = HG_HEADS * HG_K
    nb = seq // (2 * c)
    tri = jnp.asarray(np.tril(np.ones((c, c))), BF16)
    mask = jnp.asarray(_hgrn_masks(), F32)
    w_pieces = w_in.reshape(d, 4 * n // HG_PROJ_PIECE, HG_PROJ_PIECE).transpose(1, 0, 2)
    next_chunk = lambda b, j: (b * 2 * nb + jnp.minimum(2 * j + 2, 2 * nb - 1), 0)
    return pl.pallas_call(
        functools.partial(_hgrn_layer_kernel, layer=layer),
        grid=(batch, nb),
        in_specs=[pl.BlockSpec((2 * c, d), lambda b, j: (b * nb + j, 0)),
                  pl.BlockSpec((c, d), next_chunk),
                  pl.BlockSpec((1, 6, d), lambda b, j: (b, 0, 0)),
                  _const_spec((1, d)), _const_spec(w_pieces.shape),
                  _const_spec(lb_logits.shape), _const_spec((1, HG_V)),
                  _const_spec(tri.shape), _const_spec(mask.shape), _const_spec(w_out.shape)],
        out_specs=pl.BlockSpec((2 * c, d), lambda b, j: (b * nb + j, 0)),
        out_shape=jax.ShapeDtypeStruct((t, d), F32),
        scratch_shapes=[pltpu.VMEM((HG_HEADS, HG_V, HG_K), F32),
                        pltpu.VMEM((2, 4, c, n), F32)],
        compiler_params=_params("parallel", "arbitrary"),
        name="hgrn2_layer",
    )(x, x, mod, norm_gain.reshape(1, d), w_pieces, lb_logits, gain, tri, mask, w_out)


def _pad_cols(w, n):
    return jnp.pad(w, ((0, 0), (0, n - w.shape[1])))


def _mla_weights(w_uq, w_ukv):
    hq = MLA_NOPE + MLA_ROPE
    wq = w_uq.reshape(Q_LORA, MLA_HEADS, hq)
    wq = jnp.pad(wq, ((0, 0), (0, 0), (0, MLA_HEAD_PAD - hq))).reshape(Q_LORA, -1)
    wkv = w_ukv.reshape(KV_LORA, MLA_HEADS, MLA_NOPE + MLA_V)
    wk = jnp.pad(wkv[:, :, :MLA_NOPE], ((0, 0), (0, 0), (0, MLA_HEAD_PAD - MLA_NOPE)))
    wk = wk.reshape(KV_LORA, -1)
    wv = wkv[:, :, MLA_NOPE:].reshape(KV_LORA, -1)
    n_extra = MLA_COLS_PAD - Q_LORA - KV_LORA
    place = np.zeros((n_extra, MLA_HEADS, MLA_HEAD_PAD), np.float32)
    for r in range(MLA_ROPE):
        place[r, :, MLA_NOPE + r] = 1.0
    wk_full = jnp.concatenate([wk, jnp.asarray(place.reshape(n_extra, -1))], axis=0)
    wv_full = jnp.concatenate([wv, jnp.zeros((n_extra, wv.shape[1]), F32)], axis=0)
    return wq.T.astype(BF16), wk_full.astype(BF16), wv_full.T.astype(BF16)


def _rope_inv_freq():
    inv = 1.0 / (ROPE_BASE ** (jnp.arange(0, MLA_ROPE, 2, dtype=F32) / MLA_ROPE))
    return inv.reshape(MLA_ROPE // 2, 1)


def kernel(x, c, positions, ada_w, ada_b, norm_mix, norm_ffn, w_in_even, mla_q_norm, mla_w_uq, mla_kv_norm, mla_w_ukv, rwkv_mu, rwkv_w0, rwkv_w2, rwkv_a0, rwkv_a2, rwkv_g2, rwkv_k_k, rwkv_k_a, rwkv_r_k, rwkv_ln_w, rwkv_ln_b, w_out_even, w_in_odd, hg_lb_logits, hg_out_norm, w_out_odd, ffn_w_gate, ffn_w_up, ffn_w_down, final_norm):
    batch, seq, d = x.shape
    depth = ada_w.shape[0]
    t = batch * seq
    tm = min(512, seq)
    xt = x.reshape(t, d)

    mod_all = _ada_mod(c, ada_w, ada_b).reshape(depth, batch, 6, d)

    for l in range(depth):
        mod = mod_all[l]
        j = l // 2
        if l % 2 == 0:
            w_in = w_in_even[j]
            w_mla = _pad_cols(w_in[:, :MLA_COLS], MLA_COLS_PAD).astype(BF16)
            w_rwkv = w_in[:, MLA_COLS:].astype(BF16)
            p_mla, p_rwkv = _in_proj(xt, mod, norm_mix[l], [w_mla, w_rwkv],
                                     seq=seq, sh=0, sc=1, tm=tm, out_dtype=BF16)
            wqt, wk, wvt = _mla_weights(mla_w_uq[j], mla_w_ukv[j])
            qt, kh, vt = _mla_prep(p_mla, positions, _rope_inv_freq(),
                                   mla_q_norm[j].reshape(1, -1), mla_kv_norm[j].reshape(1, -1),
                                   wqt, wk, wvt, batch=batch, seq=seq, tm=tm, tk=ATTN_TILE)
            y_a = _attention(qt, kh, vt, batch=batch, seq=seq, tq=ATTN_TILE, nh=ATTN_HEADS)

            zeros_lo = jnp.zeros((DECAY_LORA, RWKV_DIM), F32)
            w2 = jnp.concatenate([rwkv_w2[j], zeros_lo], axis=0).astype(BF16)
            a2 = jnp.concatenate([zeros_lo, rwkv_a2[j]], axis=0).astype(BF16)
            vecs = jnp.stack([rwkv_w0[j], rwkv_a0[j], rwkv_k_k[j], rwkv_k_a[j],
                              rwkv_r_k[j].reshape(-1), rwkv_ln_w[j], rwkv_ln_b[j],
                              jnp.zeros((RWKV_DIM,), F32)])
            y_b = _rwkv(p_rwkv, rwkv_mu[j].reshape(1, -1), vecs, w2, a2,
                        rwkv_g2[j].astype(BF16), batch=batch, seq=seq,
                        n_chunks=min(RWKV_BLOCK_CHUNKS, seq // RWKV_CHUNK))
            w_out = w_out_even[j].astype(BF16)
            n_a = MLA_HEADS * MLA_V
            ys, w_outs = [y_a, y_b], [w_out[:n_a], w_out[n_a:]]
        else:
            xt = _hgrn_layer(xt, mod, norm_mix[l], w_in_odd[j].astype(BF16), hg_lb_logits,
                             hg_out_norm[j].reshape(1, -1), w_out_odd[j].astype(BF16),
                             batch=batch, seq=seq, layer=l)
            ys, w_outs = [], []
        xt = _ffn(xt, mod, norm_ffn[l], ffn_w_gate[l].astype(BF16), ffn_w_up[l].astype(BF16),
                  ffn_w_down[l].astype(BF16), final_norm, ys, w_outs, seq=seq, tm=tm,
                  final=(l == depth - 1))
    return xt.reshape(batch, seq, d)
```

```python
import functools

import numpy as np
import jax
import jax.numpy as jnp
from jax import lax
from jax.experimental import pallas as pl
from jax.experimental.pallas import tpu as pltpu

F32 = jnp.float32
BF16 = jnp.bfloat16

RMS_EPS = 1e-6
LOG2_E = 1.4426950408889634
EXP_NEG_HALF = 0.6065306597126334
LANES = 128
VMEM_LIMIT = 56 * 1024 * 1024

MLA_HEADS = 8
MLA_NOPE = 64
MLA_ROPE = 32
MLA_V = 64
Q_LORA = 384
KV_LORA = 256
ROPE_BASE = 10000.0
MLA_COLS = Q_LORA + KV_LORA + MLA_ROPE
MLA_COLS_PAD = 768
MLA_HEAD_PAD = 128
ATTN_TILE = 256
ATTN_HEADS = 4

RWKV_HEAD = 64
RWKV_DIM = 512
RWKV_HEADS = 8
DECAY_LORA = 64
AAA_LORA = 64
GATE_LORA = 128
RWKV_GN_EPS = RWKV_HEAD * 1e-5
RWKV_COLS = 3 * RWKV_DIM + DECAY_LORA + AAA_LORA + GATE_LORA
RWKV_CHUNK = 64
RWKV_BLOCK_CHUNKS = 4
RWKV_GROUP = 4

HG_K = 128
HG_HEADS = 8
HG_V = 128
HG_CHUNK = 128
HG_LEVELS = 7
HG_PROJ_PIECE = 256


def _params(*sem):
    return pltpu.CompilerParams(dimension_semantics=sem, vmem_limit_bytes=VMEM_LIMIT)


def _const_spec(shape):
    nd = len(shape)
    return pl.BlockSpec(shape, lambda *_: (0,) * nd, pipeline_mode=pl.Buffered(1))


def _dot(a, b):
    return jnp.dot(a, b, preferred_element_type=F32)


def _dot_nt(a, b):
    return lax.dot_general(a, b, (((1,), (1,)), ((), ())), preferred_element_type=F32)


def _dot_tn(a, b):
    return lax.dot_general(a, b, (((0,), (0,)), ((), ())), preferred_element_type=F32)


def _dot_split(sel, x):
    hi = x.astype(BF16)
    lo = (x - hi.astype(F32)).astype(BF16)
    return _dot(sel, hi) + _dot(sel, lo)


def _head_sum(x, ind):
    return _dot(x.astype(BF16), ind)


def _ada_kernel(c_ref, w_ref, b_ref, o_ref):
    c = c_ref[...]
    cond = c * jax.nn.sigmoid(c)
    o_ref[0] = _dot(cond.astype(BF16), w_ref[0].astype(BF16)) + b_ref[0]


def _ada_mod(c, ada_w, ada_b):
    depth, d, n = ada_w.shape
    b = c.shape[0]
    tn = n // 4
    return pl.pallas_call(
        _ada_kernel,
        grid=(depth, n // tn),
        in_specs=[
            pl.BlockSpec((b, d), lambda l, j: (0, 0)),
            pl.BlockSpec((1, d, tn), lambda l, j: (l, 0, j)),
            pl.BlockSpec((1, 1, tn), lambda l, j: (l, 0, j)),
        ],
        out_specs=pl.BlockSpec((1, b, tn), lambda l, j: (l, 0, j)),
        out_shape=jax.ShapeDtypeStruct((depth, b, n), F32),
        compiler_params=_params("arbitrary", "arbitrary"),
        name="ada_mod",
    )(c, ada_w, ada_b.reshape(depth, 1, n))


def _norm_mod(x, gain, mod, sh, sc):
    ms = jnp.mean(x * x, axis=-1, keepdims=True)
    y = x * lax.rsqrt(ms + RMS_EPS) * gain
    return y * (1.0 + mod[sc:sc + 1]) + mod[sh:sh + 1]


def _in_proj_kernel(x_ref, mod_ref, gain_ref, *refs, n_out, sh, sc):
    w_refs, o_refs = refs[:n_out], refs[n_out:]
    h = _norm_mod(x_ref[...], gain_ref[...], mod_ref[0], sh, sc).astype(BF16)
    for w_ref, o_ref in zip(w_refs, o_refs):
        o_ref[...] = _dot(h, w_ref[...]).astype(o_ref.dtype)


def _in_proj(x, mod, gain, weights, *, seq, sh, sc, tm, out_dtype):
    t, d = x.shape
    per_b = seq // tm
    n_out = len(weights)
    in_specs = [
        pl.BlockSpec((tm, d), lambda i: (i, 0)),
        pl.BlockSpec((1, 6, d), lambda i: (i // per_b, 0, 0)),
        _const_spec((1, d)),
    ] + [_const_spec(w.shape) for w in weights]
    out_specs = [pl.BlockSpec((tm, w.shape[1]), lambda i: (i, 0)) for w in weights]
    out_shape = [jax.ShapeDtypeStruct((t, w.shape[1]), out_dtype) for w in weights]
    return pl.pallas_call(
        functools.partial(_in_proj_kernel, n_out=n_out, sh=sh, sc=sc),
        grid=(t // tm,),
        in_specs=in_specs,
        out_specs=out_specs,
        out_shape=out_shape,
        compiler_params=_params("parallel"),
        name="in_proj",
    )(x, mod, gain.reshape(1, d), *weights)


def _ffn_kernel(x_ref, mod_ref, gain_ref, wg_ref, wu_ref, wd_ref, fin_ref, *refs, n_in, final):
    y_refs, w_refs, o_ref = refs[:n_in], refs[n_in:2 * n_in], refs[2 * n_in]
    x = x_ref[...]
    mod = mod_ref[0]
    if n_in:
        mix = _dot(y_refs[0][...], w_refs[0][...])
        for y_ref, w_ref in zip(y_refs[1:], w_refs[1:]):
            mix += _dot(y_ref[...], w_ref[...])
        x = x + mod[2:3] * mix
    h = _norm_mod(x, gain_ref[...], mod, 3, 4).astype(BF16)
    gate = _dot(h, wg_ref[...])
    up = _dot(h, wu_ref[...])
    act = (gate * jax.nn.sigmoid(gate) * up).astype(BF16)
    y = x + mod[5:6] * _dot(act, wd_ref[...])
    if final:
        ms = jnp.mean(y * y, axis=-1, keepdims=True)
        y = y * lax.rsqrt(ms + RMS_EPS) * fin_ref[...]
    o_ref[...] = y


def _ffn(x, mod, gain, wg, wu, wd, fin, ys, w_outs, *, seq, tm, final):
    t, d = x.shape
    per_b = seq // tm
    return pl.pallas_call(
        functools.partial(_ffn_kernel, n_in=len(ys), final=final),
        grid=(t // tm,),
        in_specs=[
            pl.BlockSpec((tm, d), lambda i: (i, 0)),
            pl.BlockSpec((1, 6, d), lambda i: (i // per_b, 0, 0)),
            _const_spec((1, d)),
            _const_spec(wg.shape),
            _const_spec(wu.shape),
            _const_spec(wd.shape),
            _const_spec((1, d)),
        ] + [pl.BlockSpec((tm, y.shape[1]), lambda i: (i, 0)) for y in ys]
          + [_const_spec(w.shape) for w in w_outs],
        out_specs=pl.BlockSpec((tm, d), lambda i: (i, 0)),
        out_shape=jax.ShapeDtypeStruct((t, d), F32),
        compiler_params=_params("parallel"),
        name="ffn",
    )(x, mod, gain.reshape(1, d), wg, wu, wd, fin.reshape(1, d), *ys, *w_outs)


def _mla_prep_kernel(p_ref, pos_ref, invf_ref, qn_ref, kvn_ref, wqt_ref, wk_ref, wvt_ref,
                     qt_ref, k_ref, vt_ref, *, scale, tk):
    p = p_ref[...].astype(F32)
    tm = p.shape[0]
    half = MLA_ROPE // 2
    ang = invf_ref[...] * pos_ref[0].astype(F32)
    cos_t, sin_t = jnp.cos(ang), jnp.sin(ang)

    c_q = p[:, :Q_LORA]
    ms = jnp.mean(c_q * c_q, axis=-1, keepdims=True)
    cqn = (c_q * lax.rsqrt(ms + RMS_EPS) * qn_ref[...]).astype(BF16)
    qt = _dot_nt(wqt_ref[...], cqn) * scale
    pieces = []
    for h in range(MLA_HEADS):
        base = h * MLA_HEAD_PAD
        x1 = qt[base + MLA_NOPE:base + MLA_NOPE + half]
        x2 = qt[base + MLA_NOPE + half:base + MLA_NOPE + MLA_ROPE]
        pieces += [qt[base:base + MLA_NOPE], x1 * cos_t - x2 * sin_t, x1 * sin_t + x2 * cos_t,
                   qt[base + MLA_NOPE + MLA_ROPE:base + MLA_HEAD_PAD]]
    qt_ref[0] = jnp.concatenate(pieces, axis=0).astype(qt_ref.dtype)

    ckv = p[:, Q_LORA:Q_LORA + KV_LORA]
    ss = jnp.mean(ckv * ckv, axis=-1, keepdims=True)
    ckvn = ckv * lax.rsqrt(ss + RMS_EPS) * kvn_ref[...]
    kr = p[:, Q_LORA + KV_LORA:]
    rest = kr.shape[1] - MLA_ROPE
    cos_k = jnp.concatenate([cos_t, cos_t, jnp.ones((rest, tm), F32)], axis=0).T
    sin_k = jnp.concatenate([sin_t, sin_t, jnp.zeros((rest, tm), F32)], axis=0).T
    lane = lax.broadcasted_iota(jnp.int32, kr.shape, 1)
    partner = jnp.where(lane < half, -pltpu.roll(kr, kr.shape[1] - half, 1), pltpu.roll(kr, half, 1))
    lhs = jnp.concatenate([ckvn, kr * cos_k + partner * sin_k], axis=1).astype(BF16)
    k_ref[...] = _dot(lhs, wk_ref[...]).astype(k_ref.dtype)
    vt = _dot_nt(wvt_ref[...], lhs).astype(vt_ref.dtype)
    for jj in range(tm // tk):
        vt_ref[0, jj] = vt[:, jj * tk:(jj + 1) * tk]


def _mla_prep(p_mla, pos, invf, q_norm, kv_norm, wqt, wk, wvt, *, batch, seq, tm, tk):
    t = p_mla.shape[0]
    n_q = MLA_HEADS * MLA_HEAD_PAD
    n_v = MLA_HEADS * MLA_V
    per_b = seq // tm
    scale = (MLA_NOPE + MLA_ROPE) ** -0.5 * LOG2_E
    return pl.pallas_call(
        functools.partial(_mla_prep_kernel, scale=scale, tk=tk),
        grid=(t // tm,),
        in_specs=[
            pl.BlockSpec((tm, MLA_COLS_PAD), lambda i: (i, 0)),
            pl.BlockSpec((1, 1, tm), lambda i: (i, 0, 0)),
            _const_spec(invf.shape),
            _const_spec((1, Q_LORA)),
            _const_spec((1, KV_LORA)),
            _const_spec(wqt.shape),
            _const_spec(wk.shape),
            _const_spec(wvt.shape),
        ],
        out_specs=[
            pl.BlockSpec((1, n_q, tm), lambda i: (i // per_b, 0, i % per_b)),
            pl.BlockSpec((tm, n_q), lambda i: (i, 0)),
            pl.BlockSpec((1, tm // tk, n_v, tk), lambda i: (i // per_b, i % per_b, 0, 0)),
        ],
        out_shape=[
            jax.ShapeDtypeStruct((batch, n_q, seq), BF16),
            jax.ShapeDtypeStruct((t, n_q), BF16),
            jax.ShapeDtypeStruct((batch, seq // tk, n_v, tk), BF16),
        ],
        compiler_params=_params("parallel"),
        name="mla_prep",
    )(p_mla, pos.reshape(t // tm, 1, tm), invf, q_norm, kv_norm, wqt, wk, wvt)


def _attn_kernel(qt_ref, k_ref, vt_ref, o_ref, s_ref, p_ref, *, tq, nh):
    i = pl.program_id(2)
    qts = [qt_ref[0, h * MLA_HEAD_PAD:(h + 1) * MLA_HEAD_PAD, :] for h in range(nh)]
    causal = (lax.broadcasted_iota(jnp.int32, (tq, tq), 0)
              <= lax.broadcasted_iota(jnp.int32, (tq, tq), 1))

    def put_scores(j, slot):
        start = pl.multiple_of(j * tq, tq)
        for h in range(nh):
            s_ref[slot, h] = _dot(
                k_ref[pl.ds(start, tq), h * MLA_HEAD_PAD:(h + 1) * MLA_HEAD_PAD], qts[h])

    def value_dots(j, probs):
        return [_dot(vt_ref[0, j, h * MLA_V:(h + 1) * MLA_V, :], probs[h]) for h in range(nh)]

    def step(carry, slot, pv_prev, masked):
        new, probs = [], []
        for h in range(nh):
            m, l, acc = carry[h]
            s = s_ref[slot, h]
            if masked:
                s = jnp.where(causal, s, -jnp.inf)
            m_new = jnp.maximum(m, jnp.max(s, axis=0, keepdims=True))
            pr = jnp.exp2(s - m_new)
            alpha = jnp.exp2(m - m_new)
            probs.append(pr.astype(BF16))
            new.append((m_new, alpha * l + jnp.sum(pr, axis=0, keepdims=True),
                        alpha * (acc + pv_prev[h])))
        return tuple(new), probs

    def parked(slot):
        return [p_ref[slot, h] for h in range(nh)]

    def pair(t, carry):
        j = 2 * t
        put_scores(j + 1, 1)
        pv = value_dots(jnp.where(t == 0, i, j - 1), parked(1))
        carry, probs = step(carry, 0, pv, False)
        for h in range(nh):
            p_ref[0, h] = probs[h]
        put_scores(j + 2, 0)
        pv = value_dots(j, parked(0))
        carry, probs = step(carry, 1, pv, False)
        for h in range(nh):
            p_ref[1, h] = probs[h]
        return carry

    def emit(carry, pv_last):
        outs = [(acc + pv_last[h]) * (1.0 / l) for h, (_, l, acc) in enumerate(carry)]
        o_ref[...] = jnp.concatenate(outs, axis=0).T.astype(o_ref.dtype)

    put_scores(i, 1)
    put_scores(0, 0)
    zeros = [jnp.zeros((MLA_V, tq), F32)] * nh
    init = tuple((jnp.full((1, tq), -jnp.inf, F32), jnp.zeros((1, tq), F32), zeros[h])
                 for h in range(nh))
    carry, probs = step(init, 1, zeros, True)
    for h in range(nh):
        p_ref[1, h] = probs[h]
    carry = lax.fori_loop(0, i // 2, pair, carry)
    n_done = 2 * (i // 2)
    last = jnp.where(n_done == 0, i, n_done - 1)

    @pl.when(i % 2 == 0)
    def _():
        emit(carry, value_dots(last, parked(1)))

    @pl.when(i % 2 == 1)
    def _():
        final, probs = step(carry, 0, value_dots(last, parked(1)), False)
        emit(final, value_dots(i - 1, probs))


def _attention(qt, k, vt, *, batch, seq, tq, nh):
    t = k.shape[0]
    nq = seq // tq
    return pl.pallas_call(
        functools.partial(_attn_kernel, tq=tq, nh=nh),
        grid=(batch, MLA_HEADS // nh, nq),
        in_specs=[
            pl.BlockSpec((1, nh * MLA_HEAD_PAD, tq), lambda b, h, i: (b, h, i)),
            pl.BlockSpec((seq, nh * MLA_HEAD_PAD), lambda b, h, i: (b, h)),
            pl.BlockSpec((1, seq // tq, nh * MLA_V, tq), lambda b, h, i: (b, 0, h, 0)),
        ],
        out_specs=pl.BlockSpec((tq, nh * MLA_V), lambda b, h, i: (b * nq + i, h)),
        out_shape=jax.ShapeDtypeStruct((t, MLA_HEADS * MLA_V), BF16),
        scratch_shapes=[pltpu.VMEM((2, nh, tq, tq), F32), pltpu.VMEM((2, nh, tq, tq), BF16)],
        compiler_params=_params("parallel", "parallel", "arbitrary"),
        name="mla_attention",
    )(qt, k, vt)


def _stack_heads(x, lane_head):
    return jnp.concatenate(
        [jnp.where(lane_head == h, x, 0.0) for h in range(RWKV_GROUP)], axis=0)


def _rwkv_kernel(p_ref, mu_ref, vec_ref, w2_ref, a2_ref, g2_ref, ind_ref, tri_ref,
                 o_ref, state_ref, prev_ref, *, n_chunks):
    c = RWKV_CHUNK
    gw = RWKV_GROUP * RWKV_HEAD
    gc = RWKV_GROUP * c
    n_groups = RWKV_HEADS // RWKV_GROUP
    tb = n_chunks * c

    @pl.when(pl.program_id(1) == 0)
    def _():
        state_ref[...] = jnp.zeros_like(state_ref)
        prev_ref[...] = jnp.zeros_like(prev_ref)

    p = p_ref[...].astype(F32)
    rows = lax.broadcasted_iota(jnp.int32, p.shape, 0)
    shifted = jnp.where(rows == 0, prev_ref[...], pltpu.roll(p, 1, 0))
    prev_ref[...] = p[tb - 1:tb, :]
    p = p + (shifted - p) * mu_ref[...]

    d = RWKV_DIM
    r, k, v = p[:, :d], p[:, d:2 * d], p[:, 2 * d:3 * d]
    lo = p[:, 3 * d:3 * d + DECAY_LORA + AAA_LORA]
    g_lo = p[:, 3 * d + DECAY_LORA + AAA_LORA:]
    w0, a0, k_k, k_a = vec_ref[0:1], vec_ref[1:2], vec_ref[2:3], vec_ref[3:4]
    r_k, ln_w, ln_b = vec_ref[4:5], vec_ref[5:6], vec_ref[6:7]

    x_w = w0 + _dot(jnp.tanh(lo).astype(BF16), w2_ref[...])
    lw = -EXP_NEG_HALF * jax.nn.sigmoid(x_w)
    a = jax.nn.sigmoid(a0 + _dot(lo.astype(BF16), a2_ref[...]))
    g = _dot(jax.nn.sigmoid(g_lo).astype(BF16), g2_ref[...])

    ind = ind_ref[...]
    kk = k * k_k
    kk = kk * lax.rsqrt(jnp.maximum(_head_sum(kk * kk, ind), 1e-24))
    k2 = k * (1.0 + (a - 1.0) * k_a)
    alpha = kk * a

    b = _dot_split(tri_ref[...], lw)
    b_prev = b - lw
    bcast = lambda row: jnp.broadcast_to(row, (c, d))
    b_mid = jnp.concatenate([bcast(b[ci * c + c // 2 - 1:ci * c + c // 2]) for ci in range(n_chunks)], axis=0)
    b_end = jnp.concatenate([bcast(b[ci * c + c - 1:ci * c + c]) for ci in range(n_chunks)], axis=0)
    kap_t = kk * jnp.exp(b_prev - b_mid)
    r_t = r * jnp.exp(b - b_mid)
    inv_t = jnp.exp(b_mid - b)
    alp_t, k_t = alpha * inv_t, k2 * inv_t
    kap_h = kk * jnp.exp(b_prev)
    r_h = r * jnp.exp(b)
    tail = jnp.exp(b_end - b)
    alp_c, k_c = alpha * tail, k2 * tail
    gam_end = jnp.exp(b_end)

    eye = (lax.broadcasted_iota(jnp.int32, (gc, gc), 0)
           == lax.broadcasted_iota(jnp.int32, (gc, gc), 1)).astype(F32)
    lane_head = lax.broadcasted_iota(jnp.int32, (c, gw), 1) // RWKV_HEAD

    def stacked(x, ci, gi):
        return _stack_heads(x[ci * c:(ci + 1) * c, gi * gw:(gi + 1) * gw], lane_head)

    pairs = [(ci, gi) for ci in range(n_chunks) for gi in range(n_groups)]
    half = gc // 2
    lane_pair = lax.broadcasted_iota(jnp.int32, (c, half), 1) // RWKV_HEAD
    tri_r = lax.broadcasted_iota(jnp.int32, (half, half), 0) % c
    tri_c = lax.broadcasted_iota(jnp.int32, (half, half), 1) % c
    zero_blk = jnp.zeros((half, half), F32)

    def pair_stack(x, ci, pi):
        xs = x[ci * c:(ci + 1) * c, pi * half:(pi + 1) * half]
        return jnp.concatenate([jnp.where(lane_pair == k, xs, 0.0) for k in range(2)], axis=0)

    def block_diag(a, b_):
        return jnp.concatenate([jnp.concatenate([a, zero_blk], axis=1),
                                jnp.concatenate([zero_blk, b_], axis=1)], axis=0)

    a_ka, a_kk, a_ra, a_rk, vs, n_pow, inv = {}, {}, {}, {}, {}, {}, {}
    for pr in pairs:
        ci, gi = pr
        blocks = []
        for pi in (2 * gi, 2 * gi + 1):
            ps = lambda x: pair_stack(x, ci, pi)
            kr = jnp.concatenate([ps(kap_t), ps(r_t)], axis=0).astype(BF16)
            ak = jnp.concatenate([ps(alp_t), ps(k_t)], axis=0).astype(BF16)
            blocks.append(_dot_nt(kr, ak))
        quad = lambda r0, c0, keep: block_diag(
            *[jnp.where(keep, blk[r0:r0 + half, c0:c0 + half], 0.0) for blk in blocks])
        a_ka[pr] = quad(0, 0, tri_c < tri_r)
        a_kk[pr] = quad(0, half, tri_c < tri_r).astype(BF16)
        a_ra[pr] = quad(half, 0, tri_c <= tri_r).astype(BF16)
        a_rk[pr] = quad(half, half, tri_c <= tri_r).astype(BF16)
        vs[pr] = stacked(v, *pr).astype(BF16)
    for pr in pairs:
        n_pow[pr] = a_ka[pr]
        inv[pr] = eye - n_pow[pr]
    def later_rows(x, t0):
        if t0 == 0:
            return x
        return jnp.concatenate([x[h * c + t0:(h + 1) * c] for h in range(RWKV_GROUP)], axis=0)

    def restore_rows(x, t0):
        if t0 == 0:
            return x
        rows = c - t0
        pad = jnp.zeros((t0, x.shape[1]), x.dtype)
        return jnp.concatenate(
            [blk for h in range(RWKV_GROUP) for blk in (pad, x[h * rows:(h + 1) * rows])], axis=0)

    for j in range(1, int(np.log2(c))):
        t0 = (1 << j) // 8 * 8
        for pr in pairs:
            pw = n_pow[pr]
            n_pow[pr] = restore_rows(_dot(later_rows(pw, t0).astype(BF16), pw.astype(BF16)), t0)
        for pr in pairs:
            upd = _dot(later_rows(inv[pr], t0).astype(BF16), n_pow[pr].astype(BF16))
            inv[pr] = inv[pr] + restore_rows(upd, t0)
    akv = {pr: _dot(a_kk[pr], vs[pr]).astype(BF16) for pr in pairs}
    pre = {}
    for pr in pairs:
        st = lambda x: stacked(x, *pr)
        invb = inv[pr].astype(BF16)
        kt = _dot(invb, st(kap_h).astype(BF16)).astype(BF16)
        w_mat = _dot(invb, akv[pr])
        akc = jnp.concatenate([st(alp_c), st(k_c)], axis=0).astype(BF16)
        pre[pr] = (kt, w_mat, a_ra[pr], a_rk[pr], st(r_h).astype(BF16), akc)

    y_rows = []
    for ci in range(n_chunks):
        s0 = [state_ref[gi] for gi in range(n_groups)]
        s0b = [x.astype(BF16) for x in s0]
        ub = [(-(_dot_nt(pre[ci, gi][0], s0b[gi]) + pre[ci, gi][1])).astype(BF16)
              for gi in range(n_groups)]
        ys = []
        for gi in range(n_groups):
            kt, w_mat, a_ra, a_rk, rh, akc = pre[ci, gi]
            uv = jnp.concatenate([ub[gi], vs[ci, gi]], axis=0)
            sl = slice(gi * gw, (gi + 1) * gw)
            state_ref[gi] = s0[gi] * gam_end[ci * c:ci * c + 1, sl] + _dot_tn(uv, akc)
            y_st = _dot_nt(rh, s0b[gi]) + _dot(a_ra, ub[gi]) + _dot(a_rk, vs[ci, gi])
            y = y_st[0:c]
            for h in range(1, RWKV_GROUP):
                y = y + y_st[h * c:(h + 1) * c]
            ys.append(y)
        y_rows.append(jnp.concatenate(ys, axis=1))
    y = jnp.concatenate(y_rows, axis=0)

    inv_n = 1.0 / RWKV_HEAD
    mean = _head_sum(y, ind) * inv_n
    yc = y - mean
    var = _head_sum(yc * yc, ind) * inv_n
    y = yc * lax.rsqrt(var + RWKV_GN_EPS) * ln_w + ln_b
    bonus = _head_sum(r * k2 * r_k, ind) * v
    o_ref[...] = ((y + bonus) * g).astype(o_ref.dtype)


def _rwkv(p_rwkv, mu, vecs, w2, a2, g2, *, batch, seq, n_chunks):
    t = p_rwkv.shape[0]
    c = RWKV_CHUNK
    tb = n_chunks * c
    nb = seq // tb
    gw = RWKV_GROUP * RWKV_HEAD
    head = np.arange(RWKV_DIM) // RWKV_HEAD
    ind = jnp.asarray(head[:, None] == head[None, :], BF16)
    tt = np.arange(tb)
    tri = jnp.asarray((tt[None, :] <= tt[:, None]) & (tt[None, :] // c == tt[:, None] // c), BF16)
    return pl.pallas_call(
        functools.partial(_rwkv_kernel, n_chunks=n_chunks),
        grid=(batch, nb),
        in_specs=[
            pl.BlockSpec((tb, RWKV_COLS), lambda b, j: (b * nb + j, 0)),
            _const_spec((1, RWKV_COLS)),
            _const_spec(vecs.shape),
            _const_spec(w2.shape),
            _const_spec(a2.shape),
            _const_spec(g2.shape),
            _const_spec(ind.shape),
            _const_spec(tri.shape),
        ],
        out_specs=pl.BlockSpec((tb, RWKV_DIM), lambda b, j: (b * nb + j, 0)),
        out_shape=jax.ShapeDtypeStruct((t, RWKV_DIM), BF16),
        scratch_shapes=[
            pltpu.VMEM((RWKV_HEADS // RWKV_GROUP, gw, gw), F32),
            pltpu.VMEM((1, RWKV_COLS), F32),
        ],
        compiler_params=_params("parallel", "arbitrary"),
        name="rwkv7",
    )(p_rwkv, mu, vecs, w2, a2, g2, ind, tri)


def _hgrn_masks():
    c = HG_CHUNK
    t = np.arange(c)
    mask = np.zeros((HG_LEVELS + 1, c, c), np.float32)
    mask[0] = np.eye(c)
    for l in range(1, HG_LEVELS + 1):
        m = 1 << l
        is_right = (t % m) >= m // 2
        same = (t[:, None] // m) == (t[None, :] // m)
        mask[l] = same & is_right[:, None] & ~is_right[None, :]
    return mask


def _hgrn_chunk(proj, lb, gain, tri, mask_ref, state_ref, side_jobs):
    c = HG_CHUNK
    jobs = iter(side_jobs)
    run_job = lambda: next(jobs, lambda: None)()
    q, f, vals, g = proj
    n = q.shape[1]
    q = q * jax.nn.sigmoid(q)
    forget = lb + (1.0 - lb) * jax.nn.sigmoid(f)
    key = 1.0 - forget
    logf = jnp.log(forget)

    b = _dot_split(tri, logf)
    e_cum = jnp.exp(b)
    b_end = b[c - 1:c, :]
    q_hat = (q * e_cum).astype(BF16)
    k_hat = (key * jnp.exp(b_end - b)).astype(BF16)
    decay_end = e_cum[c - 1:c, :]
    qb, kb = q.astype(BF16), key.astype(BF16)
    run_job()

    t_idx = lax.broadcasted_iota(jnp.int32, q.shape, 0)
    f_prev = pltpu.roll(forget, 1, 0)
    f_next = pltpu.roll(forget, c - 1, 0)
    level_ops = []
    for l in range(HG_LEVELS):
        m = 2 << l
        is_right = (t_idx & (m // 2)) != 0
        if m == 2:
            decay = jnp.where(is_right, forget, 1.0)
        elif m == 4:
            pos = t_idx & 3
            decay = jnp.where(pos == 0, f_next,
                              jnp.where(pos == 1, 1.0, jnp.where(pos == 2, forget, forget * f_prev)))
        else:
            b3 = b.reshape(c // m, m, n)
            b_ref = jnp.broadcast_to(b3[:, m // 2 - 1:m // 2, :], (c // m, m, n)).reshape(c, n)
            decay = jnp.exp(-jnp.abs(b - b_ref))
        level_ops.append((jnp.where(is_right, q, key) * decay).astype(BF16))
        run_job()

    heads = [slice(h * HG_K, (h + 1) * HG_K) for h in range(HG_HEADS)]
    attns = []
    for sl in heads:
        attn = mask_ref[0] * _dot_nt(qb[:, sl], kb[:, sl])
        for l in range(HG_LEVELS):
            ml = level_ops[l][:, sl]
            attn = attn + mask_ref[l + 1] * _dot_nt(ml, ml)
        attns.append(attn.astype(BF16))
        run_job()
    vhs = [vals[:, sl].astype(BF16) for sl in heads]
    sts = [state_ref[h] for h in range(HG_HEADS)]
    os = [_dot_nt(q_hat[:, sl], sts[h].astype(BF16)) + _dot(attns[h], vhs[h])
          for h, sl in enumerate(heads)]
    for h, sl in enumerate(heads):
        state_ref[h] = sts[h] * decay_end[:, sl] + _dot_tn(vhs[h], k_hat[:, sl])
    outs = []
    for h, sl in enumerate(heads):
        o = os[h]
        ms = jnp.mean(o * o, axis=-1, keepdims=True)
        gt = g[:, sl]
        outs.append((o * lax.rsqrt(ms + RMS_EPS) * gain * (gt * jax.nn.sigmoid(gt))).astype(BF16))
    for job in jobs:
        job()
    return jnp.concatenate(outs, axis=1)


def _hgrn_layer_kernel(x_ref, xn_ref, mod_ref, ngain_ref, win_ref, lbl_ref, gain_ref, tri_ref,
                       mask_ref, wout_ref, o_ref, state_ref, proj_ref, *, layer):
    c = HG_CHUNK
    mod = mod_ref[0]
    per_proj = proj_ref.shape[3] // HG_PROJ_PIECE

    def projection_jobs(x, slot):
        hin = _norm_mod(x, ngain_ref[...], mod, 0, 1).astype(BF16)

        def piece(k, p):
            def job():
                lo = p * HG_PROJ_PIECE
                proj_ref[slot, k, :, lo:lo + HG_PROJ_PIECE] = _dot(hin, win_ref[k * per_proj + p])
            return job

        return [piece(k, p) for k in range(4) for p in range(per_proj)]

    @pl.when(pl.program_id(1) == 0)
    def _():
        state_ref[...] = jnp.zeros_like(state_ref)
        for job in projection_jobs(x_ref[0:c, :], 0):
            job()

    logits = lbl_ref[...]
    pe = jnp.exp(logits - jnp.max(logits, axis=0, keepdims=True))
    lb = jnp.sum(pe[1:layer + 1], axis=0, keepdims=True) / jnp.sum(pe, axis=0, keepdims=True)

    def recur(slot, side_jobs):
        proj = tuple(proj_ref[slot, k] for k in range(4))
        return _hgrn_chunk(proj, lb, gain_ref[...], tri_ref[...], mask_ref, state_ref, side_jobs)

    y0 = recur(0, projection_jobs(x_ref[c:2 * c, :], 1))
    y1 = recur(1, projection_jobs(xn_ref[...], 0))
    y = jnp.concatenate([y0, y1], axis=0)
    o_ref[...] = x_ref[...] + mod[2:3] * _dot(y, wout_ref[...])


def _hgrn_layer(x, mod, norm_gain, w_in, lb_logits, gain, w_out, *, batch, seq, layer):
    t, d = x.shape
    c = HG_CHUNK
    n = HG_HEADS * HG_K
    nb = seq // (2 * c)
    tri = jnp.asarray(np.tril(np.ones((c, c))), BF16)
    mask = jnp.asarray(_hgrn_masks(), F32)
    w_pieces = w_in.reshape(d, 4 * n // HG_PROJ_PIECE, HG_PROJ_PIECE).transpose(1, 0, 2)
    next_chunk = lambda b, j: (b * 2 * nb + jnp.minimum(2 * j + 2, 2 * nb - 1), 0)
    return pl.pallas_call(
        functools.partial(_hgrn_layer_kernel, layer=layer),
        grid=(batch, nb),
        in_specs=[pl.BlockSpec((2 * c, d), lambda b, j: (b * nb + j, 0)),
                  pl.BlockSpec((c, d), next_chunk),
                  pl.BlockSpec((1, 6, d), lambda b, j: (b, 0, 0)),
                  _const_spec((1, d)), _const_spec(w_pieces.shape),
                  _const_spec(lb_logits.shape), _const_spec((1, HG_V)),
                  _const_spec(tri.shape), _const_spec(mask.shape), _const_spec(w_out.shape)],
        out_specs=pl.BlockSpec((2 * c, d), lambda b, j: (b * nb + j, 0)),
        out_shape=jax.ShapeDtypeStruct((t, d), F32),
        scratch_shapes=[pltpu.VMEM((HG_HEADS, HG_V, HG_K), F32),
                        pltpu.VMEM((2, 4, c, n), F32)],
        compiler_params=_params("parallel", "arbitrary"),
        name="hgrn2_layer",
    )(x, x, mod, norm_gain.reshape(1, d), w_pieces, lb_logits, gain, tri, mask, w_out)


def _pad_cols(w, n):
    return jnp.pad(w, ((0, 0), (0, n - w.shape[1])))


def _mla_weights(w_uq, w_ukv):
    hq = MLA_NOPE + MLA_ROPE
    wq = w_uq.reshape(Q_LORA, MLA_HEADS, hq)
    wq = jnp.pad(wq, ((0, 0), (0, 0), (0, MLA_HEAD_PAD - hq))).reshape(Q_LORA, -1)
    wkv = w_ukv.reshape(KV_LORA, MLA_HEADS, MLA_NOPE + MLA_V)
    wk = jnp.pad(wkv[:, :, :MLA_NOPE], ((0, 0), (0, 0), (0, MLA_HEAD_PAD - MLA_NOPE)))
    wk = wk.reshape(KV_LORA, -1)
    wv = wkv[:, :, MLA_NOPE:].reshape(KV_LORA, -1)
    n_extra = MLA_COLS_PAD - Q_LORA - KV_LORA
    place = np.zeros((n_extra, MLA_HEADS, MLA_HEAD_PAD), np.float32)
    for r in range(MLA_ROPE):
        place[r, :, MLA_NOPE + r] = 1.0
    wk_full = jnp.concatenate([wk, jnp.asarray(place.reshape(n_extra, -1))], axis=0)
    wv_full = jnp.concatenate([wv, jnp.zeros((n_extra, wv.shape[1]), F32)], axis=0)
    return wq.T.astype(BF16), wk_full.astype(BF16), wv_full.T.astype(BF16)


def _rope_inv_freq():
    inv = 1.0 / (ROPE_BASE ** (jnp.arange(0, MLA_ROPE, 2, dtype=F32) / MLA_ROPE))
    return inv.reshape(MLA_ROPE // 2, 1)


def kernel(x, c, positions, ada_w, ada_b, norm_mix, norm_ffn, w_in_even, mla_q_norm, mla_w_uq, mla_kv_norm, mla_w_ukv, rwkv_mu, rwkv_w0, rwkv_w2, rwkv_a0, rwkv_a2, rwkv_g2, rwkv_k_k, rwkv_k_a, rwkv_r_k, rwkv_ln_w, rwkv_ln_b, w_out_even, w_in_odd, hg_lb_logits, hg_out_norm, w_out_odd, ffn_w_gate, ffn_w_up, ffn_w_down, final_norm):
    batch, seq, d = x.shape
    depth = ada_w.shape[0]
    t = batch * seq
    tm = min(512, seq)
    xt = x.reshape(t, d)

    mod_all = _ada_mod(c, ada_w, ada_b).reshape(depth, batch, 6, d)

    for l in range(depth):
        mod = mod_all[l]
        j = l // 2
        if l % 2 == 0:
            w_in = w_in_even[j]
            w_mla = _pad_cols(w_in[:, :MLA_COLS], MLA_COLS_PAD).astype(BF16)
            w_rwkv = w_in[:, MLA_COLS:].astype(BF16)
            p_mla, p_rwkv = _in_proj(xt, mod, norm_mix[l], [w_mla, w_rwkv],
                                     seq=seq, sh=0, sc=1, tm=tm, out_dtype=BF16)
            wqt, wk, wvt = _mla_weights(mla_w_uq[j], mla_w_ukv[j])
            qt, kh, vt = _mla_prep(p_mla, positions, _rope_inv_freq(),
                                   mla_q_norm[j].reshape(1, -1), mla_kv_norm[j].reshape(1, -1),
                                   wqt, wk, wvt, batch=batch, seq=seq, tm=tm, tk=ATTN_TILE)
            y_a = _attention(qt, kh, vt, batch=batch, seq=seq, tq=ATTN_TILE, nh=ATTN_HEADS)

            zeros_lo = jnp.zeros((DECAY_LORA, RWKV_DIM), F32)
            w2 = jnp.concatenate([rwkv_w2[j], zeros_lo], axis=0).astype(BF16)
            a2 = jnp.concatenate([zeros_lo, rwkv_a2[j]], axis=0).astype(BF16)
            vecs = jnp.stack([rwkv_w0[j], rwkv_a0[j], rwkv_k_k[j], rwkv_k_a[j],
                              rwkv_r_k[j].reshape(-1), rwkv_ln_w[j], rwkv_ln_b[j],
                              jnp.zeros((RWKV_DIM,), F32)])
            y_b = _rwkv(p_rwkv, rwkv_mu[j].reshape(1, -1), vecs, w2, a2,
                        rwkv_g2[j].astype(BF16), batch=batch, seq=seq,
                        n_chunks=min(RWKV_BLOCK_CHUNKS, seq // RWKV_CHUNK))
            w_out = w_out_even[j].astype(BF16)
            n_a = MLA_HEADS * MLA_V
            ys, w_outs = [y_a, y_b], [w_out[:n_a], w_out[n_a:]]
        else:
            xt = _hgrn_layer(xt, mod, norm_mix[l], w_in_odd[j].astype(BF16), hg_lb_logits,
                             hg_out_norm[j].reshape(1, -1), w_out_odd[j].astype(BF16),
                             batch=batch, seq=seq, layer=l)
            ys, w_outs = [], []
        xt = _ffn(xt, mod, norm_ffn[l], ffn_w_gate[l].astype(BF16), ffn_w_up[l].astype(BF16),
                  ffn_w_down[l].astype(BF16), final_norm, ys, w_outs, seq=seq, tm=tm,
                  final=(l == depth - 1))
    return xt.reshape(batch, seq, d)
```

```python
import functools

import numpy as np
import jax
import jax.numpy as jnp
from jax import lax
from jax.experimental import pallas as pl
from jax.experimental.pallas import tpu as pltpu

F32 = jnp.float32
BF16 = jnp.bfloat16

RMS_EPS = 1e-6
LOG2_E = 1.4426950408889634
EXP_NEG_HALF = 0.6065306597126334
LANES = 128
VMEM_LIMIT = 56 * 1024 * 1024

MLA_HEADS = 8
MLA_NOPE = 64
MLA_ROPE = 32
MLA_V = 64
Q_LORA = 384
KV_LORA = 256
ROPE_BASE = 10000.0
MLA_COLS = Q_LORA + KV_LORA + MLA_ROPE
MLA_COLS_PAD = 768
MLA_HEAD_PAD = 128
ATTN_TILE = 256
ATTN_HEADS = 8

RWKV_HEAD = 64
RWKV_DIM = 512
RWKV_HEADS = 8
DECAY_LORA = 64
AAA_LORA = 64
GATE_LORA = 128
RWKV_GN_EPS = RWKV_HEAD * 1e-5
RWKV_COLS = 3 * RWKV_DIM + DECAY_LORA + AAA_LORA + GATE_LORA
RWKV_CHUNK = 64
RWKV_BLOCK_CHUNKS = 4
RWKV_GROUP = 4

HG_K = 128
HG_HEADS = 8
HG_V = 128
HG_CHUNK = 128
HG_LEVELS = 7
HG_PROJ_PIECE = 256


def _params(*sem):
    return pltpu.CompilerParams(dimension_semantics=sem, vmem_limit_bytes=VMEM_LIMIT)


def _const_spec(shape):
    nd = len(shape)
    return pl.BlockSpec(shape, lambda *_: (0,) * nd, pipeline_mode=pl.Buffered(1))


def _dot(a, b):
    return jnp.dot(a, b, preferred_element_type=F32)


def _dot_nt(a, b):
    return lax.dot_general(a, b, (((1,), (1,)), ((), ())), preferred_element_type=F32)


def _dot_tn(a, b):
    return lax.dot_general(a, b, (((0,), (0,)), ((), ())), preferred_element_type=F32)


def _dot_split(sel, x):
    hi = x.astype(BF16)
    lo = (x - hi.astype(F32)).astype(BF16)
    return _dot(sel, hi) + _dot(sel, lo)


def _head_sum(x, ind):
    return _dot(x.astype(BF16), ind)


def _ada_kernel(c_ref, w_ref, b_ref, o_ref):
    c = c_ref[...]
    cond = c * jax.nn.sigmoid(c)
    o_ref[0] = _dot(cond.astype(BF16), w_ref[0].astype(BF16)) + b_ref[0]


def _ada_mod(c, ada_w, ada_b):
    depth, d, n = ada_w.shape
    b = c.shape[0]
    tn = n // 4
    return pl.pallas_call(
        _ada_kernel,
        grid=(depth, n // tn),
        in_specs=[
            pl.BlockSpec((b, d), lambda l, j: (0, 0)),
            pl.BlockSpec((1, d, tn), lambda l, j: (l, 0, j)),
            pl.BlockSpec((1, 1, tn), lambda l, j: (l, 0, j)),
        ],
        out_specs=pl.BlockSpec((1, b, tn), lambda l, j: (l, 0, j)),
        out_shape=jax.ShapeDtypeStruct((depth, b, n), F32),
        compiler_params=_params("arbitrary", "arbitrary"),
        name="ada_mod",
    )(c, ada_w, ada_b.reshape(depth, 1, n))


def _norm_mod(x, gain, mod, sh, sc):
    ms = jnp.mean(x * x, axis=-1, keepdims=True)
    y = x * lax.rsqrt(ms + RMS_EPS) * gain
    return y * (1.0 + mod[sc:sc + 1]) + mod[sh:sh + 1]


def _in_proj_kernel(x_ref, mod_ref, gain_ref, *refs, n_out, sh, sc):
    w_refs, o_refs = refs[:n_out], refs[n_out:]
    h = _norm_mod(x_ref[...], gain_ref[...], mod_ref[0], sh, sc).astype(BF16)
    for w_ref, o_ref in zip(w_refs, o_refs):
        o_ref[...] = _dot(h, w_ref[...]).astype(o_ref.dtype)


def _in_proj(x, mod, gain, weights, *, seq, sh, sc, tm, out_dtype):
    t, d = x.shape
    per_b = seq // tm
    n_out = len(weights)
    in_specs = [
        pl.BlockSpec((tm, d), lambda i: (i, 0)),
        pl.BlockSpec((1, 6, d), lambda i: (i // per_b, 0, 0)),
        _const_spec((1, d)),
    ] + [_const_spec(w.shape) for w in weights]
    out_specs = [pl.BlockSpec((tm, w.shape[1]), lambda i: (i, 0)) for w in weights]
    out_shape = [jax.ShapeDtypeStruct((t, w.shape[1]), out_dtype) for w in weights]
    return pl.pallas_call(
        functools.partial(_in_proj_kernel, n_out=n_out, sh=sh, sc=sc),
        grid=(t // tm,),
        in_specs=in_specs,
        out_specs=out_specs,
        out_shape=out_shape,
        compiler_params=_params("parallel"),
        name="in_proj",
    )(x, mod, gain.reshape(1, d), *weights)


def _ffn_kernel(x_ref, mod_ref, gain_ref, wg_ref, wu_ref, wd_ref, fin_ref, *refs, n_in, final):
    y_refs, w_refs, o_ref = refs[:n_in], refs[n_in:2 * n_in], refs[2 * n_in]
    x = x_ref[...]
    mod = mod_ref[0]
    if n_in:
        mix = _dot(y_refs[0][...], w_refs[0][...])
        for y_ref, w_ref in zip(y_refs[1:], w_refs[1:]):
            mix += _dot(y_ref[...], w_ref[...])
        x = x + mod[2:3] * mix
    h = _norm_mod(x, gain_ref[...], mod, 3, 4).astype(BF16)
    gate = _dot(h, wg_ref[...])
    up = _dot(h, wu_ref[...])
    act = (gate * jax.nn.sigmoid(gate) * up).astype(BF16)
    y = x + mod[5:6] * _dot(act, wd_ref[...])
    if final:
        ms = jnp.mean(y * y, axis=-1, keepdims=True)
        y = y * lax.rsqrt(ms + RMS_EPS) * fin_ref[...]
    o_ref[...] = y


def _ffn(x, mod, gain, wg, wu, wd, fin, ys, w_outs, *, seq, tm, final):
    t, d = x.shape
    per_b = seq // tm
    return pl.pallas_call(
        functools.partial(_ffn_kernel, n_in=len(ys), final=final),
        grid=(t // tm,),
        in_specs=[
            pl.BlockSpec((tm, d), lambda i: (i, 0)),
            pl.BlockSpec((1, 6, d), lambda i: (i // per_b, 0, 0)),
            _const_spec((1, d)),
            _const_spec(wg.shape),
            _const_spec(wu.shape),
            _const_spec(wd.shape),
            _const_spec((1, d)),
        ] + [pl.BlockSpec((tm, y.shape[1]), lambda i: (i, 0)) for y in ys]
          + [_const_spec(w.shape) for w in w_outs],
        out_specs=pl.BlockSpec((tm, d), lambda i: (i, 0)),
        out_shape=jax.ShapeDtypeStruct((t, d), F32),
        compiler_params=_params("parallel"),
        name="ffn",
    )(x, mod, gain.reshape(1, d), wg, wu, wd, fin.reshape(1, d), *ys, *w_outs)


def _mla_prep_kernel(p_ref, pos_ref, invf_ref, qn_ref, kvn_ref, wqt_ref, wk_ref, wvt_ref,
                     qt_ref, k_ref, vt_ref, *, scale, tk):
    p = p_ref[...].astype(F32)
    tm = p.shape[0]
    half = MLA_ROPE // 2
    ang = invf_ref[...] * pos_ref[0].astype(F32)
    cos_t, sin_t = jnp.cos(ang), jnp.sin(ang)

    c_q = p[:, :Q_LORA]
    ms = jnp.mean(c_q * c_q, axis=-1, keepdims=True)
    cqn = (c_q * lax.rsqrt(ms + RMS_EPS) * qn_ref[...]).astype(BF16)
    qt = _dot_nt(wqt_ref[...], cqn) * scale
    pieces = []
    for h in range(MLA_HEADS):
        base = h * MLA_HEAD_PAD
        x1 = qt[base + MLA_NOPE:base + MLA_NOPE + half]
        x2 = qt[base + MLA_NOPE + half:base + MLA_NOPE + MLA_ROPE]
        pieces += [qt[base:base + MLA_NOPE], x1 * cos_t - x2 * sin_t, x1 * sin_t + x2 * cos_t,
                   qt[base + MLA_NOPE + MLA_ROPE:base + MLA_HEAD_PAD]]
    qt_ref[0] = jnp.concatenate(pieces, axis=0).astype(qt_ref.dtype)

    ckv = p[:, Q_LORA:Q_LORA + KV_LORA]
    ss = jnp.mean(ckv * ckv, axis=-1, keepdims=True)
    ckvn = ckv * lax.rsqrt(ss + RMS_EPS) * kvn_ref[...]
    kr = p[:, Q_LORA + KV_LORA:]
    rest = kr.shape[1] - MLA_ROPE
    cos_k = jnp.concatenate([cos_t, cos_t, jnp.ones((rest, tm), F32)], axis=0).T
    sin_k = jnp.concatenate([sin_t, sin_t, jnp.zeros((rest, tm), F32)], axis=0).T
    lane = lax.broadcasted_iota(jnp.int32, kr.shape, 1)
    partner = jnp.where(lane < half, -pltpu.roll(kr, kr.shape[1] - half, 1), pltpu.roll(kr, half, 1))
    lhs = jnp.concatenate([ckvn, kr * cos_k + partner * sin_k], axis=1).astype(BF16)
    k_ref[...] = _dot(lhs, wk_ref[...]).astype(k_ref.dtype)
    vt = _dot_nt(wvt_ref[...], lhs).astype(vt_ref.dtype)
    for jj in range(tm // tk):
        vt_ref[0, jj] = vt[:, jj * tk:(jj + 1) * tk]


def _mla_prep(p_mla, pos, invf, q_norm, kv_norm, wqt, wk, wvt, *, batch, seq, tm, tk):
    t = p_mla.shape[0]
    n_q = MLA_HEADS * MLA_HEAD_PAD
    n_v = MLA_HEADS * MLA_V
    per_b = seq // tm
    scale = (MLA_NOPE + MLA_ROPE) ** -0.5 * LOG2_E
    return pl.pallas_call(
        functools.partial(_mla_prep_kernel, scale=scale, tk=tk),
        grid=(t // tm,),
        in_specs=[
            pl.BlockSpec((tm, MLA_COLS_PAD), lambda i: (i, 0)),
            pl.BlockSpec((1, 1, tm), lambda i: (i, 0, 0)),
            _const_spec(invf.shape),
            _const_spec((1, Q_LORA)),
            _const_spec((1, KV_LORA)),
            _const_spec(wqt.shape),
            _const_spec(wk.shape),
            _const_spec(wvt.shape),
        ],
        out_specs=[
            pl.BlockSpec((1, n_q, tm), lambda i: (i // per_b, 0, i % per_b)),
            pl.BlockSpec((tm, n_q), lambda i: (i, 0)),
            pl.BlockSpec((1, tm // tk, n_v, tk), lambda i: (i // per_b, i % per_b, 0, 0)),
        ],
        out_shape=[
            jax.ShapeDtypeStruct((batch, n_q, seq), BF16),
            jax.ShapeDtypeStruct((t, n_q), BF16),
            jax.ShapeDtypeStruct((batch, seq // tk, n_v, tk), BF16),
        ],
        compiler_params=_params("parallel"),
        name="mla_prep",
    )(p_mla, pos.reshape(t // tm, 1, tm), invf, q_norm, kv_norm, wqt, wk, wvt)


def _attn_kernel(qt_ref, k_ref, vt_ref, o_ref, s_ref, p_ref, *, tq, nh):
    i = pl.program_id(2)
    qts = [qt_ref[0, h * MLA_HEAD_PAD:(h + 1) * MLA_HEAD_PAD, :] for h in range(nh)]
    causal = (lax.broadcasted_iota(jnp.int32, (tq, tq), 0)
              <= lax.broadcasted_iota(jnp.int32, (tq, tq), 1))

    def put_scores(j, slot):
        start = pl.multiple_of(j * tq, tq)
        for h in range(nh):
            s_ref[slot, h] = _dot(
                k_ref[pl.ds(start, tq), h * MLA_HEAD_PAD:(h + 1) * MLA_HEAD_PAD], qts[h])

    def value_dots(j, probs):
        return [_dot(vt_ref[0, j, h * MLA_V:(h + 1) * MLA_V, :], probs[h]) for h in range(nh)]

    def step(carry, slot, pv_prev, masked):
        new, probs = [], []
        for h in range(nh):
            m, l, acc = carry[h]
            s = s_ref[slot, h]
            if masked:
                s = jnp.where(causal, s, -jnp.inf)
            m_new = jnp.maximum(m, jnp.max(s, axis=0, keepdims=True))
            pr = jnp.exp2(s - m_new)
            alpha = jnp.exp2(m - m_new)
            probs.append(pr.astype(BF16))
            new.append((m_new, alpha * l + jnp.sum(pr, axis=0, keepdims=True),
                        alpha * (acc + pv_prev[h])))
        return tuple(new), probs

    def parked(slot):
        return [p_ref[slot, h] for h in range(nh)]

    def pair(t, carry):
        j = 2 * t
        put_scores(j + 1, 1)
        pv = value_dots(jnp.where(t == 0, i, j - 1), parked(1))
        carry, probs = step(carry, 0, pv, False)
        for h in range(nh):
            p_ref[0, h] = probs[h]
        put_scores(j + 2, 0)
        pv = value_dots(j, parked(0))
        carry, probs = step(carry, 1, pv, False)
        for h in range(nh):
            p_ref[1, h] = probs[h]
        return carry

    def emit(carry, pv_last):
        outs = [(acc + pv_last[h]) * (1.0 / l) for h, (_, l, acc) in enumerate(carry)]
        o_ref[...] = jnp.concatenate(outs, axis=0).T.astype(o_ref.dtype)

    put_scores(i, 1)
    put_scores(0, 0)
    zeros = [jnp.zeros((MLA_V, tq), F32)] * nh
    init = tuple((jnp.full((1, tq), -jnp.inf, F32), jnp.zeros((1, tq), F32), zeros[h])
                 for h in range(nh))
    carry, probs = step(init, 1, zeros, True)
    for h in range(nh):
        p_ref[1, h] = probs[h]
    carry = lax.fori_loop(0, i // 2, pair, carry)
    n_done = 2 * (i // 2)
    last = jnp.where(n_done == 0, i, n_done - 1)

    @pl.when(i % 2 == 0)
    def _():
        emit(carry, value_dots(last, parked(1)))

    @pl.when(i % 2 == 1)
    def _():
        final, probs = step(carry, 0, value_dots(last, parked(1)), False)
        emit(final, value_dots(i - 1, probs))


def _attention(qt, k, vt, *, batch, seq, tq, nh):
    t = k.shape[0]
    nq = seq // tq
    return pl.pallas_call(
        functools.partial(_attn_kernel, tq=tq, nh=nh),
        grid=(batch, MLA_HEADS // nh, nq),
        in_specs=[
            pl.BlockSpec((1, nh * MLA_HEAD_PAD, tq), lambda b, h, i: (b, h, i)),
            pl.BlockSpec((seq, nh * MLA_HEAD_PAD), lambda b, h, i: (b, h)),
            pl.BlockSpec((1, seq // tq, nh * MLA_V, tq), lambda b, h, i: (b, 0, h, 0)),
        ],
        out_specs=pl.BlockSpec((tq, nh * MLA_V), lambda b, h, i: (b * nq + i, h)),
        out_shape=jax.ShapeDtypeStruct((t, MLA_HEADS * MLA_V), BF16),
        scratch_shapes=[pltpu.VMEM((2, nh, tq, tq), F32), pltpu.VMEM((2, nh, tq, tq), BF16)],
        compiler_params=_params("parallel", "parallel", "arbitrary"),
        name="mla_attention",
    )(qt, k, vt)


def _stack_heads(x, lane_head):
    return jnp.concatenate(
        [jnp.where(lane_head == h, x, 0.0) for h in range(RWKV_GROUP)], axis=0)


def _rwkv_kernel(p_ref, mu_ref, vec_ref, w2_ref, a2_ref, g2_ref, ind_ref, tri_ref,
                 o_ref, state_ref, prev_ref, *, n_chunks):
    c = RWKV_CHUNK
    gw = RWKV_GROUP * RWKV_HEAD
    gc = RWKV_GROUP * c
    n_groups = RWKV_HEADS // RWKV_GROUP
    tb = n_chunks * c

    @pl.when(pl.program_id(1) == 0)
    def _():
        state_ref[...] = jnp.zeros_like(state_ref)
        prev_ref[...] = jnp.zeros_like(prev_ref)

    p = p_ref[...].astype(F32)
    rows = lax.broadcasted_iota(jnp.int32, p.shape, 0)
    shifted = jnp.where(rows == 0, prev_ref[...], pltpu.roll(p, 1, 0))
    prev_ref[...] = p[tb - 1:tb, :]
    p = p + (shifted - p) * mu_ref[...]

    d = RWKV_DIM
    r, k, v = p[:, :d], p[:, d:2 * d], p[:, 2 * d:3 * d]
    lo = p[:, 3 * d:3 * d + DECAY_LORA + AAA_LORA]
    g_lo = p[:, 3 * d + DECAY_LORA + AAA_LORA:]
    w0, a0, k_k, k_a = vec_ref[0:1], vec_ref[1:2], vec_ref[2:3], vec_ref[3:4]
    r_k, ln_w, ln_b = vec_ref[4:5], vec_ref[5:6], vec_ref[6:7]

    x_w = w0 + _dot(jnp.tanh(lo).astype(BF16), w2_ref[...])
    lw = -EXP_NEG_HALF * jax.nn.sigmoid(x_w)
    a = jax.nn.sigmoid(a0 + _dot(lo.astype(BF16), a2_ref[...]))
    g = _dot(jax.nn.sigmoid(g_lo).astype(BF16), g2_ref[...])

    ind = ind_ref[...]
    kk = k * k_k
    kk = kk * lax.rsqrt(jnp.maximum(_head_sum(kk * kk, ind), 1e-24))
    k2 = k * (1.0 + (a - 1.0) * k_a)
    alpha = kk * a

    b = _dot_split(tri_ref[...], lw)
    b_prev = b - lw
    bcast = lambda row: jnp.broadcast_to(row, (c, d))
    b_mid = jnp.concatenate([bcast(b[ci * c + c // 2 - 1:ci * c + c // 2]) for ci in range(n_chunks)], axis=0)
    b_end = jnp.concatenate([bcast(b[ci * c + c - 1:ci * c + c]) for ci in range(n_chunks)], axis=0)
    kap_t = kk * jnp.exp(b_prev - b_mid)
    r_t = r * jnp.exp(b - b_mid)
    inv_t = jnp.exp(b_mid - b)
    alp_t, k_t = alpha * inv_t, k2 * inv_t
    kap_h = kk * jnp.exp(b_prev)
    r_h = r * jnp.exp(b)
    tail = jnp.exp(b_end - b)
    alp_c, k_c = alpha * tail, k2 * tail
    gam_end = jnp.exp(b_end)

    eye = (lax.broadcasted_iota(jnp.int32, (gc, gc), 0)
           == lax.broadcasted_iota(jnp.int32, (gc, gc), 1)).astype(F32)
    lane_head = lax.broadcasted_iota(jnp.int32, (c, gw), 1) // RWKV_HEAD

    def stacked(x, ci, gi):
        return _stack_heads(x[ci * c:(ci + 1) * c, gi * gw:(gi + 1) * gw], lane_head)

    pairs = [(ci, gi) for ci in range(n_chunks) for gi in range(n_groups)]
    half = gc // 2
    lane_pair = lax.broadcasted_iota(jnp.int32, (c, half), 1) // RWKV_HEAD
    tri_r = lax.broadcasted_iota(jnp.int32, (half, half), 0) % c
    tri_c = lax.broadcasted_iota(jnp.int32, (half, half), 1) % c
    zero_blk = jnp.zeros((half, half), F32)

    def pair_stack(x, ci, pi):
        xs = x[ci * c:(ci + 1) * c, pi * half:(pi + 1) * half]
        return jnp.concatenate([jnp.where(lane_pair == k, xs, 0.0) for k in range(2)], axis=0)

    def block_diag(a, b_):
        return jnp.concatenate([jnp.concatenate([a, zero_blk], axis=1),
                                jnp.concatenate([zero_blk, b_], axis=1)], axis=0)

    a_ka, a_kk, a_ra, a_rk, vs, n_pow, inv = {}, {}, {}, {}, {}, {}, {}
    for pr in pairs:
        ci, gi = pr
        blocks = []
        for pi in (2 * gi, 2 * gi + 1):
            ps = lambda x: pair_stack(x, ci, pi)
            kr = jnp.concatenate([ps(kap_t), ps(r_t)], axis=0).astype(BF16)
            ak = jnp.concatenate([ps(alp_t), ps(k_t)], axis=0).astype(BF16)
            blocks.append(_dot_nt(kr, ak))
        quad = lambda r0, c0, keep: block_diag(
            *[jnp.where(keep, blk[r0:r0 + half, c0:c0 + half], 0.0) for blk in blocks])
        a_ka[pr] = quad(0, 0, tri_c < tri_r)
        a_kk[pr] = quad(0, half, tri_c < tri_r).astype(BF16)
        a_ra[pr] = quad(half, 0, tri_c <= tri_r).astype(BF16)
        a_rk[pr] = quad(half, half, tri_c <= tri_r).astype(BF16)
        vs[pr] = stacked(v, *pr).astype(BF16)
    for pr in pairs:
        n_pow[pr] = a_ka[pr]
        inv[pr] = eye - n_pow[pr]
    def later_rows(x, t0):
        if t0 == 0:
            return x
        return jnp.concatenate([x[h * c + t0:(h + 1) * c] for h in range(RWKV_GROUP)], axis=0)

    def restore_rows(x, t0):
        if t0 == 0:
            return x
        rows = c - t0
        pad = jnp.zeros((t0, x.shape[1]), x.dtype)
        return jnp.concatenate(
            [blk for h in range(RWKV_GROUP) for blk in (pad, x[h * rows:(h + 1) * rows])], axis=0)

    for j in range(1, int(np.log2(c))):
        t0 = (1 << j) // 8 * 8
        for pr in pairs:
            pw = n_pow[pr]
            n_pow[pr] = restore_rows(_dot(later_rows(pw, t0).astype(BF16), pw.astype(BF16)), t0)
        for pr in pairs:
            upd = _dot(later_rows(inv[pr], t0).astype(BF16), n_pow[pr].astype(BF16))
            inv[pr] = inv[pr] + restore_rows(upd, t0)
    akv = {pr: _dot(a_kk[pr], vs[pr]).astype(BF16) for pr in pairs}
    pre = {}
    for pr in pairs:
        st = lambda x: stacked(x, *pr)
        invb = inv[pr].astype(BF16)
        kt = _dot(invb, st(kap_h).astype(BF16)).astype(BF16)
        w_mat = _dot(invb, akv[pr])
        akc = jnp.concatenate([st(alp_c), st(k_c)], axis=0).astype(BF16)
        pre[pr] = (kt, w_mat, a_ra[pr], a_rk[pr], st(r_h).astype(BF16), akc)

    y_rows = []
    for ci in range(n_chunks):
        s0 = [state_ref[gi] for gi in range(n_groups)]
        s0b = [x.astype(BF16) for x in s0]
        ub = [(-(_dot_nt(pre[ci, gi][0], s0b[gi]) + pre[ci, gi][1])).astype(BF16)
              for gi in range(n_groups)]
        ys = []
        for gi in range(n_groups):
            kt, w_mat, a_ra, a_rk, rh, akc = pre[ci, gi]
            uv = jnp.concatenate([ub[gi], vs[ci, gi]], axis=0)
            sl = slice(gi * gw, (gi + 1) * gw)
            state_ref[gi] = s0[gi] * gam_end[ci * c:ci * c + 1, sl] + _dot_tn(uv, akc)
            y_st = _dot_nt(rh, s0b[gi]) + _dot(a_ra, ub[gi]) + _dot(a_rk, vs[ci, gi])
            y = y_st[0:c]
            for h in range(1, RWKV_GROUP):
                y = y + y_st[h * c:(h + 1) * c]
            ys.append(y)
        y_rows.append(jnp.concatenate(ys, axis=1))
    y = jnp.concatenate(y_rows, axis=0)

    inv_n = 1.0 / RWKV_HEAD
    mean = _head_sum(y, ind) * inv_n
    yc = y - mean
    var = _head_sum(yc * yc, ind) * inv_n
    y = yc * lax.rsqrt(var + RWKV_GN_EPS) * ln_w + ln_b
    bonus = _head_sum(r * k2 * r_k, ind) * v
    o_ref[...] = ((y + bonus) * g).astype(o_ref.dtype)


def _rwkv(p_rwkv, mu, vecs, w2, a2, g2, *, batch, seq, n_chunks):
    t = p_rwkv.shape[0]
    c = RWKV_CHUNK
    tb = n_chunks * c
    nb = seq // tb
    gw = RWKV_GROUP * RWKV_HEAD
    head = np.arange(RWKV_DIM) // RWKV_HEAD
    ind = jnp.asarray(head[:, None] == head[None, :], BF16)
    tt = np.arange(tb)
    tri = jnp.asarray((tt[None, :] <= tt[:, None]) & (tt[None, :] // c == tt[:, None] // c), BF16)
    return pl.pallas_call(
        functools.partial(_rwkv_kernel, n_chunks=n_chunks),
        grid=(batch, nb),
        in_specs=[
            pl.BlockSpec((tb, RWKV_COLS), lambda b, j: (b * nb + j, 0)),
            _const_spec((1, RWKV_COLS)),
            _const_spec(vecs.shape),
            _const_spec(w2.shape),
            _const_spec(a2.shape),
            _const_spec(g2.shape),
            _const_spec(ind.shape),
            _const_spec(tri.shape),
        ],
        out_specs=pl.BlockSpec((tb, RWKV_DIM), lambda b, j: (b * nb + j, 0)),
        out_shape=jax.ShapeDtypeStruct((t, RWKV_DIM), BF16),
        scratch_shapes=[
            pltpu.VMEM((RWKV_HEADS // RWKV_GROUP, gw, gw), F32),
            pltpu.VMEM((1, RWKV_COLS), F32),
        ],
        compiler_params=_params("parallel", "arbitrary"),
        name="rwkv7",
    )(p_rwkv, mu, vecs, w2, a2, g2, ind, tri)


def _hgrn_masks():
    c = HG_CHUNK
    t = np.arange(c)
    mask = np.zeros((HG_LEVELS + 1, c, c), np.float32)
    mask[0] = np.eye(c)
    for l in range(1, HG_LEVELS + 1):
        m = 1 << l
        is_right = (t % m) >= m // 2
        same = (t[:, None] // m) == (t[None, :] // m)
        mask[l] = same & is_right[:, None] & ~is_right[None, :]
    return mask


def _hgrn_chunk(proj, lb, gain, tri, mask_ref, state_ref, side_jobs):
    c = HG_CHUNK
    jobs = iter(side_jobs)
    run_job = lambda: next(jobs, lambda: None)()
    q, f, vals, g = proj
    n = q.shape[1]
    q = q * jax.nn.sigmoid(q)
    forget = lb + (1.0 - lb) * jax.nn.sigmoid(f)
    key = 1.0 - forget
    logf = jnp.log(forget)

    b = _dot_split(tri, logf)
    e_cum = jnp.exp(b)
    b_end = b[c - 1:c, :]
    q_hat = (q * e_cum).astype(BF16)
    k_hat = (key * jnp.exp(b_end - b)).astype(BF16)
    decay_end = e_cum[c - 1:c, :]
    qb, kb = q.astype(BF16), key.astype(BF16)
    run_job()

    t_idx = lax.broadcasted_iota(jnp.int32, q.shape, 0)
    f_prev = pltpu.roll(forget, 1, 0)
    f_next = pltpu.roll(forget, c - 1, 0)
    level_ops = []
    for l in range(HG_LEVELS):
        m = 2 << l
        is_right = (t_idx & (m // 2)) != 0
        if m == 2:
            decay = jnp.where(is_right, forget, 1.0)
        elif m == 4:
            pos = t_idx & 3
            decay = jnp.where(pos == 0, f_next,
                              jnp.where(pos == 1, 1.0, jnp.where(pos == 2, forget, forget * f_prev)))
        else:
            b3 = b.reshape(c // m, m, n)
            b_ref = jnp.broadcast_to(b3[:, m // 2 - 1:m // 2, :], (c // m, m, n)).reshape(c, n)
            decay = jnp.exp(-jnp.abs(b - b_ref))
        level_ops.append((jnp.where(is_right, q, key) * decay).astype(BF16))
        run_job()

    heads = [slice(h * HG_K, (h + 1) * HG_K) for h in range(HG_HEADS)]
    attns = []
    for sl in heads:
        attn = mask_ref[0] * _dot_nt(qb[:, sl], kb[:, sl])
        for l in range(HG_LEVELS):
            ml = level_ops[l][:, sl]
            attn = attn + mask_ref[l + 1] * _dot_nt(ml, ml)
        attns.append(attn.astype(BF16))
        run_job()
    vhs = [vals[:, sl].astype(BF16) for sl in heads]
    sts = [state_ref[h] for h in range(HG_HEADS)]
    os = [_dot_nt(q_hat[:, sl], sts[h].astype(BF16)) + _dot(attns[h], vhs[h])
          for h, sl in enumerate(heads)]
    for h, sl in enumerate(heads):
        state_ref[h] = sts[h] * decay_end[:, sl] + _dot_tn(vhs[h], k_hat[:, sl])
    outs = []
    for h, sl in enumerate(heads):
        o = os[h]
        ms = jnp.mean(o * o, axis=-1, keepdims=True)
        gt = g[:, sl]
        outs.append((o * lax.rsqrt(ms + RMS_EPS) * gain * (gt * jax.nn.sigmoid(gt))).astype(BF16))
    for job in jobs:
        job()
    return jnp.concatenate(outs, axis=1)


def _hgrn_layer_kernel(x_ref, xn_ref, mod_ref, ngain_ref, win_ref, lbl_ref, gain_ref, tri_ref,
                       mask_ref, wout_ref, o_ref, state_ref, proj_ref, *, layer):
    c = HG_CHUNK
    mod = mod_ref[0]
    per_proj = proj_ref.shape[3] // HG_PROJ_PIECE

    def projection_jobs(x, slot):
        hin = _norm_mod(x, ngain_ref[...], mod, 0, 1).astype(BF16)

        def piece(k, p):
            def job():
                lo = p * HG_PROJ_PIECE
                proj_ref[slot, k, :, lo:lo + HG_PROJ_PIECE] = _dot(hin, win_ref[k * per_proj + p])
            return job

        return [piece(k, p) for k in range(4) for p in range(per_proj)]

    @pl.when(pl.program_id(1) == 0)
    def _():
        state_ref[...] = jnp.zeros_like(state_ref)
        for job in projection_jobs(x_ref[0:c, :], 0):
            job()

    logits = lbl_ref[...]
    pe = jnp.exp(logits - jnp.max(logits, axis=0, keepdims=True))
    lb = jnp.sum(pe[1:layer + 1], axis=0, keepdims=True) / jnp.sum(pe, axis=0, keepdims=True)

    def recur(slot, side_jobs):
        proj = tuple(proj_ref[slot, k] for k in range(4))
        return _hgrn_chunk(proj, lb, gain_ref[...], tri_ref[...], mask_ref, state_ref, side_jobs)

    y0 = recur(0, projection_jobs(x_ref[c:2 * c, :], 1))
    y1 = recur(1, projection_jobs(xn_ref[...], 0))
    y = jnp.concatenate([y0, y1], axis=0)
    o_ref[...] = x_ref[...] + mod[2:3] * _dot(y, wout_ref[...])


def _hgrn_layer(x, mod, norm_gain, w_in, lb_logits, gain, w_out, *, batch, seq, layer):
    t, d = x.shape
    c = HG_CHUNK
    n = HG_HEADS * HG_K
    nb = seq // (2 * c)
    tri = jnp.asarray(np.tril(np.ones((c, c))), BF16)
    mask = jnp.asarray(_hgrn_masks(), F32)
    w_pieces = w_in.reshape(d, 4 * n // HG_PROJ_PIECE, HG_PROJ_PIECE).transpose(1, 0, 2)
    next_chunk = lambda b, j: (b * 2 * nb + jnp.minimum(2 * j + 2, 2 * nb - 1), 0)
    return pl.pallas_call(
        functools.partial(_hgrn_layer_kernel, layer=layer),
        grid=(batch, nb),
        in_specs=[pl.BlockSpec((2 * c, d), lambda b, j: (b * nb + j, 0)),
                  pl.BlockSpec((c, d), next_chunk),
                  pl.BlockSpec((1, 6, d), lambda b, j: (b, 0, 0)),
                  _const_spec((1, d)), _const_spec(w_pieces.shape),
                  _const_spec(lb_logits.shape), _const_spec((1, HG_V)),
                  _const_spec(tri.shape), _const_spec(mask.shape), _const_spec(w_out.shape)],
        out_specs=pl.BlockSpec((2 * c, d), lambda b, j: (b * nb + j, 0)),
        out_shape=jax.ShapeDtypeStruct((t, d), F32),
        scratch_shapes=[pltpu.VMEM((HG_HEADS, HG_V, HG_K), F32),
                        pltpu.VMEM((2, 4, c, n), F32)],
        compiler_params=_params("parallel", "arbitrary"),
        name="hgrn2_layer",
    )(x, x, mod, norm_gain.reshape(1, d), w_pieces, lb_logits, gain, tri, mask, w_out)


def _pad_cols(w, n):
    return jnp.pad(w, ((0, 0), (0, n - w.shape[1])))


def _mla_weights(w_uq, w_ukv):
    hq = MLA_NOPE + MLA_ROPE
    wq = w_uq.reshape(Q_LORA, MLA_HEADS, hq)
    wq = jnp.pad(wq, ((0, 0), (0, 0), (0, MLA_HEAD_PAD - hq))).reshape(Q_LORA, -1)
    wkv = w_ukv.reshape(KV_LORA, MLA_HEADS, MLA_NOPE + MLA_V)
    wk = jnp.pad(wkv[:, :, :MLA_NOPE], ((0, 0), (0, 0), (0, MLA_HEAD_PAD - MLA_NOPE)))
    wk = wk.reshape(KV_LORA, -1)
    wv = wkv[:, :, MLA_NOPE:].reshape(KV_LORA, -1)
    n_extra = MLA_COLS_PAD - Q_LORA - KV_LORA
    place = np.zeros((n_extra, MLA_HEADS, MLA_HEAD_PAD), np.float32)
    for r in range(MLA_ROPE):
        place[r, :, MLA_NOPE + r] = 1.0
    wk_full = jnp.concatenate([wk, jnp.asarray(place.reshape(n_extra, -1))], axis=0)
    wv_full = jnp.concatenate([wv, jnp.zeros((n_extra, wv.shape[1]), F32)], axis=0)
    return wq.T.astype(BF16), wk_full.astype(BF16), wv_full.T.astype(BF16)


def _rope_inv_freq():
    inv = 1.0 / (ROPE_BASE ** (jnp.arange(0, MLA_ROPE, 2, dtype=F32) / MLA_ROPE))
    return inv.reshape(MLA_ROPE // 2, 1)


def kernel(x, c, positions, ada_w, ada_b, norm_mix, norm_ffn, w_in_even, mla_q_norm, mla_w_uq, mla_kv_norm, mla_w_ukv, rwkv_mu, rwkv_w0, rwkv_w2, rwkv_a0, rwkv_a2, rwkv_g2, rwkv_k_k, rwkv_k_a, rwkv_r_k, rwkv_ln_w, rwkv_ln_b, w_out_even, w_in_odd, hg_lb_logits, hg_out_norm, w_out_odd, ffn_w_gate, ffn_w_up, ffn_w_down, final_norm):
    batch, seq, d = x.shape
    depth = ada_w.shape[0]
    t = batch * seq
    tm = min(512, seq)
    xt = x.reshape(t, d)

    mod_all = _ada_mod(c, ada_w, ada_b).reshape(depth, batch, 6, d)

    for l in range(depth):
        mod = mod_all[l]
        j = l // 2
        if l % 2 == 0:
            w_in = w_in_even[j]
            w_mla = _pad_cols(w_in[:, :MLA_COLS], MLA_COLS_PAD).astype(BF16)
            w_rwkv = w_in[:, MLA_COLS:].astype(BF16)
            p_mla, p_rwkv = _in_proj(xt, mod, norm_mix[l], [w_mla, w_rwkv],
                                     seq=seq, sh=0, sc=1, tm=tm, out_dtype=BF16)
            wqt, wk, wvt = _mla_weights(mla_w_uq[j], mla_w_ukv[j])
            qt, kh, vt = _mla_prep(p_mla, positions, _rope_inv_freq(),
                                   mla_q_norm[j].reshape(1, -1), mla_kv_norm[j].reshape(1, -1),
                                   wqt, wk, wvt, batch=batch, seq=seq, tm=tm, tk=ATTN_TILE)
            y_a = _attention(qt, kh, vt, batch=batch, seq=seq, tq=ATTN_TILE, nh=ATTN_HEADS)

            zeros_lo = jnp.zeros((DECAY_LORA, RWKV_DIM), F32)
            w2 = jnp.concatenate([rwkv_w2[j], zeros_lo], axis=0).astype(BF16)
            a2 = jnp.concatenate([zeros_lo, rwkv_a2[j]], axis=0).astype(BF16)
            vecs = jnp.stack([rwkv_w0[j], rwkv_a0[j], rwkv_k_k[j], rwkv_k_a[j],
                              rwkv_r_k[j].reshape(-1), rwkv_ln_w[j], rwkv_ln_b[j],
                              jnp.zeros((RWKV_DIM,), F32)])
            y_b = _rwkv(p_rwkv, rwkv_mu[j].reshape(1, -1), vecs, w2, a2,
                        rwkv_g2[j].astype(BF16), batch=batch, seq=seq,
                        n_chunks=min(RWKV_BLOCK_CHUNKS, seq // RWKV_CHUNK))
            w_out = w_out_even[j].astype(BF16)
            n_a = MLA_HEADS * MLA_V
            ys, w_outs = [y_a, y_b], [w_out[:n_a], w_out[n_a:]]
        else:
            xt = _hgrn_layer(xt, mod, norm_mix[l], w_in_odd[j].astype(BF16), hg_lb_logits,
                             hg_out_norm[j].reshape(1, -1), w_out_odd[j].astype(BF16),
                             batch=batch, seq=seq, layer=l)
            ys, w_outs = [], []
        xt = _ffn(xt, mod, norm_ffn[l], ffn_w_gate[l].astype(BF16), ffn_w_up[l].astype(BF16),
                  ffn_w_down[l].astype(BF16), final_norm, ys, w_outs, seq=seq, tm=tm,
                  final=(l == depth - 1))
    return xt.reshape(batch, seq, d)
```

```python
import functools

import numpy as np
import jax
import jax.numpy as jnp
from jax import lax
from jax.experimental import pallas as pl
from jax.experimental.pallas import tpu as pltpu

F32 = jnp.float32
BF16 = jnp.bfloat16

RMS_EPS = 1e-6
LOG2_E = 1.4426950408889634
EXP_NEG_HALF = 0.6065306597126334
LANES = 128
VMEM_LIMIT = 56 * 1024 * 1024

MLA_HEADS = 8
MLA_NOPE = 64
MLA_ROPE = 32
MLA_V = 64
MLA_VA = 80
Q_LORA = 384
KV_LORA = 256
ROPE_BASE = 10000.0
MLA_COLS = Q_LORA + KV_LORA + MLA_ROPE
MLA_COLS_PAD = 768
MLA_HEAD_PAD = 128
ATTN_TILE = 256
ATTN_HEADS = 8

RWKV_HEAD = 64
RWKV_DIM = 512
RWKV_HEADS = 8
DECAY_LORA = 64
AAA_LORA = 64
GATE_LORA = 128
RWKV_GN_EPS = RWKV_HEAD * 1e-5
RWKV_COLS = 3 * RWKV_DIM + DECAY_LORA + AAA_LORA + GATE_LORA
RWKV_CHUNK = 64
RWKV_BLOCK_CHUNKS = 4
RWKV_GROUP = 4

HG_K = 128
HG_HEADS = 8
HG_V = 128
HG_CHUNK = 128
HG_LEVELS = 7
HG_PROJ_PIECE = 256


def _params(*sem):
    return pltpu.CompilerParams(dimension_semantics=sem, vmem_limit_bytes=VMEM_LIMIT)


def _const_spec(shape):
    nd = len(shape)
    return pl.BlockSpec(shape, lambda *_: (0,) * nd, pipeline_mode=pl.Buffered(1))


def _dot(a, b):
    return jnp.dot(a, b, preferred_element_type=F32)


def _dot_nt(a, b):
    return lax.dot_general(a, b, (((1,), (1,)), ((), ())), preferred_element_type=F32)


def _dot_tn(a, b):
    return lax.dot_general(a, b, (((0,), (0,)), ((), ())), preferred_element_type=F32)


def _dot_split(sel, x):
    hi = x.astype(BF16)
    lo = (x - hi.astype(F32)).astype(BF16)
    return _dot(sel, hi) + _dot(sel, lo)


def _head_sum(x, ind):
    return _dot(x.astype(BF16), ind)


def _ada_kernel(c_ref, w_ref, b_ref, o_ref):
    c = c_ref[...]
    cond = c * jax.nn.sigmoid(c)
    o_ref[0] = _dot(cond.astype(BF16), w_ref[0].astype(BF16)) + b_ref[0]


def _ada_mod(c, ada_w, ada_b):
    depth, d, n = ada_w.shape
    b = c.shape[0]
    tn = n // 4
    return pl.pallas_call(
        _ada_kernel,
        grid=(depth, n // tn),
        in_specs=[
            pl.BlockSpec((b, d), lambda l, j: (0, 0)),
            pl.BlockSpec((1, d, tn), lambda l, j: (l, 0, j)),
            pl.BlockSpec((1, 1, tn), lambda l, j: (l, 0, j)),
        ],
        out_specs=pl.BlockSpec((1, b, tn), lambda l, j: (l, 0, j)),
        out_shape=jax.ShapeDtypeStruct((depth, b, n), F32),
        compiler_params=_params("arbitrary", "arbitrary"),
        name="ada_mod",
    )(c, ada_w, ada_b.reshape(depth, 1, n))


def _norm_mod(x, gain, mod, sh, sc):
    ms = jnp.mean(x * x, axis=-1, keepdims=True)
    y = x * lax.rsqrt(ms + RMS_EPS) * gain
    return y * (1.0 + mod[sc:sc + 1]) + mod[sh:sh + 1]


def _in_proj_kernel(x_ref, mod_ref, gain_ref, *refs, n_out, sh, sc):
    w_refs, o_refs = refs[:n_out], refs[n_out:]
    h = _norm_mod(x_ref[...], gain_ref[...], mod_ref[0], sh, sc).astype(BF16)
    for w_ref, o_ref in zip(w_refs, o_refs):
        o_ref[...] = _dot(h, w_ref[...]).astype(o_ref.dtype)


def _in_proj(x, mod, gain, weights, *, seq, sh, sc, tm, out_dtype):
    t, d = x.shape
    per_b = seq // tm
    n_out = len(weights)
    in_specs = [
        pl.BlockSpec((tm, d), lambda i: (i, 0)),
        pl.BlockSpec((1, 6, d), lambda i: (i // per_b, 0, 0)),
        _const_spec((1, d)),
    ] + [_const_spec(w.shape) for w in weights]
    out_specs = [pl.BlockSpec((tm, w.shape[1]), lambda i: (i, 0)) for w in weights]
    out_shape = [jax.ShapeDtypeStruct((t, w.shape[1]), out_dtype) for w in weights]
    return pl.pallas_call(
        functools.partial(_in_proj_kernel, n_out=n_out, sh=sh, sc=sc),
        grid=(t // tm,),
        in_specs=in_specs,
        out_specs=out_specs,
        out_shape=out_shape,
        compiler_params=_params("parallel"),
        name="in_proj",
    )(x, mod, gain.reshape(1, d), *weights)


def _ffn_kernel(x_ref, mod_ref, gain_ref, wg_ref, wu_ref, wd_ref, fin_ref, *refs, n_in, final):
    y_refs, w_refs, o_ref = refs[:n_in], refs[n_in:2 * n_in], refs[2 * n_in]
    x = x_ref[...]
    mod = mod_ref[0]
    if n_in:
        mix = _dot(y_refs[0][...], w_refs[0][...])
        for y_ref, w_ref in zip(y_refs[1:], w_refs[1:]):
            mix += _dot(y_ref[...], w_ref[...])
        x = x + mod[2:3] * mix
    h = _norm_mod(x, gain_ref[...], mod, 3, 4).astype(BF16)
    gate = _dot(h, wg_ref[...])
    up = _dot(h, wu_ref[...])
    act = (gate * jax.nn.sigmoid(gate) * up).astype(BF16)
    y = x + mod[5:6] * _dot(act, wd_ref[...])
    if final:
        ms = jnp.mean(y * y, axis=-1, keepdims=True)
        y = y * lax.rsqrt(ms + RMS_EPS) * fin_ref[...]
    o_ref[...] = y


def _ffn(x, mod, gain, wg, wu, wd, fin, ys, w_outs, *, seq, tm, final):
    t, d = x.shape
    per_b = seq // tm
    return pl.pallas_call(
        functools.partial(_ffn_kernel, n_in=len(ys), final=final),
        grid=(t // tm,),
        in_specs=[
            pl.BlockSpec((tm, d), lambda i: (i, 0)),
            pl.BlockSpec((1, 6, d), lambda i: (i // per_b, 0, 0)),
            _const_spec((1, d)),
            _const_spec(wg.shape),
            _const_spec(wu.shape),
            _const_spec(wd.shape),
            _const_spec((1, d)),
        ] + [pl.BlockSpec((tm, y.shape[1]), lambda i: (i, 0)) for y in ys]
          + [_const_spec(w.shape) for w in w_outs],
        out_specs=pl.BlockSpec((tm, d), lambda i: (i, 0)),
        out_shape=jax.ShapeDtypeStruct((t, d), F32),
        compiler_params=_params("parallel"),
        name="ffn",
    )(x, mod, gain.reshape(1, d), wg, wu, wd, fin.reshape(1, d), *ys, *w_outs)


def _mla_prep_kernel(p_ref, pos_ref, invf_ref, qn_ref, kvn_ref, wqt_ref, wk_ref, wvt_ref, vone_ref,
                     qt_ref, k_ref, vt_ref, *, scale, tk):
    p = p_ref[...].astype(F32)
    tm = p.shape[0]
    half = MLA_ROPE // 2
    ang = invf_ref[...] * pos_ref[0].astype(F32)
    cos_t, sin_t = jnp.cos(ang), jnp.sin(ang)

    c_q = p[:, :Q_LORA]
    ms = jnp.mean(c_q * c_q, axis=-1, keepdims=True)
    cqn = (c_q * lax.rsqrt(ms + RMS_EPS) * qn_ref[...]).astype(BF16)
    qt = _dot_nt(wqt_ref[...], cqn) * scale
    pieces = []
    for h in range(MLA_HEADS):
        base = h * MLA_HEAD_PAD
        x1 = qt[base + MLA_NOPE:base + MLA_NOPE + half]
        x2 = qt[base + MLA_NOPE + half:base + MLA_NOPE + MLA_ROPE]
        pieces += [qt[base:base + MLA_NOPE], x1 * cos_t - x2 * sin_t, x1 * sin_t + x2 * cos_t,
                   qt[base + MLA_NOPE + MLA_ROPE:base + MLA_HEAD_PAD]]
    qt_ref[0] = jnp.concatenate(pieces, axis=0).astype(qt_ref.dtype)

    ckv = p[:, Q_LORA:Q_LORA + KV_LORA]
    ss = jnp.mean(ckv * ckv, axis=-1, keepdims=True)
    ckvn = ckv * lax.rsqrt(ss + RMS_EPS) * kvn_ref[...]
    kr = p[:, Q_LORA + KV_LORA:]
    rest = kr.shape[1] - MLA_ROPE
    cos_k = jnp.concatenate([cos_t, cos_t, jnp.ones((rest, tm), F32)], axis=0).T
    sin_k = jnp.concatenate([sin_t, sin_t, jnp.zeros((rest, tm), F32)], axis=0).T
    lane = lax.broadcasted_iota(jnp.int32, kr.shape, 1)
    partner = jnp.where(lane < half, -pltpu.roll(kr, kr.shape[1] - half, 1), pltpu.roll(kr, half, 1))
    lhs = jnp.concatenate([ckvn, kr * cos_k + partner * sin_k], axis=1).astype(BF16)
    k_ref[...] = _dot(lhs, wk_ref[...]).astype(k_ref.dtype)
    vt = (_dot_nt(wvt_ref[...], lhs) + vone_ref[...]).astype(vt_ref.dtype)
    for jj in range(tm // tk):
        vt_ref[0, jj] = vt[:, jj * tk:(jj + 1) * tk]


def _mla_prep(p_mla, pos, invf, q_norm, kv_norm, wqt, wk, wvt, v_ones, *, batch, seq, tm, tk):
    t = p_mla.shape[0]
    n_q = MLA_HEADS * MLA_HEAD_PAD
    n_v = MLA_HEADS * MLA_VA
    per_b = seq // tm
    scale = (MLA_NOPE + MLA_ROPE) ** -0.5 * LOG2_E
    return pl.pallas_call(
        functools.partial(_mla_prep_kernel, scale=scale, tk=tk),
        grid=(t // tm,),
        in_specs=[
            pl.BlockSpec((tm, MLA_COLS_PAD), lambda i: (i, 0)),
            pl.BlockSpec((1, 1, tm), lambda i: (i, 0, 0)),
            _const_spec(invf.shape),
            _const_spec((1, Q_LORA)),
            _const_spec((1, KV_LORA)),
            _const_spec(wqt.shape),
            _const_spec(wk.shape),
            _const_spec(wvt.shape),
            _const_spec(v_ones.shape),
        ],
        out_specs=[
            pl.BlockSpec((1, n_q, tm), lambda i: (i // per_b, 0, i % per_b)),
            pl.BlockSpec((tm, n_q), lambda i: (i, 0)),
            pl.BlockSpec((1, tm // tk, n_v, tk), lambda i: (i // per_b, i % per_b, 0, 0)),
        ],
        out_shape=[
            jax.ShapeDtypeStruct((batch, n_q, seq), BF16),
            jax.ShapeDtypeStruct((t, n_q), BF16),
            jax.ShapeDtypeStruct((batch, seq // tk, n_v, tk), BF16),
        ],
        compiler_params=_params("parallel"),
        name="mla_prep",
    )(p_mla, pos.reshape(t // tm, 1, tm), invf, q_norm, kv_norm, wqt, wk, wvt, v_ones)


def _attn_kernel(qt_ref, k_ref, vt_ref, o_ref, s_ref, p_ref, *, tq, nh):
    i = pl.program_id(2)
    qts = [qt_ref[0, h * MLA_HEAD_PAD:(h + 1) * MLA_HEAD_PAD, :] for h in range(nh)]
    causal = (lax.broadcasted_iota(jnp.int32, (tq, tq), 0)
              <= lax.broadcasted_iota(jnp.int32, (tq, tq), 1))

    def put_scores(j, slot):
        start = pl.multiple_of(j * tq, tq)
        for h in range(nh):
            s_ref[slot, h] = _dot(
                k_ref[pl.ds(start, tq), h * MLA_HEAD_PAD:(h + 1) * MLA_HEAD_PAD], qts[h])

    def value_dots(j, probs):
        return [_dot(vt_ref[0, j, h * MLA_VA:(h + 1) * MLA_VA, :], probs[h]) for h in range(nh)]

    def step(carry, slot, pv_prev, masked):
        new, probs = [], []
        for h in range(nh):
            m, acc = carry[h]
            s = s_ref[slot, h]
            if masked:
                s = jnp.where(causal, s, -jnp.inf)
            m_new = jnp.maximum(m, jnp.max(s, axis=0, keepdims=True))
            probs.append(jnp.exp2(s - m_new).astype(BF16))
            new.append((m_new, jnp.exp2(m - m_new) * (acc + pv_prev[h])))
        return tuple(new), probs

    def parked(slot):
        return [p_ref[slot, h] for h in range(nh)]

    def pair(t, carry):
        j = 2 * t
        put_scores(j + 1, 1)
        pv = value_dots(jnp.where(t == 0, i, j - 1), parked(1))
        carry, probs = step(carry, 0, pv, False)
        for h in range(nh):
            p_ref[0, h] = probs[h]
        put_scores(j + 2, 0)
        pv = value_dots(j, parked(0))
        carry, probs = step(carry, 1, pv, False)
        for h in range(nh):
            p_ref[1, h] = probs[h]
        return carry

    def emit(carry, pv_last):
        outs = []
        for h, (_, acc) in enumerate(carry):
            tot = acc + pv_last[h]
            outs.append(tot[:MLA_V] * (1.0 / tot[MLA_V:MLA_V + 1]))
        o_ref[...] = jnp.concatenate(outs, axis=0).T.astype(o_ref.dtype)

    put_scores(i, 1)
    put_scores(0, 0)
    zeros = [jnp.zeros((MLA_VA, tq), F32)] * nh
    init = tuple((jnp.full((1, tq), -jnp.inf, F32), zeros[h]) for h in range(nh))
    carry, probs = step(init, 1, zeros, True)
    for h in range(nh):
        p_ref[1, h] = probs[h]
    carry = lax.fori_loop(0, i // 2, pair, carry)
    n_done = 2 * (i // 2)
    last = jnp.where(n_done == 0, i, n_done - 1)

    @pl.when(i % 2 == 0)
    def _():
        emit(carry, value_dots(last, parked(1)))

    @pl.when(i % 2 == 1)
    def _():
        final, probs = step(carry, 0, value_dots(last, parked(1)), False)
        emit(final, value_dots(i - 1, probs))


def _attention(qt, k, vt, *, batch, seq, tq, nh):
    t = k.shape[0]
    nq = seq // tq
    return pl.pallas_call(
        functools.partial(_attn_kernel, tq=tq, nh=nh),
        grid=(batch, MLA_HEADS // nh, nq),
        in_specs=[
            pl.BlockSpec((1, nh * MLA_HEAD_PAD, tq), lambda b, h, i: (b, h, i)),
            pl.BlockSpec((seq, nh * MLA_HEAD_PAD), lambda b, h, i: (b, h)),
            pl.BlockSpec((1, seq // tq, nh * MLA_VA, tq), lambda b, h, i: (b, 0, h, 0)),
        ],
        out_specs=pl.BlockSpec((tq, nh * MLA_V), lambda b, h, i: (b * nq + i, h)),
        out_shape=jax.ShapeDtypeStruct((t, MLA_HEADS * MLA_V), BF16),
        scratch_shapes=[pltpu.VMEM((2, nh, tq, tq), F32), pltpu.VMEM((2, nh, tq, tq), BF16)],
        compiler_params=_params("parallel", "parallel", "arbitrary"),
        name="mla_attention",
    )(qt, k, vt)


def _stack_heads(x, lane_head):
    return jnp.concatenate(
        [jnp.where(lane_head == h, x, 0.0) for h in range(RWKV_GROUP)], axis=0)


def _rwkv_kernel(p_ref, mu_ref, vec_ref, w2_ref, a2_ref, g2_ref, ind_ref, tri_ref,
                 o_ref, state_ref, prev_ref, *, n_chunks):
    c = RWKV_CHUNK
    gw = RWKV_GROUP * RWKV_HEAD
    gc = RWKV_GROUP * c
    n_groups = RWKV_HEADS // RWKV_GROUP
    tb = n_chunks * c

    @pl.when(pl.program_id(1) == 0)
    def _():
        state_ref[...] = jnp.zeros_like(state_ref)
        prev_ref[...] = jnp.zeros_like(prev_ref)

    p = p_ref[...].astype(F32)
    rows = lax.broadcasted_iota(jnp.int32, p.shape, 0)
    shifted = jnp.where(rows == 0, prev_ref[...], pltpu.roll(p, 1, 0))
    prev_ref[...] = p[tb - 1:tb, :]
    p = p + (shifted - p) * mu_ref[...]

    d = RWKV_DIM
    r, k, v = p[:, :d], p[:, d:2 * d], p[:, 2 * d:3 * d]
    lo = p[:, 3 * d:3 * d + DECAY_LORA + AAA_LORA]
    g_lo = p[:, 3 * d + DECAY_LORA + AAA_LORA:]
    w0, a0, k_k, k_a = vec_ref[0:1], vec_ref[1:2], vec_ref[2:3], vec_ref[3:4]
    r_k, ln_w, ln_b = vec_ref[4:5], vec_ref[5:6], vec_ref[6:7]

    x_w = w0 + _dot(jnp.tanh(lo).astype(BF16), w2_ref[...])
    lw = -EXP_NEG_HALF * jax.nn.sigmoid(x_w)
    a = jax.nn.sigmoid(a0 + _dot(lo.astype(BF16), a2_ref[...]))
    g = _dot(jax.nn.sigmoid(g_lo).astype(BF16), g2_ref[...])

    ind = ind_ref[...]
    kk = k * k_k
    kk = kk * lax.rsqrt(jnp.maximum(_head_sum(kk * kk, ind), 1e-24))
    k2 = k * (1.0 + (a - 1.0) * k_a)
    alpha = kk * a

    b = _dot_split(tri_ref[...], lw)
    b_prev = b - lw
    bcast = lambda row: jnp.broadcast_to(row, (c, d))
    b_mid = jnp.concatenate([bcast(b[ci * c + c // 2 - 1:ci * c + c // 2]) for ci in range(n_chunks)], axis=0)
    b_end = jnp.concatenate([bcast(b[ci * c + c - 1:ci * c + c]) for ci in range(n_chunks)], axis=0)
    kap_t = kk * jnp.exp(b_prev - b_mid)
    r_t = r * jnp.exp(b - b_mid)
    inv_t = jnp.exp(b_mid - b)
    alp_t, k_t = alpha * inv_t, k2 * inv_t
    kap_h = kk * jnp.exp(b_prev)
    r_h = r * jnp.exp(b)
    tail = jnp.exp(b_end - b)
    alp_c, k_c = alpha * tail, k2 * tail
    gam_end = jnp.exp(b_end)

    eye = (lax.broadcasted_iota(jnp.int32, (gc, gc), 0)
           == lax.broadcasted_iota(jnp.int32, (gc, gc), 1)).astype(F32)
    lane_head = lax.broadcasted_iota(jnp.int32, (c, gw), 1) // RWKV_HEAD

    def stacked(x, ci, gi):
        return _stack_heads(x[ci * c:(ci + 1) * c, gi * gw:(gi + 1) * gw], lane_head)

    pairs = [(ci, gi) for ci in range(n_chunks) for gi in range(n_groups)]
    half = gc // 2
    lane_pair = lax.broadcasted_iota(jnp.int32, (c, half), 1) // RWKV_HEAD
    tri_r = lax.broadcasted_iota(jnp.int32, (half, half), 0) % c
    tri_c = lax.broadcasted_iota(jnp.int32, (half, half), 1) % c
    zero_blk = jnp.zeros((half, half), F32)

    def pair_stack(x, ci, pi):
        xs = x[ci * c:(ci + 1) * c, pi * half:(pi + 1) * half]
        return jnp.concatenate([jnp.where(lane_pair == k, xs, 0.0) for k in range(2)], axis=0)

    def block_diag(a, b_):
        return jnp.concatenate([jnp.concatenate([a, zero_blk], axis=1),
                                jnp.concatenate([zero_blk, b_], axis=1)], axis=0)

    a_ka, a_kk, a_ra, a_rk, vs, n_pow, inv = {}, {}, {}, {}, {}, {}, {}
    for pr in pairs:
        ci, gi = pr
        blocks = []
        for pi in (2 * gi, 2 * gi + 1):
            ps = lambda x: pair_stack(x, ci, pi)
            kr = jnp.concatenate([ps(kap_t), ps(r_t)], axis=0).astype(BF16)
            ak = jnp.concatenate([ps(alp_t), ps(k_t)], axis=0).astype(BF16)
            blocks.append(_dot_nt(kr, ak))
        quad = lambda r0, c0, keep: block_diag(
            *[jnp.where(keep, blk[r0:r0 + half, c0:c0 + half], 0.0) for blk in blocks])
        a_ka[pr] = quad(0, 0, tri_c < tri_r)
        a_kk[pr] = quad(0, half, tri_c < tri_r).astype(BF16)
        a_ra[pr] = quad(half, 0, tri_c <= tri_r).astype(BF16)
        a_rk[pr] = quad(half, half, tri_c <= tri_r).astype(BF16)
        vs[pr] = stacked(v, *pr).astype(BF16)
    for pr in pairs:
        n_pow[pr] = a_ka[pr]
        inv[pr] = eye - n_pow[pr]
    def later_rows(x, t0):
        if t0 == 0:
            return x
        return jnp.concatenate([x[h * c + t0:(h + 1) * c] for h in range(RWKV_GROUP)], axis=0)

    def restore_rows(x, t0):
        if t0 == 0:
            return x
        rows = c - t0
        pad = jnp.zeros((t0, x.shape[1]), x.dtype)
        return jnp.concatenate(
            [blk for h in range(RWKV_GROUP) for blk in (pad, x[h * rows:(h + 1) * rows])], axis=0)

    for j in range(1, int(np.log2(c))):
        t0 = (1 << j) // 8 * 8
        for pr in pairs:
            pw = n_pow[pr]
            n_pow[pr] = restore_rows(_dot(later_rows(pw, t0).astype(BF16), pw.astype(BF16)), t0)
        for pr in pairs:
            upd = _dot(later_rows(inv[pr], t0).astype(BF16), n_pow[pr].astype(BF16))
            inv[pr] = inv[pr] + restore_rows(upd, t0)
    akv = {pr: _dot(a_kk[pr], vs[pr]).astype(BF16) for pr in pairs}
    pre = {}
    for pr in pairs:
        st = lambda x: stacked(x, *pr)
        invb = inv[pr].astype(BF16)
        kt = _dot(invb, st(kap_h).astype(BF16)).astype(BF16)
        w_mat = _dot(invb, akv[pr])
        akc = jnp.concatenate([st(alp_c), st(k_c)], axis=0).astype(BF16)
        pre[pr] = (kt, w_mat, a_ra[pr], a_rk[pr], st(r_h).astype(BF16), akc)

    y_rows = []
    for ci in range(n_chunks):
        s0 = [state_ref[gi] for gi in range(n_groups)]
        s0b = [x.astype(BF16) for x in s0]
        ub = [(-(_dot_nt(pre[ci, gi][0], s0b[gi]) + pre[ci, gi][1])).astype(BF16)
              for gi in range(n_groups)]
        ys = []
        for gi in range(n_groups):
            kt, w_mat, a_ra, a_rk, rh, akc = pre[ci, gi]
            uv = jnp.concatenate([ub[gi], vs[ci, gi]], axis=0)
            sl = slice(gi * gw, (gi + 1) * gw)
            state_ref[gi] = s0[gi] * gam_end[ci * c:ci * c + 1, sl] + _dot_tn(uv, akc)
            y_st = _dot_nt(rh, s0b[gi]) + _dot(a_ra, ub[gi]) + _dot(a_rk, vs[ci, gi])
            y = y_st[0:c]
            for h in range(1, RWKV_GROUP):
                y = y + y_st[h * c:(h + 1) * c]
            ys.append(y)
        y_rows.append(jnp.concatenate(ys, axis=1))
    y = jnp.concatenate(y_rows, axis=0)

    inv_n = 1.0 / RWKV_HEAD
    mean = _head_sum(y, ind) * inv_n
    yc = y - mean
    var = _head_sum(yc * yc, ind) * inv_n
    y = yc * lax.rsqrt(var + RWKV_GN_EPS) * ln_w + ln_b
    bonus = _head_sum(r * k2 * r_k, ind) * v
    o_ref[...] = ((y + bonus) * g).astype(o_ref.dtype)


def _rwkv(p_rwkv, mu, vecs, w2, a2, g2, *, batch, seq, n_chunks):
    t = p_rwkv.shape[0]
    c = RWKV_CHUNK
    tb = n_chunks * c
    nb = seq // tb
    gw = RWKV_GROUP * RWKV_HEAD
    head = np.arange(RWKV_DIM) // RWKV_HEAD
    ind = jnp.asarray(head[:, None] == head[None, :], BF16)
    tt = np.arange(tb)
    tri = jnp.asarray((tt[None, :] <= tt[:, None]) & (tt[None, :] // c == tt[:, None] // c), BF16)
    return pl.pallas_call(
        functools.partial(_rwkv_kernel, n_chunks=n_chunks),
        grid=(batch, nb),
        in_specs=[
            pl.BlockSpec((tb, RWKV_COLS), lambda b, j: (b * nb + j, 0)),
            _const_spec((1, RWKV_COLS)),
            _const_spec(vecs.shape),
            _const_spec(w2.shape),
            _const_spec(a2.shape),
            _const_spec(g2.shape),
            _const_spec(ind.shape),
            _const_spec(tri.shape),
        ],
        out_specs=pl.BlockSpec((tb, RWKV_DIM), lambda b, j: (b * nb + j, 0)),
        out_shape=jax.ShapeDtypeStruct((t, RWKV_DIM), BF16),
        scratch_shapes=[
            pltpu.VMEM((RWKV_HEADS // RWKV_GROUP, gw, gw), F32),
            pltpu.VMEM((1, RWKV_COLS), F32),
        ],
        compiler_params=_params("parallel", "arbitrary"),
        name="rwkv7",
    )(p_rwkv, mu, vecs, w2, a2, g2, ind, tri)


def _hgrn_masks():
    c = HG_CHUNK
    t = np.arange(c)
    mask = np.zeros((HG_LEVELS + 1, c, c), np.float32)
    mask[0] = np.eye(c)
    for l in range(1, HG_LEVELS + 1):
        m = 1 << l
        is_right = (t % m) >= m // 2
        same = (t[:, None] // m) == (t[None, :] // m)
        mask[l] = same & is_right[:, None] & ~is_right[None, :]
    return mask


def _hgrn_chunk(proj, lb, gain, tri, mask_ref, state_ref, side_jobs):
    c = HG_CHUNK
    jobs = iter(side_jobs)
    run_job = lambda: next(jobs, lambda: None)()
    q, f, vals, g = proj
    n = q.shape[1]
    q = q * jax.nn.sigmoid(q)
    forget = lb + (1.0 - lb) * jax.nn.sigmoid(f)
    key = 1.0 - forget
    logf = jnp.log(forget)

    b = _dot_split(tri, logf)
    e_cum = jnp.exp(b)
    b_end = b[c - 1:c, :]
    q_hat = (q * e_cum).astype(BF16)
    k_hat = (key * jnp.exp(b_end - b)).astype(BF16)
    decay_end = e_cum[c - 1:c, :]
    qb, kb = q.astype(BF16), key.astype(BF16)
    run_job()

    t_idx = lax.broadcasted_iota(jnp.int32, q.shape, 0)
    f_prev = pltpu.roll(forget, 1, 0)
    f_next = pltpu.roll(forget, c - 1, 0)
    level_ops = []
    for l in range(HG_LEVELS):
        m = 2 << l
        is_right = (t_idx & (m // 2)) != 0
        if m == 2:
            decay = jnp.where(is_right, forget, 1.0)
        elif m == 4:
            pos = t_idx & 3
            decay = jnp.where(pos == 0, f_next,
                              jnp.where(pos == 1, 1.0, jnp.where(pos == 2, forget, forget * f_prev)))
        else:
            b3 = b.reshape(c // m, m, n)
            b_ref = jnp.broadcast_to(b3[:, m // 2 - 1:m // 2, :], (c // m, m, n)).reshape(c, n)
            decay = jnp.exp(-jnp.abs(b - b_ref))
        level_ops.append((jnp.where(is_right, q, key) * decay).astype(BF16))
        run_job()

    heads = [slice(h * HG_K, (h + 1) * HG_K) for h in range(HG_HEADS)]
    attns = []
    for sl in heads:
        attn = mask_ref[0] * _dot_nt(qb[:, sl], kb[:, sl])
        for l in range(HG_LEVELS):
            ml = level_ops[l][:, sl]
            attn = attn + mask_ref[l + 1] * _dot_nt(ml, ml)
        attns.append(attn.astype(BF16))
        run_job()
    vhs = [vals[:, sl].astype(BF16) for sl in heads]
    sts = [state_ref[h] for h in range(HG_HEADS)]
    os = [_dot_nt(q_hat[:, sl], sts[h].astype(BF16)) + _dot(attns[h], vhs[h])
          for h, sl in enumerate(heads)]
    for h, sl in enumerate(heads):
        state_ref[h] = sts[h] * decay_end[:, sl] + _dot_tn(vhs[h], k_hat[:, sl])
    outs = []
    for h, sl in enumerate(heads):
        o = os[h]
        ms = jnp.mean(o * o, axis=-1, keepdims=True)
        gt = g[:, sl]
        outs.append((o * lax.rsqrt(ms + RMS_EPS) * gain * (gt * jax.nn.sigmoid(gt))).astype(BF16))
    for job in jobs:
        job()
    return jnp.concatenate(outs, axis=1)


def _hgrn_layer_kernel(x_ref, xn_ref, mod_ref, ngain_ref, win_ref, lbl_ref, gain_ref, tri_ref,
                       mask_ref, wout_ref, o_ref, state_ref, proj_ref, *, layer):
    c = HG_CHUNK
    mod = mod_ref[0]
    per_proj = proj_ref.shape[3] // HG_PROJ_PIECE

    def projection_jobs(x, slot):
        hin = _norm_mod(x, ngain_ref[...], mod, 0, 1).astype(BF16)

        def piece(k, p):
            def job():
                lo = p * HG_PROJ_PIECE
                proj_ref[slot, k, :, lo:lo + HG_PROJ_PIECE] = _dot(hin, win_ref[k * per_proj + p])
            return job

        return [piece(k, p) for k in range(4) for p in range(per_proj)]

    @pl.when(pl.program_id(1) == 0)
    def _():
        state_ref[...] = jnp.zeros_like(state_ref)
        for job in projection_jobs(x_ref[0:c, :], 0):
            job()

    logits = lbl_ref[...]
    pe = jnp.exp(logits - jnp.max(logits, axis=0, keepdims=True))
    lb = jnp.sum(pe[1:layer + 1], axis=0, keepdims=True) / jnp.sum(pe, axis=0, keepdims=True)

    def recur(slot, side_jobs):
        proj = tuple(proj_ref[slot, k] for k in range(4))
        return _hgrn_chunk(proj, lb, gain_ref[...], tri_ref[...], mask_ref, state_ref, side_jobs)

    y0 = recur(0, projection_jobs(x_ref[c:2 * c, :], 1))
    y1 = recur(1, projection_jobs(xn_ref[...], 0))
    y = jnp.concatenate([y0, y1], axis=0)
    o_ref[...] = x_ref[...] + mod[2:3] * _dot(y, wout_ref[...])


def _hgrn_layer(x, mod, norm_gain, w_in, lb_logits, gain, w_out, *, batch, seq, layer):
    t, d = x.shape
    c = HG_CHUNK
    n = HG_HEADS * HG_K
    nb = seq // (2 * c)
    tri = jnp.asarray(np.tril(np.ones((c, c))), BF16)
    mask = jnp.asarray(_hgrn_masks(), F32)
    w_pieces = w_in.reshape(d, 4 * n // HG_PROJ_PIECE, HG_PROJ_PIECE).transpose(1, 0, 2)
    next_chunk = lambda b, j: (b * 2 * nb + jnp.minimum(2 * j + 2, 2 * nb - 1), 0)
    return pl.pallas_call(
        functools.partial(_hgrn_layer_kernel, layer=layer),
        grid=(batch, nb),
        in_specs=[pl.BlockSpec((2 * c, d), lambda b, j: (b * nb + j, 0)),
                  pl.BlockSpec((c, d), next_chunk),
                  pl.BlockSpec((1, 6, d), lambda b, j: (b, 0, 0)),
                  _const_spec((1, d)), _const_spec(w_pieces.shape),
                  _const_spec(lb_logits.shape), _const_spec((1, HG_V)),
                  _const_spec(tri.shape), _const_spec(mask.shape), _const_spec(w_out.shape)],
        out_specs=pl.BlockSpec((2 * c, d), lambda b, j: (b * nb + j, 0)),
        out_shape=jax.ShapeDtypeStruct((t, d), F32),
        scratch_shapes=[pltpu.VMEM((HG_HEADS, HG_V, HG_K), F32),
                        pltpu.VMEM((2, 4, c, n), F32)],
        compiler_params=_params("parallel", "arbitrary"),
        name="hgrn2_layer",
    )(x, x, mod, norm_gain.reshape(1, d), w_pieces, lb_logits, gain, tri, mask, w_out)


def _pad_cols(w, n):
    return jnp.pad(w, ((0, 0), (0, n - w.shape[1])))


def _mla_weights(w_uq, w_ukv):
    hq = MLA_NOPE + MLA_ROPE
    wq = w_uq.reshape(Q_LORA, MLA_HEADS, hq)
    wq = jnp.pad(wq, ((0, 0), (0, 0), (0, MLA_HEAD_PAD - hq))).reshape(Q_LORA, -1)
    wkv = w_ukv.reshape(KV_LORA, MLA_HEADS, MLA_NOPE + MLA_V)
    wk = jnp.pad(wkv[:, :, :MLA_NOPE], ((0, 0), (0, 0), (0, MLA_HEAD_PAD - MLA_NOPE)))
    wk = wk.reshape(KV_LORA, -1)
    n_extra = MLA_COLS_PAD - Q_LORA - KV_LORA
    place = np.zeros((n_extra, MLA_HEADS, MLA_HEAD_PAD), np.float32)
    for r in range(MLA_ROPE):
        place[r, :, MLA_NOPE + r] = 1.0
    wk_full = jnp.concatenate([wk, jnp.asarray(place.reshape(n_extra, -1))], axis=0)
    wv = jnp.pad(wkv[:, :, MLA_NOPE:], ((0, 0), (0, 0), (0, MLA_VA - MLA_V))).reshape(KV_LORA, -1)
    wv_full = jnp.concatenate([wv, jnp.zeros((n_extra, wv.shape[1]), F32)], axis=0)
    v_ones = np.zeros((MLA_HEADS, MLA_VA, 1), np.float32)
    v_ones[:, MLA_V] = 1.0
    return (wq.T.astype(BF16), wk_full.astype(BF16), wv_full.T.astype(BF16),
            jnp.asarray(v_ones.reshape(-1, 1)))


def _rope_inv_freq():
    inv = 1.0 / (ROPE_BASE ** (jnp.arange(0, MLA_ROPE, 2, dtype=F32) / MLA_ROPE))
    return inv.reshape(MLA_ROPE // 2, 1)


def kernel(x, c, positions, ada_w, ada_b, norm_mix, norm_ffn, w_in_even, mla_q_norm, mla_w_uq, mla_kv_norm, mla_w_ukv, rwkv_mu, rwkv_w0, rwkv_w2, rwkv_a0, rwkv_a2, rwkv_g2, rwkv_k_k, rwkv_k_a, rwkv_r_k, rwkv_ln_w, rwkv_ln_b, w_out_even, w_in_odd, hg_lb_logits, hg_out_norm, w_out_odd, ffn_w_gate, ffn_w_up, ffn_w_down, final_norm):
    batch, seq, d = x.shape
    depth = ada_w.shape[0]
    t = batch * seq
    tm = min(512, seq)
    xt = x.reshape(t, d)

    mod_all = _ada_mod(c, ada_w, ada_b).reshape(depth, batch, 6, d)

    for l in range(depth):
        mod = mod_all[l]
        j = l // 2
        if l % 2 == 0:
            w_in = w_in_even[j]
            w_mla = _pad_cols(w_in[:, :MLA_COLS], MLA_COLS_PAD).astype(BF16)
            w_rwkv = w_in[:, MLA_COLS:].astype(BF16)
            p_mla, p_rwkv = _in_proj(xt, mod, norm_mix[l], [w_mla, w_rwkv],
                                     seq=seq, sh=0, sc=1, tm=tm, out_dtype=BF16)
            wqt, wk, wvt, v_ones = _mla_weights(mla_w_uq[j], mla_w_ukv[j])
            qt, kh, vt = _mla_prep(p_mla, positions, _rope_inv_freq(),
                                   mla_q_norm[j].reshape(1, -1), mla_kv_norm[j].reshape(1, -1),
                                   wqt, wk, wvt, v_ones, batch=batch, seq=seq, tm=tm, tk=ATTN_TILE)
            y_a = _attention(qt, kh, vt, batch=batch, seq=seq, tq=ATTN_TILE, nh=ATTN_HEADS)

            zeros_lo = jnp.zeros((DECAY_LORA, RWKV_DIM), F32)
            w2 = jnp.concatenate([rwkv_w2[j], zeros_lo], axis=0).astype(BF16)
            a2 = jnp.concatenate([zeros_lo, rwkv_a2[j]], axis=0).astype(BF16)
            vecs = jnp.stack([rwkv_w0[j], rwkv_a0[j], rwkv_k_k[j], rwkv_k_a[j],
                              rwkv_r_k[j].reshape(-1), rwkv_ln_w[j], rwkv_ln_b[j],
                              jnp.zeros((RWKV_DIM,), F32)])
            y_b = _rwkv(p_rwkv, rwkv_mu[j].reshape(1, -1), vecs, w2, a2,
                        rwkv_g2[j].astype(BF16), batch=batch, seq=seq,
                        n_chunks=min(RWKV_BLOCK_CHUNKS, seq // RWKV_CHUNK))
            w_out = w_out_even[j].astype(BF16)
            n_a = MLA_HEADS * MLA_V
            ys, w_outs = [y_a, y_b], [w_out[:n_a], w_out[n_a:]]
        else:
            xt = _hgrn_layer(xt, mod, norm_mix[l], w_in_odd[j].astype(BF16), hg_lb_logits,
                             hg_out_norm[j].reshape(1, -1), w_out_odd[j].astype(BF16),
                             batch=batch, seq=seq, layer=l)
            ys, w_outs = [], []
        xt = _ffn(xt, mod, norm_ffn[l], ffn_w_gate[l].astype(BF16), ffn_w_up[l].astype(BF16),
                  ffn_w_down[l].astype(BF16), final_norm, ys, w_outs, seq=seq, tm=tm,
                  final=(l == depth - 1))
    return xt.reshape(batch, seq, d)
```

```python
import functools

import numpy as np
import jax
import jax.numpy as jnp
from jax import lax
from jax.experimental import pallas as pl
from jax.experimental.pallas import tpu as pltpu

F32 = jnp.float32
BF16 = jnp.bfloat16

RMS_EPS = 1e-6
LOG2_E = 1.4426950408889634
EXP_NEG_HALF = 0.6065306597126334
LANES = 128
VMEM_LIMIT = 56 * 1024 * 1024

MLA_HEADS = 8
MLA_NOPE = 64
MLA_ROPE = 32
MLA_V = 64
MLA_VA = 80
Q_LORA = 384
KV_LORA = 256
ROPE_BASE = 10000.0
MLA_COLS = Q_LORA + KV_LORA + MLA_ROPE
MLA_COLS_PAD = 768
MLA_HEAD_PAD = 128
ATTN_TILE = 256
ATTN_HEADS = 8

RWKV_HEAD = 64
RWKV_DIM = 512
RWKV_HEADS = 8
DECAY_LORA = 64
AAA_LORA = 64
GATE_LORA = 128
RWKV_GN_EPS = RWKV_HEAD * 1e-5
RWKV_COLS = 3 * RWKV_DIM + DECAY_LORA + AAA_LORA + GATE_LORA
RWKV_CHUNK = 64
RWKV_BLOCK_CHUNKS = 4
RWKV_GROUP = 4

HG_K = 128
HG_HEADS = 8
HG_V = 128
HG_CHUNK = 128
HG_LEVELS = 7
HG_PROJ_PIECE = 256


def _params(*sem):
    return pltpu.CompilerParams(dimension_semantics=sem, vmem_limit_bytes=VMEM_LIMIT)


def _const_spec(shape):
    nd = len(shape)
    return pl.BlockSpec(shape, lambda *_: (0,) * nd, pipeline_mode=pl.Buffered(1))


def _dot(a, b):
    return jnp.dot(a, b, preferred_element_type=F32)


def _dot_nt(a, b):
    return lax.dot_general(a, b, (((1,), (1,)), ((), ())), preferred_element_type=F32)


def _dot_tn(a, b):
    return lax.dot_general(a, b, (((0,), (0,)), ((), ())), preferred_element_type=F32)


def _dot_split(sel, x):
    hi = x.astype(BF16)
    lo = (x - hi.astype(F32)).astype(BF16)
    return _dot(sel, hi) + _dot(sel, lo)


def _head_sum(x, ind):
    return _dot(x.astype(BF16), ind)


def _ada_kernel(c_ref, w_ref, b_ref, o_ref):
    c = c_ref[...]
    cond = c * jax.nn.sigmoid(c)
    o_ref[0] = _dot(cond.astype(BF16), w_ref[0].astype(BF16)) + b_ref[0]


def _ada_mod(c, ada_w, ada_b):
    depth, d, n = ada_w.shape
    b = c.shape[0]
    tn = n // 4
    return pl.pallas_call(
        _ada_kernel,
        grid=(depth, n // tn),
        in_specs=[
            pl.BlockSpec((b, d), lambda l, j: (0, 0)),
            pl.BlockSpec((1, d, tn), lambda l, j: (l, 0, j)),
            pl.BlockSpec((1, 1, tn), lambda l, j: (l, 0, j)),
        ],
        out_specs=pl.BlockSpec((1, b, tn), lambda l, j: (l, 0, j)),
        out_shape=jax.ShapeDtypeStruct((depth, b, n), F32),
        compiler_params=_params("arbitrary", "arbitrary"),
        name="ada_mod",
    )(c, ada_w, ada_b.reshape(depth, 1, n))


def _norm_mod(x, gain, mod, sh, sc):
    ms = jnp.mean(x * x, axis=-1, keepdims=True)
    y = x * lax.rsqrt(ms + RMS_EPS) * gain
    return y * (1.0 + mod[sc:sc + 1]) + mod[sh:sh + 1]


def _ffn_kernel(x_ref, mod_ref, gain_ref, wg_ref, wu_ref, wd_ref, fin_ref, *refs, n_in, final):
    y_refs, w_refs, o_ref = refs[:n_in], refs[n_in:2 * n_in], refs[2 * n_in]
    x = x_ref[...]
    mod = mod_ref[0]
    if n_in:
        mix = _dot(y_refs[0][...], w_refs[0][...])
        for y_ref, w_ref in zip(y_refs[1:], w_refs[1:]):
            mix += _dot(y_ref[...], w_ref[...])
        x = x + mod[2:3] * mix
    h = _norm_mod(x, gain_ref[...], mod, 3, 4).astype(BF16)
    gate = _dot(h, wg_ref[...])
    up = _dot(h, wu_ref[...])
    act = (gate * jax.nn.sigmoid(gate) * up).astype(BF16)
    y = x + mod[5:6] * _dot(act, wd_ref[...])
    if final:
        ms = jnp.mean(y * y, axis=-1, keepdims=True)
        y = y * lax.rsqrt(ms + RMS_EPS) * fin_ref[...]
    o_ref[...] = y


def _ffn(x, mod, gain, wg, wu, wd, fin, ys, w_outs, *, seq, tm, final):
    t, d = x.shape
    per_b = seq // tm
    return pl.pallas_call(
        functools.partial(_ffn_kernel, n_in=len(ys), final=final),
        grid=(t // tm,),
        in_specs=[
            pl.BlockSpec((tm, d), lambda i: (i, 0)),
            pl.BlockSpec((1, 6, d), lambda i: (i // per_b, 0, 0)),
            _const_spec((1, d)),
            _const_spec(wg.shape),
            _const_spec(wu.shape),
            _const_spec(wd.shape),
            _const_spec((1, d)),
        ] + [pl.BlockSpec((tm, y.shape[1]), lambda i: (i, 0)) for y in ys]
          + [_const_spec(w.shape) for w in w_outs],
        out_specs=pl.BlockSpec((tm, d), lambda i: (i, 0)),
        out_shape=jax.ShapeDtypeStruct((t, d), F32),
        compiler_params=_params("parallel"),
        name="ffn",
    )(x, mod, gain.reshape(1, d), wg, wu, wd, fin.reshape(1, d), *ys, *w_outs)


def _even_in_kernel(x_ref, mod_ref, gain_ref, wmla_ref, wrwkv_ref, pos_ref, invf_ref, qn_ref, kvn_ref,
                    wqt_ref, wk_ref, wvt_ref, vone_ref, prwkv_ref, qt_ref, k_ref, vt_ref, *, scale, tk):
    h = _norm_mod(x_ref[...], gain_ref[...], mod_ref[0], 0, 1).astype(BF16)
    prwkv_ref[...] = _dot(h, wrwkv_ref[...]).astype(prwkv_ref.dtype)
    p = _dot(h, wmla_ref[...])
    tm = p.shape[0]
    half = MLA_ROPE // 2
    ang = invf_ref[...] * pos_ref[0].astype(F32)
    cos_t, sin_t = jnp.cos(ang), jnp.sin(ang)

    c_q = p[:, :Q_LORA]
    ms = jnp.mean(c_q * c_q, axis=-1, keepdims=True)
    cqn = (c_q * lax.rsqrt(ms + RMS_EPS) * qn_ref[...]).astype(BF16)
    qt = _dot_nt(wqt_ref[...], cqn) * scale
    pieces = []
    for h in range(MLA_HEADS):
        base = h * MLA_HEAD_PAD
        x1 = qt[base + MLA_NOPE:base + MLA_NOPE + half]
        x2 = qt[base + MLA_NOPE + half:base + MLA_NOPE + MLA_ROPE]
        pieces += [qt[base:base + MLA_NOPE], x1 * cos_t - x2 * sin_t, x1 * sin_t + x2 * cos_t,
                   qt[base + MLA_NOPE + MLA_ROPE:base + MLA_HEAD_PAD]]
    qt_ref[0] = jnp.concatenate(pieces, axis=0).astype(qt_ref.dtype)

    ckv = p[:, Q_LORA:Q_LORA + KV_LORA]
    ss = jnp.mean(ckv * ckv, axis=-1, keepdims=True)
    ckvn = ckv * lax.rsqrt(ss + RMS_EPS) * kvn_ref[...]
    kr = p[:, Q_LORA + KV_LORA:]
    rest = kr.shape[1] - MLA_ROPE
    cos_k = jnp.concatenate([cos_t, cos_t, jnp.ones((rest, tm), F32)], axis=0).T
    sin_k = jnp.concatenate([sin_t, sin_t, jnp.zeros((rest, tm), F32)], axis=0).T
    lane = lax.broadcasted_iota(jnp.int32, kr.shape, 1)
    partner = jnp.where(lane < half, -pltpu.roll(kr, kr.shape[1] - half, 1), pltpu.roll(kr, half, 1))
    lhs = jnp.concatenate([ckvn, kr * cos_k + partner * sin_k], axis=1).astype(BF16)
    k_ref[...] = _dot(lhs, wk_ref[...]).astype(k_ref.dtype)
    vt = (_dot_nt(wvt_ref[...], lhs) + vone_ref[...]).astype(vt_ref.dtype)
    for jj in range(tm // tk):
        vt_ref[0, jj] = vt[:, jj * tk:(jj + 1) * tk]


def _even_in(x, mod, gain, w_mla, w_rwkv, pos, invf, q_norm, kv_norm, wqt, wk, wvt, v_ones, *,
             batch, seq, tm, tk):
    t, d = x.shape
    n_q = MLA_HEADS * MLA_HEAD_PAD
    n_v = MLA_HEADS * MLA_VA
    per_b = seq // tm
    scale = (MLA_NOPE + MLA_ROPE) ** -0.5 * LOG2_E
    return pl.pallas_call(
        functools.partial(_even_in_kernel, scale=scale, tk=tk),
        grid=(t // tm,),
        in_specs=[
            pl.BlockSpec((tm, d), lambda i: (i, 0)),
            pl.BlockSpec((1, 6, d), lambda i: (i // per_b, 0, 0)),
            _const_spec((1, d)),
            _const_spec(w_mla.shape),
            _const_spec(w_rwkv.shape),
            pl.BlockSpec((1, 1, tm), lambda i: (i, 0, 0)),
            _const_spec(invf.shape),
            _const_spec((1, Q_LORA)),
            _const_spec((1, KV_LORA)),
            _const_spec(wqt.shape),
            _const_spec(wk.shape),
            _const_spec(wvt.shape),
            _const_spec(v_ones.shape),
        ],
        out_specs=[
            pl.BlockSpec((tm, w_rwkv.shape[1]), lambda i: (i, 0)),
            pl.BlockSpec((1, n_q, tm), lambda i: (i // per_b, 0, i % per_b)),
            pl.BlockSpec((tm, n_q), lambda i: (i, 0)),
            pl.BlockSpec((1, tm // tk, n_v, tk), lambda i: (i // per_b, i % per_b, 0, 0)),
        ],
        out_shape=[
            jax.ShapeDtypeStruct((t, w_rwkv.shape[1]), BF16),
            jax.ShapeDtypeStruct((batch, n_q, seq), BF16),
            jax.ShapeDtypeStruct((t, n_q), BF16),
            jax.ShapeDtypeStruct((batch, seq // tk, n_v, tk), BF16),
        ],
        compiler_params=_params("parallel"),
        name="even_in",
    )(x, mod, gain.reshape(1, d), w_mla, w_rwkv, pos.reshape(t // tm, 1, tm), invf, q_norm, kv_norm,
      wqt, wk, wvt, v_ones)


def _attn_kernel(qt_ref, k_ref, vt_ref, o_ref, s_ref, p_ref, *, tq, nh):
    i = pl.program_id(2)
    qts = [qt_ref[0, h * MLA_HEAD_PAD:(h + 1) * MLA_HEAD_PAD, :] for h in range(nh)]
    causal = (lax.broadcasted_iota(jnp.int32, (tq, tq), 0)
              <= lax.broadcasted_iota(jnp.int32, (tq, tq), 1))

    def put_scores(j, slot):
        start = pl.multiple_of(j * tq, tq)
        for h in range(nh):
            s_ref[slot, h] = _dot(
                k_ref[pl.ds(start, tq), h * MLA_HEAD_PAD:(h + 1) * MLA_HEAD_PAD], qts[h])

    def value_dots(j, probs):
        return [_dot(vt_ref[0, j, h * MLA_VA:(h + 1) * MLA_VA, :], probs[h]) for h in range(nh)]

    def step(carry, slot, pv_prev, masked):
        new, probs = [], []
        for h in range(nh):
            m, acc = carry[h]
            s = s_ref[slot, h]
            if masked:
                s = jnp.where(causal, s, -jnp.inf)
            m_new = jnp.maximum(m, jnp.max(s, axis=0, keepdims=True))
            probs.append(jnp.exp2(s - m_new).astype(BF16))
            new.append((m_new, jnp.exp2(m - m_new) * (acc + pv_prev[h])))
        return tuple(new), probs

    def parked(slot):
        return [p_ref[slot, h] for h in range(nh)]

    def pair(t, carry):
        j = 2 * t
        put_scores(j + 1, 1)
        pv = value_dots(jnp.where(t == 0, i, j - 1), parked(1))
        carry, probs = step(carry, 0, pv, False)
        for h in range(nh):
            p_ref[0, h] = probs[h]
        put_scores(j + 2, 0)
        pv = value_dots(j, parked(0))
        carry, probs = step(carry, 1, pv, False)
        for h in range(nh):
            p_ref[1, h] = probs[h]
        return carry

    def emit(carry, pv_last):
        outs = []
        for h, (_, acc) in enumerate(carry):
            tot = acc + pv_last[h]
            outs.append(tot[:MLA_V] * (1.0 / tot[MLA_V:MLA_V + 1]))
        o_ref[...] = jnp.concatenate(outs, axis=0).T.astype(o_ref.dtype)

    put_scores(i, 1)
    put_scores(0, 0)
    zeros = [jnp.zeros((MLA_VA, tq), F32)] * nh
    init = tuple((jnp.full((1, tq), -jnp.inf, F32), zeros[h]) for h in range(nh))
    carry, probs = step(init, 1, zeros, True)
    for h in range(nh):
        p_ref[1, h] = probs[h]
    carry = lax.fori_loop(0, i // 2, pair, carry)
    n_done = 2 * (i // 2)
    last = jnp.where(n_done == 0, i, n_done - 1)

    @pl.when(i % 2 == 0)
    def _():
        emit(carry, value_dots(last, parked(1)))

    @pl.when(i % 2 == 1)
    def _():
        final, probs = step(carry, 0, value_dots(last, parked(1)), False)
        emit(final, value_dots(i - 1, probs))


def _attention(qt, k, vt, *, batch, seq, tq, nh):
    t = k.shape[0]
    nq = seq // tq
    return pl.pallas_call(
        functools.partial(_attn_kernel, tq=tq, nh=nh),
        grid=(batch, MLA_HEADS // nh, nq),
        in_specs=[
            pl.BlockSpec((1, nh * MLA_HEAD_PAD, tq), lambda b, h, i: (b, h, i)),
            pl.BlockSpec((seq, nh * MLA_HEAD_PAD), lambda b, h, i: (b, h)),
            pl.BlockSpec((1, seq // tq, nh * MLA_VA, tq), lambda b, h, i: (b, 0, h, 0)),
        ],
        out_specs=pl.BlockSpec((tq, nh * MLA_V), lambda b, h, i: (b * nq + i, h)),
        out_shape=jax.ShapeDtypeStruct((t, MLA_HEADS * MLA_V), BF16),
        scratch_shapes=[pltpu.VMEM((2, nh, tq, tq), F32), pltpu.VMEM((2, nh, tq, tq), BF16)],
        compiler_params=_params("parallel", "parallel", "arbitrary"),
        name="mla_attention",
    )(qt, k, vt)


def _stack_heads(x, lane_head):
    return jnp.concatenate(
        [jnp.where(lane_head == h, x, 0.0) for h in range(RWKV_GROUP)], axis=0)


def _rwkv_kernel(p_ref, mu_ref, vec_ref, w2_ref, a2_ref, g2_ref, ind_ref, tri_ref,
                 o_ref, state_ref, prev_ref, *, n_chunks):
    c = RWKV_CHUNK
    gw = RWKV_GROUP * RWKV_HEAD
    gc = RWKV_GROUP * c
    n_groups = RWKV_HEADS // RWKV_GROUP
    tb = n_chunks * c

    @pl.when(pl.program_id(1) == 0)
    def _():
        state_ref[...] = jnp.zeros_like(state_ref)
        prev_ref[...] = jnp.zeros_like(prev_ref)

    p = p_ref[...].astype(F32)
    rows = lax.broadcasted_iota(jnp.int32, p.shape, 0)
    shifted = jnp.where(rows == 0, prev_ref[...], pltpu.roll(p, 1, 0))
    prev_ref[...] = p[tb - 1:tb, :]
    p = p + (shifted - p) * mu_ref[...]

    d = RWKV_DIM
    r, k, v = p[:, :d], p[:, d:2 * d], p[:, 2 * d:3 * d]
    lo = p[:, 3 * d:3 * d + DECAY_LORA + AAA_LORA]
    g_lo = p[:, 3 * d + DECAY_LORA + AAA_LORA:]
    w0, a0, k_k, k_a = vec_ref[0:1], vec_ref[1:2], vec_ref[2:3], vec_ref[3:4]
    r_k, ln_w, ln_b = vec_ref[4:5], vec_ref[5:6], vec_ref[6:7]

    x_w = w0 + _dot(jnp.tanh(lo).astype(BF16), w2_ref[...])
    lw = -EXP_NEG_HALF * jax.nn.sigmoid(x_w)
    a = jax.nn.sigmoid(a0 + _dot(lo.astype(BF16), a2_ref[...]))
    g = _dot(jax.nn.sigmoid(g_lo).astype(BF16), g2_ref[...])

    ind = ind_ref[...]
    kk = k * k_k
    kk = kk * lax.rsqrt(jnp.maximum(_head_sum(kk * kk, ind), 1e-24))
    k2 = k * (1.0 + (a - 1.0) * k_a)
    alpha = kk * a

    b = _dot_split(tri_ref[...], lw)
    b_prev = b - lw
    bcast = lambda row: jnp.broadcast_to(row, (c, d))
    b_mid = jnp.concatenate([bcast(b[ci * c + c // 2 - 1:ci * c + c // 2]) for ci in range(n_chunks)], axis=0)
    b_end = jnp.concatenate([bcast(b[ci * c + c - 1:ci * c + c]) for ci in range(n_chunks)], axis=0)
    kap_t = kk * jnp.exp(b_prev - b_mid)
    r_t = r * jnp.exp(b - b_mid)
    inv_t = jnp.exp(b_mid - b)
    alp_t, k_t = alpha * inv_t, k2 * inv_t
    kap_h = kk * jnp.exp(b_prev)
    r_h = r * jnp.exp(b)
    tail = jnp.exp(b_end - b)
    alp_c, k_c = alpha * tail, k2 * tail
    gam_end = jnp.exp(b_end)

    eye = (lax.broadcasted_iota(jnp.int32, (gc, gc), 0)
           == lax.broadcasted_iota(jnp.int32, (gc, gc), 1)).astype(F32)
    lane_head = lax.broadcasted_iota(jnp.int32, (c, gw), 1) // RWKV_HEAD

    def stacked(x, ci, gi):
        return _stack_heads(x[ci * c:(ci + 1) * c, gi * gw:(gi + 1) * gw], lane_head)

    pairs = [(ci, gi) for ci in range(n_chunks) for gi in range(n_groups)]
    half = gc // 2
    lane_pair = lax.broadcasted_iota(jnp.int32, (c, half), 1) // RWKV_HEAD
    tri_r = lax.broadcasted_iota(jnp.int32, (half, half), 0) % c
    tri_c = lax.broadcasted_iota(jnp.int32, (half, half), 1) % c
    zero_blk = jnp.zeros((half, half), F32)

    def pair_stack(x, ci, pi):
        xs = x[ci * c:(ci + 1) * c, pi * half:(pi + 1) * half]
        return jnp.concatenate([jnp.where(lane_pair == k, xs, 0.0) for k in range(2)], axis=0)

    def block_diag(a, b_):
        return jnp.concatenate([jnp.concatenate([a, zero_blk], axis=1),
                                jnp.concatenate([zero_blk, b_], axis=1)], axis=0)

    a_ka, a_kk, a_ra, a_rk, vs, n_pow, inv = {}, {}, {}, {}, {}, {}, {}
    for pr in pairs:
        ci, gi = pr
        blocks = []
        for pi in (2 * gi, 2 * gi + 1):
            ps = lambda x: pair_stack(x, ci, pi)
            kr = jnp.concatenate([ps(kap_t), ps(r_t)], axis=0).astype(BF16)
            ak = jnp.concatenate([ps(alp_t), ps(k_t)], axis=0).astype(BF16)
            blocks.append(_dot_nt(kr, ak))
        quad = lambda r0, c0, keep: block_diag(
            *[jnp.where(keep, blk[r0:r0 + half, c0:c0 + half], 0.0) for blk in blocks])
        a_ka[pr] = quad(0, 0, tri_c < tri_r)
        a_kk[pr] = quad(0, half, tri_c < tri_r).astype(BF16)
        a_ra[pr] = quad(half, 0, tri_c <= tri_r).astype(BF16)
        a_rk[pr] = quad(half, half, tri_c <= tri_r).astype(BF16)
        vs[pr] = stacked(v, *pr).astype(BF16)
    for pr in pairs:
        n_pow[pr] = a_ka[pr]
        inv[pr] = eye - n_pow[pr]
    def later_rows(x, t0):
        if t0 == 0:
            return x
        return jnp.concatenate([x[h * c + t0:(h + 1) * c] for h in range(RWKV_GROUP)], axis=0)

    def restore_rows(x, t0):
        if t0 == 0:
            return x
        rows = c - t0
        pad = jnp.zeros((t0, x.shape[1]), x.dtype)
        return jnp.concatenate(
            [blk for h in range(RWKV_GROUP) for blk in (pad, x[h * rows:(h + 1) * rows])], axis=0)

    for j in range(1, int(np.log2(c))):
        t0 = (1 << j) // 8 * 8
        for pr in pairs:
            pw = n_pow[pr]
            n_pow[pr] = restore_rows(_dot(later_rows(pw, t0).astype(BF16), pw.astype(BF16)), t0)
        for pr in pairs:
            upd = _dot(later_rows(inv[pr], t0).astype(BF16), n_pow[pr].astype(BF16))
            inv[pr] = inv[pr] + restore_rows(upd, t0)
    akv = {pr: _dot(a_kk[pr], vs[pr]).astype(BF16) for pr in pairs}
    pre = {}
    for pr in pairs:
        st = lambda x: stacked(x, *pr)
        invb = inv[pr].astype(BF16)
        kt = _dot(invb, st(kap_h).astype(BF16)).astype(BF16)
        w_mat = _dot(invb, akv[pr])
        akc = jnp.concatenate([st(alp_c), st(k_c)], axis=0).astype(BF16)
        pre[pr] = (kt, w_mat, a_ra[pr], a_rk[pr], st(r_h).astype(BF16), akc)

    y_rows = []
    for ci in range(n_chunks):
        s0 = [state_ref[gi] for gi in range(n_groups)]
        s0b = [x.astype(BF16) for x in s0]
        ub = [(-(_dot_nt(pre[ci, gi][0], s0b[gi]) + pre[ci, gi][1])).astype(BF16)
              for gi in range(n_groups)]
        ys = []
        for gi in range(n_groups):
            kt, w_mat, a_ra, a_rk, rh, akc = pre[ci, gi]
            uv = jnp.concatenate([ub[gi], vs[ci, gi]], axis=0)
            sl = slice(gi * gw, (gi + 1) * gw)
            state_ref[gi] = s0[gi] * gam_end[ci * c:ci * c + 1, sl] + _dot_tn(uv, akc)
            y_st = _dot_nt(rh, s0b[gi]) + _dot(a_ra, ub[gi]) + _dot(a_rk, vs[ci, gi])
            y = y_st[0:c]
            for h in range(1, RWKV_GROUP):
                y = y + y_st[h * c:(h + 1) * c]
            ys.append(y)
        y_rows.append(jnp.concatenate(ys, axis=1))
    y = jnp.concatenate(y_rows, axis=0)

    inv_n = 1.0 / RWKV_HEAD
    mean = _head_sum(y, ind) * inv_n
    yc = y - mean
    var = _head_sum(yc * yc, ind) * inv_n
    y = yc * lax.rsqrt(var + RWKV_GN_EPS) * ln_w + ln_b
    bonus = _head_sum(r * k2 * r_k, ind) * v
    o_ref[...] = ((y + bonus) * g).astype(o_ref.dtype)


def _rwkv(p_rwkv, mu, vecs, w2, a2, g2, *, batch, seq, n_chunks):
    t = p_rwkv.shape[0]
    c = RWKV_CHUNK
    tb = n_chunks * c
    nb = seq // tb
    gw = RWKV_GROUP * RWKV_HEAD
    head = np.arange(RWKV_DIM) // RWKV_HEAD
    ind = jnp.asarray(head[:, None] == head[None, :], BF16)
    tt = np.arange(tb)
    tri = jnp.asarray((tt[None, :] <= tt[:, None]) & (tt[None, :] // c == tt[:, None] // c), BF16)
    return pl.pallas_call(
        functools.partial(_rwkv_kernel, n_chunks=n_chunks),
        grid=(batch, nb),
        in_specs=[
            pl.BlockSpec((tb, RWKV_COLS), lambda b, j: (b * nb + j, 0)),
            _const_spec((1, RWKV_COLS)),
            _const_spec(vecs.shape),
            _const_spec(w2.shape),
            _const_spec(a2.shape),
            _const_spec(g2.shape),
            _const_spec(ind.shape),
            _const_spec(tri.shape),
        ],
        out_specs=pl.BlockSpec((tb, RWKV_DIM), lambda b, j: (b * nb + j, 0)),
        out_shape=jax.ShapeDtypeStruct((t, RWKV_DIM), BF16),
        scratch_shapes=[
            pltpu.VMEM((RWKV_HEADS // RWKV_GROUP, gw, gw), F32),
            pltpu.VMEM((1, RWKV_COLS), F32),
        ],
        compiler_params=_params("parallel", "arbitrary"),
        name="rwkv7",
    )(p_rwkv, mu, vecs, w2, a2, g2, ind, tri)


def _hgrn_masks():
    c = HG_CHUNK
    t = np.arange(c)
    mask = np.zeros((HG_LEVELS + 1, c, c), np.float32)
    mask[0] = np.eye(c)
    for l in range(1, HG_LEVELS + 1):
        m = 1 << l
        is_right = (t % m) >= m // 2
        same = (t[:, None] // m) == (t[None, :] // m)
        mask[l] = same & is_right[:, None] & ~is_right[None, :]
    return mask


def _hgrn_chunk(proj, lb, gain, tri, mask_ref, state_ref, side_jobs):
    c = HG_CHUNK
    jobs = iter(side_jobs)
    run_job = lambda: next(jobs, lambda: None)()
    q, f, vals, g = proj
    n = q.shape[1]
    q = q * jax.nn.sigmoid(q)
    forget = lb + (1.0 - lb) * jax.nn.sigmoid(f)
    key = 1.0 - forget
    logf = jnp.log(forget)

    b = _dot_split(tri, logf)
    e_cum = jnp.exp(b)
    b_end = b[c - 1:c, :]
    q_hat = (q * e_cum).astype(BF16)
    k_hat = (key * jnp.exp(b_end - b)).astype(BF16)
    decay_end = e_cum[c - 1:c, :]
    qb, kb = q.astype(BF16), key.astype(BF16)
    run_job()

    t_idx = lax.broadcasted_iota(jnp.int32, q.shape, 0)
    f_prev = pltpu.roll(forget, 1, 0)
    f_next = pltpu.roll(forget, c - 1, 0)
    level_ops = []
    for l in range(HG_LEVELS):
        m = 2 << l
        is_right = (t_idx & (m // 2)) != 0
        if m == 2:
            decay = jnp.where(is_right, forget, 1.0)
        elif m == 4:
            pos = t_idx & 3
            decay = jnp.where(pos == 0, f_next,
                              jnp.where(pos == 1, 1.0, jnp.where(pos == 2, forget, forget * f_prev)))
        else:
            b3 = b.reshape(c // m, m, n)
            b_ref = jnp.broadcast_to(b3[:, m // 2 - 1:m // 2, :], (c // m, m, n)).reshape(c, n)
            decay = jnp.exp(-jnp.abs(b - b_ref))
        level_ops.append((jnp.where(is_right, q, key) * decay).astype(BF16))
        run_job()

    heads = [slice(h * HG_K, (h + 1) * HG_K) for h in range(HG_HEADS)]
    attns = []
    for sl in heads:
        attn = mask_ref[0] * _dot_nt(qb[:, sl], kb[:, sl])
        for l in range(HG_LEVELS):
            ml = level_ops[l][:, sl]
            attn = attn + mask_ref[l + 1] * _dot_nt(ml, ml)
        attns.append(attn.astype(BF16))
        run_job()
    vhs = [vals[:, sl].astype(BF16) for sl in heads]
    sts = [state_ref[h] for h in range(HG_HEADS)]
    os = [_dot_nt(q_hat[:, sl], sts[h].astype(BF16)) + _dot(attns[h], vhs[h])
          for h, sl in enumerate(heads)]
    for h, sl in enumerate(heads):
        state_ref[h] = sts[h] * decay_end[:, sl] + _dot_tn(vhs[h], k_hat[:, sl])
    outs = []
    for h, sl in enumerate(heads):
        o = os[h]
        ms = jnp.mean(o * o, axis=-1, keepdims=True)
        gt = g[:, sl]
        outs.append((o * lax.rsqrt(ms + RMS_EPS) * gain * (gt * jax.nn.sigmoid(gt))).astype(BF16))
    for job in jobs:
        job()
    return jnp.concatenate(outs, axis=1)


def _hgrn_layer_kernel(x_ref, xn_ref, mod_ref, ngain_ref, win_ref, lbl_ref, gain_ref, tri_ref,
                       mask_ref, wout_ref, o_ref, state_ref, proj_ref, *, layer):
    c = HG_CHUNK
    mod = mod_ref[0]
    per_proj = proj_ref.shape[3] // HG_PROJ_PIECE

    def projection_jobs(x, slot):
        hin = _norm_mod(x, ngain_ref[...], mod, 0, 1).astype(BF16)

        def piece(k, p):
            def job():
                lo = p * HG_PROJ_PIECE
                proj_ref[slot, k, :, lo:lo + HG_PROJ_PIECE] = _dot(hin, win_ref[k * per_proj + p])
            return job

        return [piece(k, p) for k in range(4) for p in range(per_proj)]

    @pl.when(pl.program_id(1) == 0)
    def _():
        state_ref[...] = jnp.zeros_like(state_ref)
        for job in projection_jobs(x_ref[0:c, :], 0):
            job()

    logits = lbl_ref[...]
    pe = jnp.exp(logits - jnp.max(logits, axis=0, keepdims=True))
    lb = jnp.sum(pe[1:layer + 1], axis=0, keepdims=True) / jnp.sum(pe, axis=0, keepdims=True)

    def recur(slot, side_jobs):
        proj = tuple(proj_ref[slot, k] for k in range(4))
        return _hgrn_chunk(proj, lb, gain_ref[...], tri_ref[...], mask_ref, state_ref, side_jobs)

    y0 = recur(0, projection_jobs(x_ref[c:2 * c, :], 1))
    y1 = recur(1, projection_jobs(xn_ref[...], 0))
    y = jnp.concatenate([y0, y1], axis=0)
    o_ref[...] = x_ref[...] + mod[2:3] * _dot(y, wout_ref[...])


def _hgrn_layer(x, mod, norm_gain, w_in, lb_logits, gain, w_out, *, batch, seq, layer):
    t, d = x.shape
    c = HG_CHUNK
    n = HG_HEADS * HG_K
    nb = seq // (2 * c)
    tri = jnp.asarray(np.tril(np.ones((c, c))), BF16)
    mask = jnp.asarray(_hgrn_masks(), F32)
    w_pieces = w_in.reshape(d, 4 * n // HG_PROJ_PIECE, HG_PROJ_PIECE).transpose(1, 0, 2)
    next_chunk = lambda b, j: (b * 2 * nb + jnp.minimum(2 * j + 2, 2 * nb - 1), 0)
    return pl.pallas_call(
        functools.partial(_hgrn_layer_kernel, layer=layer),
        grid=(batch, nb),
        in_specs=[pl.BlockSpec((2 * c, d), lambda b, j: (b * nb + j, 0)),
                  pl.BlockSpec((c, d), next_chunk),
                  pl.BlockSpec((1, 6, d), lambda b, j: (b, 0, 0)),
                  _const_spec((1, d)), _const_spec(w_pieces.shape),
                  _const_spec(lb_logits.shape), _const_spec((1, HG_V)),
                  _const_spec(tri.shape), _const_spec(mask.shape), _const_spec(w_out.shape)],
        out_specs=pl.BlockSpec((2 * c, d), lambda b, j: (b * nb + j, 0)),
        out_shape=jax.ShapeDtypeStruct((t, d), F32),
        scratch_shapes=[pltpu.VMEM((HG_HEADS, HG_V, HG_K), F32),
                        pltpu.VMEM((2, 4, c, n), F32)],
        compiler_params=_params("parallel", "arbitrary"),
        name="hgrn2_layer",
    )(x, x, mod, norm_gain.reshape(1, d), w_pieces, lb_logits, gain, tri, mask, w_out)


def _pad_cols(w, n):
    return jnp.pad(w, ((0, 0), (0, n - w.shape[1])))


def _mla_weights(w_uq, w_ukv):
    hq = MLA_NOPE + MLA_ROPE
    wq = w_uq.reshape(Q_LORA, MLA_HEADS, hq)
    wq = jnp.pad(wq, ((0, 0), (0, 0), (0, MLA_HEAD_PAD - hq))).reshape(Q_LORA, -1)
    wkv = w_ukv.reshape(KV_LORA, MLA_HEADS, MLA_NOPE + MLA_V)
    wk = jnp.pad(wkv[:, :, :MLA_NOPE], ((0, 0), (0, 0), (0, MLA_HEAD_PAD - MLA_NOPE)))
    wk = wk.reshape(KV_LORA, -1)
    n_extra = MLA_COLS_PAD - Q_LORA - KV_LORA
    place = np.zeros((n_extra, MLA_HEADS, MLA_HEAD_PAD), np.float32)
    for r in range(MLA_ROPE):
        place[r, :, MLA_NOPE + r] = 1.0
    wk_full = jnp.concatenate([wk, jnp.asarray(place.reshape(n_extra, -1))], axis=0)
    wv = jnp.pad(wkv[:, :, MLA_NOPE:], ((0, 0), (0, 0), (0, MLA_VA - MLA_V))).reshape(KV_LORA, -1)
    wv_full = jnp.concatenate([wv, jnp.zeros((n_extra, wv.shape[1]), F32)], axis=0)
    v_ones = np.zeros((MLA_HEADS, MLA_VA, 1), np.float32)
    v_ones[:, MLA_V] = 1.0
    return (wq.T.astype(BF16), wk_full.astype(BF16), wv_full.T.astype(BF16),
            jnp.asarray(v_ones.reshape(-1, 1)))


def _rope_inv_freq():
    inv = 1.0 / (ROPE_BASE ** (jnp.arange(0, MLA_ROPE, 2, dtype=F32) / MLA_ROPE))
    return inv.reshape(MLA_ROPE // 2, 1)


def kernel(x, c, positions, ada_w, ada_b, norm_mix, norm_ffn, w_in_even, mla_q_norm, mla_w_uq, mla_kv_norm, mla_w_ukv, rwkv_mu, rwkv_w0, rwkv_w2, rwkv_a0, rwkv_a2, rwkv_g2, rwkv_k_k, rwkv_k_a, rwkv_r_k, rwkv_ln_w, rwkv_ln_b, w_out_even, w_in_odd, hg_lb_logits, hg_out_norm, w_out_odd, ffn_w_gate, ffn_w_up, ffn_w_down, final_norm):
    batch, seq, d = x.shape
    depth = ada_w.shape[0]
    t = batch * seq
    tm = min(512, seq)
    xt = x.reshape(t, d)

    mod_all = _ada_mod(c, ada_w, ada_b).reshape(depth, batch, 6, d)

    for l in range(depth):
        mod = mod_all[l]
        j = l // 2
        if l % 2 == 0:
            w_in = w_in_even[j]
            w_mla = _pad_cols(w_in[:, :MLA_COLS], MLA_COLS_PAD).astype(BF16)
            w_rwkv = w_in[:, MLA_COLS:].astype(BF16)
            wqt, wk, wvt, v_ones = _mla_weights(mla_w_uq[j], mla_w_ukv[j])
            p_rwkv, qt, kh, vt = _even_in(
                xt, mod, norm_mix[l], w_mla, w_rwkv, positions, _rope_inv_freq(),
                mla_q_norm[j].reshape(1, -1), mla_kv_norm[j].reshape(1, -1), wqt, wk, wvt, v_ones,
                batch=batch, seq=seq, tm=tm, tk=ATTN_TILE)
            y_a = _attention(qt, kh, vt, batch=batch, seq=seq, tq=ATTN_TILE, nh=ATTN_HEADS)

            zeros_lo = jnp.zeros((DECAY_LORA, RWKV_DIM), F32)
            w2 = jnp.concatenate([rwkv_w2[j], zeros_lo], axis=0).astype(BF16)
            a2 = jnp.concatenate([zeros_lo, rwkv_a2[j]], axis=0).astype(BF16)
            vecs = jnp.stack([rwkv_w0[j], rwkv_a0[j], rwkv_k_k[j], rwkv_k_a[j],
                              rwkv_r_k[j].reshape(-1), rwkv_ln_w[j], rwkv_ln_b[j],
                              jnp.zeros((RWKV_DIM,), F32)])
            y_b = _rwkv(p_rwkv, rwkv_mu[j].reshape(1, -1), vecs, w2, a2,
                        rwkv_g2[j].astype(BF16), batch=batch, seq=seq,
                        n_chunks=min(RWKV_BLOCK_CHUNKS, seq // RWKV_CHUNK))
            w_out = w_out_even[j].astype(BF16)
            n_a = MLA_HEADS * MLA_V
            ys, w_outs = [y_a, y_b], [w_out[:n_a], w_out[n_a:]]
        else:
            xt = _hgrn_layer(xt, mod, norm_mix[l], w_in_odd[j].astype(BF16), hg_lb_logits,
                             hg_out_norm[j].reshape(1, -1), w_out_odd[j].astype(BF16),
                             batch=batch, seq=seq, layer=l)
            ys, w_outs = [], []
        xt = _ffn(xt, mod, norm_ffn[l], ffn_w_gate[l].astype(BF16), ffn_w_up[l].astype(BF16),
                  ffn_w_down[l].astype(BF16), final_norm, ys, w_outs, seq=seq, tm=tm,
                  final=(l == depth - 1))
    return xt.reshape(batch, seq, d)
```

```python
import functools

import numpy as np
import jax
import jax.numpy as jnp
from jax import lax
from jax.experimental import pallas as pl
from jax.experimental.pallas import tpu as pltpu

F32 = jnp.float32
BF16 = jnp.bfloat16

RMS_EPS = 1e-6
LOG2_E = 1.4426950408889634
EXP_NEG_HALF = 0.6065306597126334
LANES = 128
VMEM_LIMIT = 56 * 1024 * 1024

MLA_HEADS = 8
MLA_NOPE = 64
MLA_ROPE = 32
MLA_V = 64
MLA_VA = 80
Q_LORA = 384
KV_LORA = 256
ROPE_BASE = 10000.0
MLA_COLS = Q_LORA + KV_LORA + MLA_ROPE
MLA_COLS_PAD = 768
MLA_HEAD_PAD = 128
ATTN_TILE = 512
ATTN_HEADS = 8

RWKV_HEAD = 64
RWKV_DIM = 512
RWKV_HEADS = 8
DECAY_LORA = 64
AAA_LORA = 64
GATE_LORA = 128
RWKV_GN_EPS = RWKV_HEAD * 1e-5
RWKV_COLS = 3 * RWKV_DIM + DECAY_LORA + AAA_LORA + GATE_LORA
RWKV_CHUNK = 64
RWKV_BLOCK_CHUNKS = 4
RWKV_GROUP = 4

HG_K = 128
HG_HEADS = 8
HG_V = 128
HG_CHUNK = 128
HG_LEVELS = 7
HG_PROJ_PIECE = 256


def _params(*sem):
    return pltpu.CompilerParams(dimension_semantics=sem, vmem_limit_bytes=VMEM_LIMIT)


def _const_spec(shape):
    nd = len(shape)
    return pl.BlockSpec(shape, lambda *_: (0,) * nd, pipeline_mode=pl.Buffered(1))


def _dot(a, b):
    return jnp.dot(a, b, preferred_element_type=F32)


def _dot_nt(a, b):
    return lax.dot_general(a, b, (((1,), (1,)), ((), ())), preferred_element_type=F32)


def _dot_tn(a, b):
    return lax.dot_general(a, b, (((0,), (0,)), ((), ())), preferred_element_type=F32)


def _dot_split(sel, x):
    hi = x.astype(BF16)
    lo = (x - hi.astype(F32)).astype(BF16)
    return _dot(sel, hi) + _dot(sel, lo)


def _head_sum(x, ind):
    return _dot(x.astype(BF16), ind)


def _ada_kernel(c_ref, w_ref, b_ref, o_ref):
    c = c_ref[...]
    cond = c * jax.nn.sigmoid(c)
    o_ref[0] = _dot(cond.astype(BF16), w_ref[0].astype(BF16)) + b_ref[0]


def _ada_mod(c, ada_w, ada_b):
    depth, d, n = ada_w.shape
    b = c.shape[0]
    tn = n // 4
    return pl.pallas_call(
        _ada_kernel,
        grid=(depth, n // tn),
        in_specs=[
            pl.BlockSpec((b, d), lambda l, j: (0, 0)),
            pl.BlockSpec((1, d, tn), lambda l, j: (l, 0, j)),
            pl.BlockSpec((1, 1, tn), lambda l, j: (l, 0, j)),
        ],
        out_specs=pl.BlockSpec((1, b, tn), lambda l, j: (l, 0, j)),
        out_shape=jax.ShapeDtypeStruct((depth, b, n), F32),
        compiler_params=_params("arbitrary", "arbitrary"),
        name="ada_mod",
    )(c, ada_w, ada_b.reshape(depth, 1, n))


def _norm_mod(x, gain, mod, sh, sc):
    ms = jnp.mean(x * x, axis=-1, keepdims=True)
    y = x * lax.rsqrt(ms + RMS_EPS) * gain
    return y * (1.0 + mod[sc:sc + 1]) + mod[sh:sh + 1]


def _in_proj_kernel(x_ref, mod_ref, gain_ref, *refs, n_out, sh, sc):
    w_refs, o_refs = refs[:n_out], refs[n_out:]
    h = _norm_mod(x_ref[...], gain_ref[...], mod_ref[0], sh, sc).astype(BF16)
    for w_ref, o_ref in zip(w_refs, o_refs):
        o_ref[...] = _dot(h, w_ref[...]).astype(o_ref.dtype)


def _in_proj(x, mod, gain, weights, *, seq, sh, sc, tm, out_dtype):
    t, d = x.shape
    per_b = seq // tm
    n_out = len(weights)
    in_specs = [
        pl.BlockSpec((tm, d), lambda i: (i, 0)),
        pl.BlockSpec((1, 6, d), lambda i: (i // per_b, 0, 0)),
        _const_spec((1, d)),
    ] + [_const_spec(w.shape) for w in weights]
    out_specs = [pl.BlockSpec((tm, w.shape[1]), lambda i: (i, 0)) for w in weights]
    out_shape = [jax.ShapeDtypeStruct((t, w.shape[1]), out_dtype) for w in weights]
    return pl.pallas_call(
        functools.partial(_in_proj_kernel, n_out=n_out, sh=sh, sc=sc),
        grid=(t // tm,),
        in_specs=in_specs,
        out_specs=out_specs,
        out_shape=out_shape,
        compiler_params=_params("parallel"),
        name="in_proj",
    )(x, mod, gain.reshape(1, d), *weights)


def _ffn_kernel(x_ref, mod_ref, gain_ref, wg_ref, wu_ref, wd_ref, fin_ref, *refs, n_in, final):
    y_refs, w_refs, o_ref = refs[:n_in], refs[n_in:2 * n_in], refs[2 * n_in]
    x = x_ref[...]
    mod = mod_ref[0]
    if n_in:
        mix = _dot(y_refs[0][...], w_refs[0][...])
        for y_ref, w_ref in zip(y_refs[1:], w_refs[1:]):
            mix += _dot(y_ref[...], w_ref[...])
        x = x + mod[2:3] * mix
    h = _norm_mod(x, gain_ref[...], mod, 3, 4).astype(BF16)
    gate = _dot(h, wg_ref[...])
    up = _dot(h, wu_ref[...])
    act = (gate * jax.nn.sigmoid(gate) * up).astype(BF16)
    y = x + mod[5:6] * _dot(act, wd_ref[...])
    if final:
        ms = jnp.mean(y * y, axis=-1, keepdims=True)
        y = y * lax.rsqrt(ms + RMS_EPS) * fin_ref[...]
    o_ref[...] = y


def _ffn(x, mod, gain, wg, wu, wd, fin, ys, w_outs, *, seq, tm, final):
    t, d = x.shape
    per_b = seq // tm
    return pl.pallas_call(
        functools.partial(_ffn_kernel, n_in=len(ys), final=final),
        grid=(t // tm,),
        in_specs=[
            pl.BlockSpec((tm, d), lambda i: (i, 0)),
            pl.BlockSpec((1, 6, d), lambda i: (i // per_b, 0, 0)),
            _const_spec((1, d)),
            _const_spec(wg.shape),
            _const_spec(wu.shape),
            _const_spec(wd.shape),
            _const_spec((1, d)),
        ] + [pl.BlockSpec((tm, y.shape[1]), lambda i: (i, 0)) for y in ys]
          + [_const_spec(w.shape) for w in w_outs],
        out_specs=pl.BlockSpec((tm, d), lambda i: (i, 0)),
        out_shape=jax.ShapeDtypeStruct((t, d), F32),
        compiler_params=_params("parallel"),
        name="ffn",
    )(x, mod, gain.reshape(1, d), wg, wu, wd, fin.reshape(1, d), *ys, *w_outs)


def _mla_prep_kernel(p_ref, pos_ref, invf_ref, qn_ref, kvn_ref, wqt_ref, wk_ref, wvt_ref, vone_ref,
                     qt_ref, k_ref, vt_ref, *, scale, tk):
    p = p_ref[...].astype(F32)
    tm = p.shape[0]
    half = MLA_ROPE // 2
    ang = invf_ref[...] * pos_ref[0].astype(F32)
    cos_t, sin_t = jnp.cos(ang), jnp.sin(ang)

    c_q = p[:, :Q_LORA]
    ms = jnp.mean(c_q * c_q, axis=-1, keepdims=True)
    cqn = (c_q * lax.rsqrt(ms + RMS_EPS) * qn_ref[...]).astype(BF16)
    qt = _dot_nt(wqt_ref[...], cqn) * scale
    pieces = []
    for h in range(MLA_HEADS):
        base = h * MLA_HEAD_PAD
        x1 = qt[base + MLA_NOPE:base + MLA_NOPE + half]
        x2 = qt[base + MLA_NOPE + half:base + MLA_NOPE + MLA_ROPE]
        pieces += [qt[base:base + MLA_NOPE], x1 * cos_t - x2 * sin_t, x1 * sin_t + x2 * cos_t,
                   qt[base + MLA_NOPE + MLA_ROPE:base + MLA_HEAD_PAD]]
    qt_ref[0] = jnp.concatenate(pieces, axis=0).astype(qt_ref.dtype)

    ckv = p[:, Q_LORA:Q_LORA + KV_LORA]
    ss = jnp.mean(ckv * ckv, axis=-1, keepdims=True)
    ckvn = ckv * lax.rsqrt(ss + RMS_EPS) * kvn_ref[...]
    kr = p[:, Q_LORA + KV_LORA:]
    rest = kr.shape[1] - MLA_ROPE
    cos_k = jnp.concatenate([cos_t, cos_t, jnp.ones((rest, tm), F32)], axis=0).T
    sin_k = jnp.concatenate([sin_t, sin_t, jnp.zeros((rest, tm), F32)], axis=0).T
    lane = lax.broadcasted_iota(jnp.int32, kr.shape, 1)
    partner = jnp.where(lane < half, -pltpu.roll(kr, kr.shape[1] - half, 1), pltpu.roll(kr, half, 1))
    lhs = jnp.concatenate([ckvn, kr * cos_k + partner * sin_k], axis=1).astype(BF16)
    k_ref[...] = _dot(lhs, wk_ref[...]).astype(k_ref.dtype)
    vt = (_dot_nt(wvt_ref[...], lhs) + vone_ref[...]).astype(vt_ref.dtype)
    for jj in range(tm // tk):
        vt_ref[0, jj] = vt[:, jj * tk:(jj + 1) * tk]


def _mla_prep(p_mla, pos, invf, q_norm, kv_norm, wqt, wk, wvt, v_ones, *, batch, seq, tm, tk):
    t = p_mla.shape[0]
    n_q = MLA_HEADS * MLA_HEAD_PAD
    n_v = MLA_HEADS * MLA_VA
    per_b = seq // tm
    scale = (MLA_NOPE + MLA_ROPE) ** -0.5 * LOG2_E
    return pl.pallas_call(
        functools.partial(_mla_prep_kernel, scale=scale, tk=tk),
        grid=(t // tm,),
        in_specs=[
            pl.BlockSpec((tm, MLA_COLS_PAD), lambda i: (i, 0)),
            pl.BlockSpec((1, 1, tm), lambda i: (i, 0, 0)),
            _const_spec(invf.shape),
            _const_spec((1, Q_LORA)),
            _const_spec((1, KV_LORA)),
            _const_spec(wqt.shape),
            _const_spec(wk.shape),
            _const_spec(wvt.shape),
            _const_spec(v_ones.shape),
        ],
        out_specs=[
            pl.BlockSpec((1, n_q, tm), lambda i: (i // per_b, 0, i % per_b)),
            pl.BlockSpec((tm, n_q), lambda i: (i, 0)),
            pl.BlockSpec((1, tm // tk, n_v, tk), lambda i: (i // per_b, i % per_b, 0, 0)),
        ],
        out_shape=[
            jax.ShapeDtypeStruct((batch, n_q, seq), BF16),
            jax.ShapeDtypeStruct((t, n_q), BF16),
            jax.ShapeDtypeStruct((batch, seq // tk, n_v, tk), BF16),
        ],
        compiler_params=_params("parallel"),
        name="mla_prep",
    )(p_mla, pos.reshape(t // tm, 1, tm), invf, q_norm, kv_norm, wqt, wk, wvt, v_ones)


def _attn_kernel(qt_ref, k_ref, vt_ref, o_ref, s_ref, p_ref, *, tq, nh):
    i = pl.program_id(2)
    qts = [qt_ref[0, h * MLA_HEAD_PAD:(h + 1) * MLA_HEAD_PAD, :] for h in range(nh)]
    causal = (lax.broadcasted_iota(jnp.int32, (tq, tq), 0)
              <= lax.broadcasted_iota(jnp.int32, (tq, tq), 1))

    def put_scores(j, slot):
        start = pl.multiple_of(j * tq, tq)
        for h in range(nh):
            s_ref[slot, h] = _dot(
                k_ref[pl.ds(start, tq), h * MLA_HEAD_PAD:(h + 1) * MLA_HEAD_PAD], qts[h])

    def value_dots(j, probs):
        return [_dot(vt_ref[0, j, h * MLA_VA:(h + 1) * MLA_VA, :], probs[h]) for h in range(nh)]

    def step(carry, slot, pv_prev, masked):
        new, probs = [], []
        for h in range(nh):
            m, acc = carry[h]
            s = s_ref[slot, h]
            if masked:
                s = jnp.where(causal, s, -jnp.inf)
            m_new = jnp.maximum(m, jnp.max(s, axis=0, keepdims=True))
            probs.append(jnp.exp2(s - m_new).astype(BF16))
            new.append((m_new, jnp.exp2(m - m_new) * (acc + pv_prev[h])))
        return tuple(new), probs

    def parked(slot):
        return [p_ref[slot, h] for h in range(nh)]

    def pair(t, carry):
        j = 2 * t
        put_scores(j + 1, 1)
        pv = value_dots(jnp.where(t == 0, i, j - 1), parked(1))
        carry, probs = step(carry, 0, pv, False)
        for h in range(nh):
            p_ref[0, h] = probs[h]
        put_scores(j + 2, 0)
        pv = value_dots(j, parked(0))
        carry, probs = step(carry, 1, pv, False)
        for h in range(nh):
            p_ref[1, h] = probs[h]
        return carry

    def emit(carry, pv_last):
        outs = []
        for h, (_, acc) in enumerate(carry):
            tot = acc + pv_last[h]
            outs.append(tot[:MLA_V] * (1.0 / tot[MLA_V:MLA_V + 1]))
        o_ref[...] = jnp.concatenate(outs, axis=0).T.astype(o_ref.dtype)

    put_scores(i, 1)
    put_scores(0, 0)
    zeros = [jnp.zeros((MLA_VA, tq), F32)] * nh
    init = tuple((jnp.full((1, tq), -jnp.inf, F32), zeros[h]) for h in range(nh))
    carry, probs = step(init, 1, zeros, True)
    for h in range(nh):
        p_ref[1, h] = probs[h]
    carry = lax.fori_loop(0, i // 2, pair, carry)
    n_done = 2 * (i // 2)
    last = jnp.where(n_done == 0, i, n_done - 1)

    @pl.when(i % 2 == 0)
    def _():
        emit(carry, value_dots(last, parked(1)))

    @pl.when(i % 2 == 1)
    def _():
        final, probs = step(carry, 0, value_dots(last, parked(1)), False)
        emit(final, value_dots(i - 1, probs))


def _attention(qt, k, vt, *, batch, seq, tq, nh):
    t = k.shape[0]
    nq = seq // tq
    return pl.pallas_call(
        functools.partial(_attn_kernel, tq=tq, nh=nh),
        grid=(batch, MLA_HEADS // nh, nq),
        in_specs=[
            pl.BlockSpec((1, nh * MLA_HEAD_PAD, tq), lambda b, h, i: (b, h, i)),
            pl.BlockSpec((seq, nh * MLA_HEAD_PAD), lambda b, h, i: (b, h)),
            pl.BlockSpec((1, seq // tq, nh * MLA_VA, tq), lambda b, h, i: (b, 0, h, 0)),
        ],
        out_specs=pl.BlockSpec((tq, nh * MLA_V), lambda b, h, i: (b * nq + i, h)),
        out_shape=jax.ShapeDtypeStruct((t, MLA_HEADS * MLA_V), BF16),
        scratch_shapes=[pltpu.VMEM((2, nh, tq, tq), F32), pltpu.VMEM((2, nh, tq, tq), BF16)],
        compiler_params=_params("parallel", "parallel", "arbitrary"),
        name="mla_attention",
    )(qt, k, vt)


def _stack_heads(x, lane_head):
    return jnp.concatenate(
        [jnp.where(lane_head == h, x, 0.0) for h in range(RWKV_GROUP)], axis=0)


def _rwkv_kernel(p_ref, mu_ref, vec_ref, w2_ref, a2_ref, g2_ref, ind_ref, tri_ref,
                 o_ref, state_ref, prev_ref, *, n_chunks):
    c = RWKV_CHUNK
    gw = RWKV_GROUP * RWKV_HEAD
    gc = RWKV_GROUP * c
    n_groups = RWKV_HEADS // RWKV_GROUP
    tb = n_chunks * c

    @pl.when(pl.program_id(1) == 0)
    def _():
        state_ref[...] = jnp.zeros_like(state_ref)
        prev_ref[...] = jnp.zeros_like(prev_ref)

    p = p_ref[...].astype(F32)
    rows = lax.broadcasted_iota(jnp.int32, p.shape, 0)
    shifted = jnp.where(rows == 0, prev_ref[...], pltpu.roll(p, 1, 0))
    prev_ref[...] = p[tb - 1:tb, :]
    p = p + (shifted - p) * mu_ref[...]

    d = RWKV_DIM
    r, k, v = p[:, :d], p[:, d:2 * d], p[:, 2 * d:3 * d]
    lo = p[:, 3 * d:3 * d + DECAY_LORA + AAA_LORA]
    g_lo = p[:, 3 * d + DECAY_LORA + AAA_LORA:]
    w0, a0, k_k, k_a = vec_ref[0:1], vec_ref[1:2], vec_ref[2:3], vec_ref[3:4]
    r_k, ln_w, ln_b = vec_ref[4:5], vec_ref[5:6], vec_ref[6:7]

    x_w = w0 + _dot(jnp.tanh(lo).astype(BF16), w2_ref[...])
    lw = -EXP_NEG_HALF * jax.nn.sigmoid(x_w)
    a = jax.nn.sigmoid(a0 + _dot(lo.astype(BF16), a2_ref[...]))
    g = _dot(jax.nn.sigmoid(g_lo).astype(BF16), g2_ref[...])

    ind = ind_ref[...]
    kk = k * k_k
    kk = kk * lax.rsqrt(jnp.maximum(_head_sum(kk * kk, ind), 1e-24))
    k2 = k * (1.0 + (a - 1.0) * k_a)
    alpha = kk * a

    b = _dot_split(tri_ref[...], lw)
    b_prev = b - lw
    bcast = lambda row: jnp.broadcast_to(row, (c, d))
    b_mid = jnp.concatenate([bcast(b[ci * c + c // 2 - 1:ci * c + c // 2]) for ci in range(n_chunks)], axis=0)
    b_end = jnp.concatenate([bcast(b[ci * c + c - 1:ci * c + c]) for ci in range(n_chunks)], axis=0)
    kap_t = kk * jnp.exp(b_prev - b_mid)
    r_t = r * jnp.exp(b - b_mid)
    inv_t = jnp.exp(b_mid - b)
    alp_t, k_t = alpha * inv_t, k2 * inv_t
    kap_h = kk * jnp.exp(b_prev)
    r_h = r * jnp.exp(b)
    tail = jnp.exp(b_end - b)
    alp_c, k_c = alpha * tail, k2 * tail
    gam_end = jnp.exp(b_end)

    eye = (lax.broadcasted_iota(jnp.int32, (gc, gc), 0)
           == lax.broadcasted_iota(jnp.int32, (gc, gc), 1)).astype(F32)
    lane_head = lax.broadcasted_iota(jnp.int32, (c, gw), 1) // RWKV_HEAD

    def stacked(x, ci, gi):
        return _stack_heads(x[ci * c:(ci + 1) * c, gi * gw:(gi + 1) * gw], lane_head)

    pairs = [(ci, gi) for ci in range(n_chunks) for gi in range(n_groups)]
    half = gc // 2
    lane_pair = lax.broadcasted_iota(jnp.int32, (c, half), 1) // RWKV_HEAD
    tri_r = lax.broadcasted_iota(jnp.int32, (half, half), 0) % c
    tri_c = lax.broadcasted_iota(jnp.int32, (half, half), 1) % c
    zero_blk = jnp.zeros((half, half), F32)

    def pair_stack(x, ci, pi):
        xs = x[ci * c:(ci + 1) * c, pi * half:(pi + 1) * half]
        return jnp.concatenate([jnp.where(lane_pair == k, xs, 0.0) for k in range(2)], axis=0)

    def block_diag(a, b_):
        return jnp.concatenate([jnp.concatenate([a, zero_blk], axis=1),
                                jnp.concatenate([zero_blk, b_], axis=1)], axis=0)

    a_ka, a_kk, a_ra, a_rk, vs, n_pow, inv = {}, {}, {}, {}, {}, {}, {}
    for pr in pairs:
        ci, gi = pr
        blocks = []
        for pi in (2 * gi, 2 * gi + 1):
            ps = lambda x: pair_stack(x, ci, pi)
            kr = jnp.concatenate([ps(kap_t), ps(r_t)], axis=0).astype(BF16)
            ak = jnp.concatenate([ps(alp_t), ps(k_t)], axis=0).astype(BF16)
            blocks.append(_dot_nt(kr, ak))
        quad = lambda r0, c0, keep: block_diag(
            *[jnp.where(keep, blk[r0:r0 + half, c0:c0 + half], 0.0) for blk in blocks])
        a_ka[pr] = quad(0, 0, tri_c < tri_r)
        a_kk[pr] = quad(0, half, tri_c < tri_r).astype(BF16)
        a_ra[pr] = quad(half, 0, tri_c <= tri_r).astype(BF16)
        a_rk[pr] = quad(half, half, tri_c <= tri_r).astype(BF16)
        vs[pr] = stacked(v, *pr).astype(BF16)
    for pr in pairs:
        n_pow[pr] = a_ka[pr]
        inv[pr] = eye - n_pow[pr]
    def later_rows(x, t0):
        if t0 == 0:
            return x
        return jnp.concatenate([x[h * c + t0:(h + 1) * c] for h in range(RWKV_GROUP)], axis=0)

    def restore_rows(x, t0):
        if t0 == 0:
            return x
        rows = c - t0
        pad = jnp.zeros((t0, x.shape[1]), x.dtype)
        return jnp.concatenate(
            [blk for h in range(RWKV_GROUP) for blk in (pad, x[h * rows:(h + 1) * rows])], axis=0)

    for j in range(1, int(np.log2(c))):
        t0 = (1 << j) // 8 * 8
        for pr in pairs:
            pw = n_pow[pr]
            n_pow[pr] = restore_rows(_dot(later_rows(pw, t0).astype(BF16), pw.astype(BF16)), t0)
        for pr in pairs:
            upd = _dot(later_rows(inv[pr], t0).astype(BF16), n_pow[pr].astype(BF16))
            inv[pr] = inv[pr] + restore_rows(upd, t0)
    akv = {pr: _dot(a_kk[pr], vs[pr]).astype(BF16) for pr in pairs}
    pre = {}
    for pr in pairs:
        st = lambda x: stacked(x, *pr)
        invb = inv[pr].astype(BF16)
        kt = _dot(invb, st(kap_h).astype(BF16)).astype(BF16)
        w_mat = _dot(invb, akv[pr])
        akc = jnp.concatenate([st(alp_c), st(k_c)], axis=0).astype(BF16)
        pre[pr] = (kt, w_mat, a_ra[pr], a_rk[pr], st(r_h).astype(BF16), akc)

    y_rows = []
    for ci in range(n_chunks):
        s0 = [state_ref[gi] for gi in range(n_groups)]
        s0b = [x.astype(BF16) for x in s0]
        ub = [(-(_dot_nt(pre[ci, gi][0], s0b[gi]) + pre[ci, gi][1])).astype(BF16)
              for gi in range(n_groups)]
        ys = []
        for gi in range(n_groups):
            kt, w_mat, a_ra, a_rk, rh, akc = pre[ci, gi]
            uv = jnp.concatenate([ub[gi], vs[ci, gi]], axis=0)
            sl = slice(gi * gw, (gi + 1) * gw)
            state_ref[gi] = s0[gi] * gam_end[ci * c:ci * c + 1, sl] + _dot_tn(uv, akc)
            y_st = _dot_nt(rh, s0b[gi]) + _dot(a_ra, ub[gi]) + _dot(a_rk, vs[ci, gi])
            y = y_st[0:c]
            for h in range(1, RWKV_GROUP):
                y = y + y_st[h * c:(h + 1) * c]
            ys.append(y)
        y_rows.append(jnp.concatenate(ys, axis=1))
    y = jnp.concatenate(y_rows, axis=0)

    inv_n = 1.0 / RWKV_HEAD
    mean = _head_sum(y, ind) * inv_n
    yc = y - mean
    var = _head_sum(yc * yc, ind) * inv_n
    y = yc * lax.rsqrt(var + RWKV_GN_EPS) * ln_w + ln_b
    bonus = _head_sum(r * k2 * r_k, ind) * v
    o_ref[...] = ((y + bonus) * g).astype(o_ref.dtype)


def _rwkv(p_rwkv, mu, vecs, w2, a2, g2, *, batch, seq, n_chunks):
    t = p_rwkv.shape[0]
    c = RWKV_CHUNK
    tb = n_chunks * c
    nb = seq // tb
    gw = RWKV_GROUP * RWKV_HEAD
    head = np.arange(RWKV_DIM) // RWKV_HEAD
    ind = jnp.asarray(head[:, None] == head[None, :], BF16)
    tt = np.arange(tb)
    tri = jnp.asarray((tt[None, :] <= tt[:, None]) & (tt[None, :] // c == tt[:, None] // c), BF16)
    return pl.pallas_call(
        functools.partial(_rwkv_kernel, n_chunks=n_chunks),
        grid=(batch, nb),
        in_specs=[
            pl.BlockSpec((tb, RWKV_COLS), lambda b, j: (b * nb + j, 0)),
            _const_spec((1, RWKV_COLS)),
            _const_spec(vecs.shape),
            _const_spec(w2.shape),
            _const_spec(a2.shape),
            _const_spec(g2.shape),
            _const_spec(ind.shape),
            _const_spec(tri.shape),
        ],
        out_specs=pl.BlockSpec((tb, RWKV_DIM), lambda b, j: (b * nb + j, 0)),
        out_shape=jax.ShapeDtypeStruct((t, RWKV_DIM), BF16),
        scratch_shapes=[
            pltpu.VMEM((RWKV_HEADS // RWKV_GROUP, gw, gw), F32),
            pltpu.VMEM((1, RWKV_COLS), F32),
        ],
        compiler_params=_params("parallel", "arbitrary"),
        name="rwkv7",
    )(p_rwkv, mu, vecs, w2, a2, g2, ind, tri)


def _hgrn_masks():
    c = HG_CHUNK
    t = np.arange(c)
    mask = np.zeros((HG_LEVELS + 1, c, c), np.float32)
    mask[0] = np.eye(c)
    for l in range(1, HG_LEVELS + 1):
        m = 1 << l
        is_right = (t % m) >= m // 2
        same = (t[:, None] // m) == (t[None, :] // m)
        mask[l] = same & is_right[:, None] & ~is_right[None, :]
    return mask


def _hgrn_chunk(proj, lb, gain, tri, mask_ref, state_ref, side_jobs):
    c = HG_CHUNK
    jobs = iter(side_jobs)
    run_job = lambda: next(jobs, lambda: None)()
    q, f, vals, g = proj
    n = q.shape[1]
    q = q * jax.nn.sigmoid(q)
    forget = lb + (1.0 - lb) * jax.nn.sigmoid(f)
    key = 1.0 - forget
    logf = jnp.log(forget)

    b = _dot_split(tri, logf)
    e_cum = jnp.exp(b)
    b_end = b[c - 1:c, :]
    q_hat = (q * e_cum).astype(BF16)
    k_hat = (key * jnp.exp(b_end - b)).astype(BF16)
    decay_end = e_cum[c - 1:c, :]
    qb, kb = q.astype(BF16), key.astype(BF16)
    run_job()

    t_idx = lax.broadcasted_iota(jnp.int32, q.shape, 0)
    f_prev = pltpu.roll(forget, 1, 0)
    f_next = pltpu.roll(forget, c - 1, 0)
    level_ops = []
    for l in range(HG_LEVELS):
        m = 2 << l
        is_right = (t_idx & (m // 2)) != 0
        if m == 2:
            decay = jnp.where(is_right, forget, 1.0)
        elif m == 4:
            pos = t_idx & 3
            decay = jnp.where(pos == 0, f_next,
                              jnp.where(pos == 1, 1.0, jnp.where(pos == 2, forget, forget * f_prev)))
        else:
            b3 = b.reshape(c // m, m, n)
            b_ref = jnp.broadcast_to(b3[:, m // 2 - 1:m // 2, :], (c // m, m, n)).reshape(c, n)
            decay = jnp.exp(-jnp.abs(b - b_ref))
        level_ops.append((jnp.where(is_right, q, key) * decay).astype(BF16))
        run_job()

    heads = [slice(h * HG_K, (h + 1) * HG_K) for h in range(HG_HEADS)]
    attns = []
    for sl in heads:
        attn = mask_ref[0] * _dot_nt(qb[:, sl], kb[:, sl])
        for l in range(HG_LEVELS):
            ml = level_ops[l][:, sl]
            attn = attn + mask_ref[l + 1] * _dot_nt(ml, ml)
        attns.append(attn.astype(BF16))
        run_job()
    vhs = [vals[:, sl].astype(BF16) for sl in heads]
    sts = [state_ref[h] for h in range(HG_HEADS)]
    os = [_dot_nt(q_hat[:, sl], sts[h].astype(BF16)) + _dot(attns[h], vhs[h])
          for h, sl in enumerate(heads)]
    for h, sl in enumerate(heads):
        state_ref[h] = sts[h] * decay_end[:, sl] + _dot_tn(vhs[h], k_hat[:, sl])
    outs = []
    for h, sl in enumerate(heads):
        o = os[h]
        ms = jnp.mean(o * o, axis=-1, keepdims=True)
        gt = g[:, sl]
        outs.append((o * lax.rsqrt(ms + RMS_EPS) * gain * (gt * jax.nn.sigmoid(gt))).astype(BF16))
    for job in jobs:
        job()
    return jnp.concatenate(outs, axis=1)


def _hgrn_layer_kernel(x_ref, xn_ref, mod_ref, ngain_ref, win_ref, lbl_ref, gain_ref, tri_ref,
                       mask_ref, wout_ref, o_ref, state_ref, proj_ref, *, layer):
    c = HG_CHUNK
    mod = mod_ref[0]
    per_proj = proj_ref.shape[3] // HG_PROJ_PIECE

    def projection_jobs(x, slot):
        hin = _norm_mod(x, ngain_ref[...], mod, 0, 1).astype(BF16)

        def piece(k, p):
            def job():
                lo = p * HG_PROJ_PIECE
                proj_ref[slot, k, :, lo:lo + HG_PROJ_PIECE] = _dot(hin, win_ref[k * per_proj + p])
            return job

        return [piece(k, p) for k in range(4) for p in range(per_proj)]

    @pl.when(pl.program_id(1) == 0)
    def _():
        state_ref[...] = jnp.zeros_like(state_ref)
        for job in projection_jobs(x_ref[0:c, :], 0):
            job()

    logits = lbl_ref[...]
    pe = jnp.exp(logits - jnp.max(logits, axis=0, keepdims=True))
    lb = jnp.sum(pe[1:layer + 1], axis=0, keepdims=True) / jnp.sum(pe, axis=0, keepdims=True)

    def recur(slot, side_jobs):
        proj = tuple(proj_ref[slot, k] for k in range(4))
        return _hgrn_chunk(proj, lb, gain_ref[...], tri_ref[...], mask_ref, state_ref, side_jobs)

    y0 = recur(0, projection_jobs(x_ref[c:2 * c, :], 1))
    y1 = recur(1, projection_jobs(xn_ref[...], 0))
    y = jnp.concatenate([y0, y1], axis=0)
    o_ref[...] = x_ref[...] + mod[2:3] * _dot(y, wout_ref[...])


def _hgrn_layer(x, mod, norm_gain, w_in, lb_logits, gain, w_out, *, batch, seq, layer):
    t, d = x.shape
    c = HG_CHUNK
    n = HG_HEADS * HG_K
    nb = seq // (2 * c)
    tri = jnp.asarray(np.tril(np.ones((c, c))), BF16)
    mask = jnp.asarray(_hgrn_masks(), F32)
    w_pieces = w_in.reshape(d, 4 * n // HG_PROJ_PIECE, HG_PROJ_PIECE).transpose(1, 0, 2)
    next_chunk = lambda b, j: (b * 2 * nb + jnp.minimum(2 * j + 2, 2 * nb - 1), 0)
    return pl.pallas_call(
        functools.partial(_hgrn_layer_kernel, layer=layer),
        grid=(batch, nb),
        in_specs=[pl.BlockSpec((2 * c, d), lambda b, j: (b * nb + j, 0)),
                  pl.BlockSpec((c, d), next_chunk),
                  pl.BlockSpec((1, 6, d), lambda b, j: (b, 0, 0)),
                  _const_spec((1, d)), _const_spec(w_pieces.shape),
                  _const_spec(lb_logits.shape), _const_spec((1, HG_V)),
                  _const_spec(tri.shape), _const_spec(mask.shape), _const_spec(w_out.shape)],
        out_specs=pl.BlockSpec((2 * c, d), lambda b, j: (b * nb + j, 0)),
        out_shape=jax.ShapeDtypeStruct((t, d), F32),
        scratch_shapes=[pltpu.VMEM((HG_HEADS, HG_V, HG_K), F32),
                        pltpu.VMEM((2, 4, c, n), F32)],
        compiler_params=_params("parallel", "arbitrary"),
        name="hgrn2_layer",
    )(x, x, mod, norm_gain.reshape(1, d), w_pieces, lb_logits, gain, tri, mask, w_out)


def _pad_cols(w, n):
    return jnp.pad(w, ((0, 0), (0, n - w.shape[1])))


def _mla_weights(w_uq, w_ukv):
    hq = MLA_NOPE + MLA_ROPE
    wq = w_uq.reshape(Q_LORA, MLA_HEADS, hq)
    wq = jnp.pad(wq, ((0, 0), (0, 0), (0, MLA_HEAD_PAD - hq))).reshape(Q_LORA, -1)
    wkv = w_ukv.reshape(KV_LORA, MLA_HEADS, MLA_NOPE + MLA_V)
    wk = jnp.pad(wkv[:, :, :MLA_NOPE], ((0, 0), (0, 0), (0, MLA_HEAD_PAD - MLA_NOPE)))
    wk = wk.reshape(KV_LORA, -1)
    n_extra = MLA_COLS_PAD - Q_LORA - KV_LORA
    place = np.zeros((n_extra, MLA_HEADS, MLA_HEAD_PAD), np.float32)
    for r in range(MLA_ROPE):
        place[r, :, MLA_NOPE + r] = 1.0
    wk_full = jnp.concatenate([wk, jnp.asarray(place.reshape(n_extra, -1))], axis=0)
    wv = jnp.pad(wkv[:, :, MLA_NOPE:], ((0, 0), (0, 0), (0, MLA_VA - MLA_V))).reshape(KV_LORA, -1)
    wv_full = jnp.concatenate([wv, jnp.zeros((n_extra, wv.shape[1]), F32)], axis=0)
    v_ones = np.zeros((MLA_HEADS, MLA_VA, 1), np.float32)
    v_ones[:, MLA_V] = 1.0
    return (wq.T.astype(BF16), wk_full.astype(BF16), wv_full.T.astype(BF16),
            jnp.asarray(v_ones.reshape(-1, 1)))


def _rope_inv_freq():
    inv = 1.0 / (ROPE_BASE ** (jnp.arange(0, MLA_ROPE, 2, dtype=F32) / MLA_ROPE))
    return inv.reshape(MLA_ROPE // 2, 1)


def kernel(x, c, positions, ada_w, ada_b, norm_mix, norm_ffn, w_in_even, mla_q_norm, mla_w_uq, mla_kv_norm, mla_w_ukv, rwkv_mu, rwkv_w0, rwkv_w2, rwkv_a0, rwkv_a2, rwkv_g2, rwkv_k_k, rwkv_k_a, rwkv_r_k, rwkv_ln_w, rwkv_ln_b, w_out_even, w_in_odd, hg_lb_logits, hg_out_norm, w_out_odd, ffn_w_gate, ffn_w_up, ffn_w_down, final_norm):
    batch, seq, d = x.shape
    depth = ada_w.shape[0]
    t = batch * seq
    tm = min(512, seq)
    xt = x.reshape(t, d)

    mod_all = _ada_mod(c, ada_w, ada_b).reshape(depth, batch, 6, d)

    for l in range(depth):
        mod = mod_all[l]
        j = l // 2
        if l % 2 == 0:
            w_in = w_in_even[j]
            w_mla = _pad_cols(w_in[:, :MLA_COLS], MLA_COLS_PAD).astype(BF16)
            w_rwkv = w_in[:, MLA_COLS:].astype(BF16)
            p_mla, p_rwkv = _in_proj(xt, mod, norm_mix[l], [w_mla, w_rwkv],
                                     seq=seq, sh=0, sc=1, tm=tm, out_dtype=BF16)
            wqt, wk, wvt, v_ones = _mla_weights(mla_w_uq[j], mla_w_ukv[j])
            qt, kh, vt = _mla_prep(p_mla, positions, _rope_inv_freq(),
                                   mla_q_norm[j].reshape(1, -1), mla_kv_norm[j].reshape(1, -1),
                                   wqt, wk, wvt, v_ones, batch=batch, seq=seq, tm=tm, tk=ATTN_TILE)
            y_a = _attention(qt, kh, vt, batch=batch, seq=seq, tq=ATTN_TILE, nh=ATTN_HEADS)

            zeros_lo = jnp.zeros((DECAY_LORA, RWKV_DIM), F32)
            w2 = jnp.concatenate([rwkv_w2[j], zeros_lo], axis=0).astype(BF16)
            a2 = jnp.concatenate([zeros_lo, rwkv_a2[j]], axis=0).astype(BF16)
            vecs = jnp.stack([rwkv_w0[j], rwkv_a0[j], rwkv_k_k[j], rwkv_k_a[j],
                              rwkv_r_k[j].reshape(-1), rwkv_ln_w[j], rwkv_ln_b[j],
                              jnp.zeros((RWKV_DIM,), F32)])
            y_b = _rwkv(p_rwkv, rwkv_mu[j].reshape(1, -1), vecs, w2, a2,
                        rwkv_g2[j].astype(BF16), batch=batch, seq=seq,
                        n_chunks=min(RWKV_BLOCK_CHUNKS, seq // RWKV_CHUNK))
            w_out = w_out_even[j].astype(BF16)
            n_a = MLA_HEADS * MLA_V
            ys, w_outs = [y_a, y_b], [w_out[:n_a], w_out[n_a:]]
        else:
            xt = _hgrn_layer(xt, mod, norm_mix[l], w_in_odd[j].astype(BF16), hg_lb_logits,
                             hg_out_norm[j].reshape(1, -1), w_out_odd[j].astype(BF16),
                             batch=batch, seq=seq, layer=l)
            ys, w_outs = [], []
        xt = _ffn(xt, mod, norm_ffn[l], ffn_w_gate[l].astype(BF16), ffn_w_up[l].astype(BF16),
                  ffn_w_down[l].astype(BF16), final_norm, ys, w_outs, seq=seq, tm=tm,
                  final=(l == depth - 1))
    return xt.reshape(batch, seq, d)
```

```python
import functools

import numpy as np
import jax
import jax.numpy as jnp
from jax import lax
from jax.experimental import pallas as pl
from jax.experimental.pallas import tpu as pltpu

F32 = jnp.float32
BF16 = jnp.bfloat16

RMS_EPS = 1e-6
LOG2_E = 1.4426950408889634
EXP_NEG_HALF = 0.6065306597126334
LANES = 128
VMEM_LIMIT = 56 * 1024 * 1024

MLA_HEADS = 8
MLA_NOPE = 64
MLA_ROPE = 32
MLA_V = 64
MLA_VA = 80
Q_LORA = 384
KV_LORA = 256
ROPE_BASE = 10000.0
MLA_COLS = Q_LORA + KV_LORA + MLA_ROPE
MLA_COLS_PAD = 768
MLA_HEAD_PAD = 128
ATTN_TILE = 512
ATTN_HEADS = 8

RWKV_HEAD = 64
RWKV_DIM = 512
RWKV_HEADS = 8
DECAY_LORA = 64
AAA_LORA = 64
GATE_LORA = 128
RWKV_GN_EPS = RWKV_HEAD * 1e-5
RWKV_COLS = 3 * RWKV_DIM + DECAY_LORA + AAA_LORA + GATE_LORA
RWKV_CHUNK = 64
RWKV_BLOCK_CHUNKS = 4
RWKV_GROUP = 4

HG_K = 128
HG_HEADS = 8
HG_V = 128
HG_CHUNK = 128
HG_LEVELS = 7
HG_PROJ_PIECE = 256


def _params(*sem, fuse_inputs=None):
    return pltpu.CompilerParams(dimension_semantics=sem, vmem_limit_bytes=VMEM_LIMIT,
                                allow_input_fusion=fuse_inputs)


def _const_spec(shape):
    nd = len(shape)
    return pl.BlockSpec(shape, lambda *_: (0,) * nd, pipeline_mode=pl.Buffered(1))


def _dot(a, b):
    return jnp.dot(a, b, preferred_element_type=F32)


def _dot_nt(a, b):
    return lax.dot_general(a, b, (((1,), (1,)), ((), ())), preferred_element_type=F32)


def _dot_tn(a, b):
    return lax.dot_general(a, b, (((0,), (0,)), ((), ())), preferred_element_type=F32)


def _dot_split(sel, x):
    hi = x.astype(BF16)
    lo = (x - hi.astype(F32)).astype(BF16)
    return _dot(sel, hi) + _dot(sel, lo)


def _head_sum(x, ind):
    return _dot(x.astype(BF16), ind)


def _ada_kernel(c_ref, w_ref, b_ref, o_ref):
    c = c_ref[...]
    cond = c * jax.nn.sigmoid(c)
    o_ref[0] = _dot(cond.astype(BF16), w_ref[0].astype(BF16)) + b_ref[0]


def _ada_mod(c, ada_w, ada_b):
    depth, d, n = ada_w.shape
    b = c.shape[0]
    tn = n // 4
    return pl.pallas_call(
        _ada_kernel,
        grid=(depth, n // tn),
        in_specs=[
            pl.BlockSpec((b, d), lambda l, j: (0, 0)),
            pl.BlockSpec((1, d, tn), lambda l, j: (l, 0, j)),
            pl.BlockSpec((1, 1, tn), lambda l, j: (l, 0, j)),
        ],
        out_specs=pl.BlockSpec((1, b, tn), lambda l, j: (l, 0, j)),
        out_shape=jax.ShapeDtypeStruct((depth, b, n), F32),
        compiler_params=_params("arbitrary", "arbitrary"),
        name="ada_mod",
    )(c, ada_w, ada_b.reshape(depth, 1, n))


def _norm_mod(x, gain, mod, sh, sc):
    ms = jnp.mean(x * x, axis=-1, keepdims=True)
    y = x * lax.rsqrt(ms + RMS_EPS) * gain
    return y * (1.0 + mod[sc:sc + 1]) + mod[sh:sh + 1]


def _in_proj_kernel(x_ref, mod_ref, gain_ref, *refs, n_out, sh, sc):
    w_refs, o_refs = refs[:n_out], refs[n_out:]
    h = _norm_mod(x_ref[...], gain_ref[...], mod_ref[0], sh, sc).astype(BF16)
    for w_ref, o_ref in zip(w_refs, o_refs):
        o_ref[...] = _dot(h, w_ref[...]).astype(o_ref.dtype)


def _in_proj(x, mod, gain, weights, *, seq, sh, sc, tm, out_dtype):
    t, d = x.shape
    per_b = seq // tm
    n_out = len(weights)
    in_specs = [
        pl.BlockSpec((tm, d), lambda i: (i, 0)),
        pl.BlockSpec((1, 6, d), lambda i: (i // per_b, 0, 0)),
        _const_spec((1, d)),
    ] + [_const_spec(w.shape) for w in weights]
    out_specs = [pl.BlockSpec((tm, w.shape[1]), lambda i: (i, 0)) for w in weights]
    out_shape = [jax.ShapeDtypeStruct((t, w.shape[1]), out_dtype) for w in weights]
    return pl.pallas_call(
        functools.partial(_in_proj_kernel, n_out=n_out, sh=sh, sc=sc),
        grid=(t // tm,),
        in_specs=in_specs,
        out_specs=out_specs,
        out_shape=out_shape,
        compiler_params=_params("parallel", fuse_inputs=[False] * 3 + [True] * n_out),
        name="in_proj",
    )(x, mod, gain.reshape(1, d), *weights)


def _ffn_kernel(x_ref, mod_ref, gain_ref, wg_ref, wu_ref, wd_ref, fin_ref, *refs, n_in, final):
    y_refs, w_refs, o_ref = refs[:n_in], refs[n_in:2 * n_in], refs[2 * n_in]
    x = x_ref[...]
    mod = mod_ref[0]
    if n_in:
        mix = _dot(y_refs[0][...], w_refs[0][...])
        for y_ref, w_ref in zip(y_refs[1:], w_refs[1:]):
            mix += _dot(y_ref[...], w_ref[...])
        x = x + mod[2:3] * mix
    h = _norm_mod(x, gain_ref[...], mod, 3, 4).astype(BF16)
    gate = _dot(h, wg_ref[...])
    up = _dot(h, wu_ref[...])
    act = (gate * jax.nn.sigmoid(gate) * up).astype(BF16)
    y = x + mod[5:6] * _dot(act, wd_ref[...])
    if final:
        ms = jnp.mean(y * y, axis=-1, keepdims=True)
        y = y * lax.rsqrt(ms + RMS_EPS) * fin_ref[...]
    o_ref[...] = y


def _ffn(x, mod, gain, wg, wu, wd, fin, ys, w_outs, *, seq, tm, final):
    t, d = x.shape
    per_b = seq // tm
    return pl.pallas_call(
        functools.partial(_ffn_kernel, n_in=len(ys), final=final),
        grid=(t // tm,),
        in_specs=[
            pl.BlockSpec((tm, d), lambda i: (i, 0)),
            pl.BlockSpec((1, 6, d), lambda i: (i // per_b, 0, 0)),
            _const_spec((1, d)),
            _const_spec(wg.shape),
            _const_spec(wu.shape),
            _const_spec(wd.shape),
            _const_spec((1, d)),
        ] + [pl.BlockSpec((tm, y.shape[1]), lambda i: (i, 0)) for y in ys]
          + [_const_spec(w.shape) for w in w_outs],
        out_specs=pl.BlockSpec((tm, d), lambda i: (i, 0)),
        out_shape=jax.ShapeDtypeStruct((t, d), F32),
        compiler_params=_params("parallel", fuse_inputs=[False] * 3 + [True] * 3 + [False]
                                + [False] * len(ys) + [True] * len(w_outs)),
        name="ffn",
    )(x, mod, gain.reshape(1, d), wg, wu, wd, fin.reshape(1, d), *ys, *w_outs)


def _mla_prep_kernel(p_ref, pos_ref, invf_ref, qn_ref, kvn_ref, wqt_ref, wk_ref, wvt_ref, vone_ref,
                     qt_ref, k_ref, vt_ref, *, scale, tk):
    p = p_ref[...].astype(F32)
    tm = p.shape[0]
    half = MLA_ROPE // 2
    ang = invf_ref[...] * pos_ref[0].astype(F32)
    cos_t, sin_t = jnp.cos(ang), jnp.sin(ang)

    c_q = p[:, :Q_LORA]
    ms = jnp.mean(c_q * c_q, axis=-1, keepdims=True)
    cqn = (c_q * lax.rsqrt(ms + RMS_EPS) * qn_ref[...]).astype(BF16)
    qt = _dot_nt(wqt_ref[...], cqn) * scale
    pieces = []
    for h in range(MLA_HEADS):
        base = h * MLA_HEAD_PAD
        x1 = qt[base + MLA_NOPE:base + MLA_NOPE + half]
        x2 = qt[base + MLA_NOPE + half:base + MLA_NOPE + MLA_ROPE]
        pieces += [qt[base:base + MLA_NOPE], x1 * cos_t - x2 * sin_t, x1 * sin_t + x2 * cos_t,
                   qt[base + MLA_NOPE + MLA_ROPE:base + MLA_HEAD_PAD]]
    qt_ref[0] = jnp.concatenate(pieces, axis=0).astype(qt_ref.dtype)

    ckv = p[:, Q_LORA:Q_LORA + KV_LORA]
    ss = jnp.mean(ckv * ckv, axis=-1, keepdims=True)
    ckvn = ckv * lax.rsqrt(ss + RMS_EPS) * kvn_ref[...]
    kr = p[:, Q_LORA + KV_LORA:]
    rest = kr.shape[1] - MLA_ROPE
    cos_k = jnp.concatenate([cos_t, cos_t, jnp.ones((rest, tm), F32)], axis=0).T
    sin_k = jnp.concatenate([sin_t, sin_t, jnp.zeros((rest, tm), F32)], axis=0).T
    lane = lax.broadcasted_iota(jnp.int32, kr.shape, 1)
    partner = jnp.where(lane < half, -pltpu.roll(kr, kr.shape[1] - half, 1), pltpu.roll(kr, half, 1))
    lhs = jnp.concatenate([ckvn, kr * cos_k + partner * sin_k], axis=1).astype(BF16)
    k_ref[...] = _dot(lhs, wk_ref[...]).astype(k_ref.dtype)
    vt = (_dot_nt(wvt_ref[...], lhs) + vone_ref[...]).astype(vt_ref.dtype)
    for jj in range(tm // tk):
        vt_ref[0, jj] = vt[:, jj * tk:(jj + 1) * tk]


def _mla_prep(p_mla, pos, invf, q_norm, kv_norm, wqt, wk, wvt, v_ones, *, batch, seq, tm, tk):
    t = p_mla.shape[0]
    n_q = MLA_HEADS * MLA_HEAD_PAD
    n_v = MLA_HEADS * MLA_VA
    per_b = seq // tm
    scale = (MLA_NOPE + MLA_ROPE) ** -0.5 * LOG2_E
    return pl.pallas_call(
        functools.partial(_mla_prep_kernel, scale=scale, tk=tk),
        grid=(t // tm,),
        in_specs=[
            pl.BlockSpec((tm, MLA_COLS_PAD), lambda i: (i, 0)),
            pl.BlockSpec((1, 1, tm), lambda i: (i, 0, 0)),
            _const_spec(invf.shape),
            _const_spec((1, Q_LORA)),
            _const_spec((1, KV_LORA)),
            _const_spec(wqt.shape),
            _const_spec(wk.shape),
            _const_spec(wvt.shape),
            _const_spec(v_ones.shape),
        ],
        out_specs=[
            pl.BlockSpec((1, n_q, tm), lambda i: (i // per_b, 0, i % per_b)),
            pl.BlockSpec((tm, n_q), lambda i: (i, 0)),
            pl.BlockSpec((1, tm // tk, n_v, tk), lambda i: (i // per_b, i % per_b, 0, 0)),
        ],
        out_shape=[
            jax.ShapeDtypeStruct((batch, n_q, seq), BF16),
            jax.ShapeDtypeStruct((t, n_q), BF16),
            jax.ShapeDtypeStruct((batch, seq // tk, n_v, tk), BF16),
        ],
        compiler_params=_params("parallel"),
        name="mla_prep",
    )(p_mla, pos.reshape(t // tm, 1, tm), invf, q_norm, kv_norm, wqt, wk, wvt, v_ones)


def _attn_kernel(qt_ref, k_ref, vt_ref, o_ref, s_ref, p_ref, *, tq, nh):
    i = pl.program_id(2)
    qts = [qt_ref[0, h * MLA_HEAD_PAD:(h + 1) * MLA_HEAD_PAD, :] for h in range(nh)]
    causal = (lax.broadcasted_iota(jnp.int32, (tq, tq), 0)
              <= lax.broadcasted_iota(jnp.int32, (tq, tq), 1))

    def put_scores(j, slot):
        start = pl.multiple_of(j * tq, tq)
        for h in range(nh):
            s_ref[slot, h] = _dot(
                k_ref[pl.ds(start, tq), h * MLA_HEAD_PAD:(h + 1) * MLA_HEAD_PAD], qts[h])

    def value_dots(j, probs):
        return [_dot(vt_ref[0, j, h * MLA_VA:(h + 1) * MLA_VA, :], probs[h]) for h in range(nh)]

    def step(carry, slot, pv_prev, masked):
        new, probs = [], []
        for h in range(nh):
            m, acc = carry[h]
            s = s_ref[slot, h]
            if masked:
                s = jnp.where(causal, s, -jnp.inf)
            m_new = jnp.maximum(m, jnp.max(s, axis=0, keepdims=True))
            probs.append(jnp.exp2(s - m_new).astype(BF16))
            new.append((m_new, jnp.exp2(m - m_new) * (acc + pv_prev[h])))
        return tuple(new), probs

    def parked(slot):
        return [p_ref[slot, h] for h in range(nh)]

    def pair(t, carry):
        j = 2 * t
        put_scores(j + 1, 1)
        pv = value_dots(jnp.where(t == 0, i, j - 1), parked(1))
        carry, probs = step(carry, 0, pv, False)
        for h in range(nh):
            p_ref[0, h] = probs[h]
        put_scores(j + 2, 0)
        pv = value_dots(j, parked(0))
        carry, probs = step(carry, 1, pv, False)
        for h in range(nh):
            p_ref[1, h] = probs[h]
        return carry

    def emit(carry, pv_last):
        outs = []
        for h, (_, acc) in enumerate(carry):
            tot = acc + pv_last[h]
            outs.append(tot[:MLA_V] * (1.0 / tot[MLA_V:MLA_V + 1]))
        o_ref[...] = jnp.concatenate(outs, axis=0).T.astype(o_ref.dtype)

    put_scores(i, 1)
    put_scores(0, 0)
    zeros = [jnp.zeros((MLA_VA, tq), F32)] * nh
    init = tuple((jnp.full((1, tq), -jnp.inf, F32), zeros[h]) for h in range(nh))
    carry, probs = step(init, 1, zeros, True)
    for h in range(nh):
        p_ref[1, h] = probs[h]
    carry = lax.fori_loop(0, i // 2, pair, carry)
    n_done = 2 * (i // 2)
    last = jnp.where(n_done == 0, i, n_done - 1)

    @pl.when(i % 2 == 0)
    def _():
        emit(carry, value_dots(last, parked(1)))

    @pl.when(i % 2 == 1)
    def _():
        final, probs = step(carry, 0, value_dots(last, parked(1)), False)
        emit(final, value_dots(i - 1, probs))


def _attention(qt, k, vt, *, batch, seq, tq, nh):
    t = k.shape[0]
    nq = seq // tq
    return pl.pallas_call(
        functools.partial(_attn_kernel, tq=tq, nh=nh),
        grid=(batch, MLA_HEADS // nh, nq),
        in_specs=[
            pl.BlockSpec((1, nh * MLA_HEAD_PAD, tq), lambda b, h, i: (b, h, i)),
            pl.BlockSpec((seq, nh * MLA_HEAD_PAD), lambda b, h, i: (b, h)),
            pl.BlockSpec((1, seq // tq, nh * MLA_VA, tq), lambda b, h, i: (b, 0, h, 0)),
        ],
        out_specs=pl.BlockSpec((tq, nh * MLA_V), lambda b, h, i: (b * nq + i, h)),
        out_shape=jax.ShapeDtypeStruct((t, MLA_HEADS * MLA_V), BF16),
        scratch_shapes=[pltpu.VMEM((2, nh, tq, tq), F32), pltpu.VMEM((2, nh, tq, tq), BF16)],
        compiler_params=_params("parallel", "parallel", "arbitrary"),
        name="mla_attention",
    )(qt, k, vt)


def _stack_heads(x, lane_head):
    return jnp.concatenate(
        [jnp.where(lane_head == h, x, 0.0) for h in range(RWKV_GROUP)], axis=0)


def _rwkv_kernel(p_ref, mu_ref, vec_ref, w2_ref, a2_ref, g2_ref, ind_ref, tri_ref,
                 o_ref, state_ref, prev_ref, *, n_chunks):
    c = RWKV_CHUNK
    gw = RWKV_GROUP * RWKV_HEAD
    gc = RWKV_GROUP * c
    n_groups = RWKV_HEADS // RWKV_GROUP
    tb = n_chunks * c

    @pl.when(pl.program_id(1) == 0)
    def _():
        state_ref[...] = jnp.zeros_like(state_ref)
        prev_ref[...] = jnp.zeros_like(prev_ref)

    p = p_ref[...].astype(F32)
    rows = lax.broadcasted_iota(jnp.int32, p.shape, 0)
    shifted = jnp.where(rows == 0, prev_ref[...], pltpu.roll(p, 1, 0))
    prev_ref[...] = p[tb - 1:tb, :]
    p = p + (shifted - p) * mu_ref[...]

    d = RWKV_DIM
    r, k, v = p[:, :d], p[:, d:2 * d], p[:, 2 * d:3 * d]
    lo = p[:, 3 * d:3 * d + DECAY_LORA + AAA_LORA]
    g_lo = p[:, 3 * d + DECAY_LORA + AAA_LORA:]
    w0, a0, k_k, k_a = vec_ref[0:1], vec_ref[1:2], vec_ref[2:3], vec_ref[3:4]
    r_k, ln_w, ln_b = vec_ref[4:5], vec_ref[5:6], vec_ref[6:7]

    x_w = w0 + _dot(jnp.tanh(lo).astype(BF16), w2_ref[...])
    lw = -EXP_NEG_HALF * jax.nn.sigmoid(x_w)
    a = jax.nn.sigmoid(a0 + _dot(lo.astype(BF16), a2_ref[...]))
    g = _dot(jax.nn.sigmoid(g_lo).astype(BF16), g2_ref[...])

    ind = ind_ref[...]
    kk = k * k_k
    kk = kk * lax.rsqrt(jnp.maximum(_head_sum(kk * kk, ind), 1e-24))
    k2 = k * (1.0 + (a - 1.0) * k_a)
    alpha = kk * a

    b = _dot_split(tri_ref[...], lw)
    b_prev = b - lw
    bcast = lambda row: jnp.broadcast_to(row, (c, d))
    b_mid = jnp.concatenate([bcast(b[ci * c + c // 2 - 1:ci * c + c // 2]) for ci in range(n_chunks)], axis=0)
    b_end = jnp.concatenate([bcast(b[ci * c + c - 1:ci * c + c]) for ci in range(n_chunks)], axis=0)
    kap_t = kk * jnp.exp(b_prev - b_mid)
    r_t = r * jnp.exp(b - b_mid)
    inv_t = jnp.exp(b_mid - b)
    alp_t, k_t = alpha * inv_t, k2 * inv_t
    kap_h = kk * jnp.exp(b_prev)
    r_h = r * jnp.exp(b)
    tail = jnp.exp(b_end - b)
    alp_c, k_c = alpha * tail, k2 * tail
    gam_end = jnp.exp(b_end)

    eye = (lax.broadcasted_iota(jnp.int32, (gc, gc), 0)
           == lax.broadcasted_iota(jnp.int32, (gc, gc), 1)).astype(F32)
    lane_head = lax.broadcasted_iota(jnp.int32, (c, gw), 1) // RWKV_HEAD

    def stacked(x, ci, gi):
        return _stack_heads(x[ci * c:(ci + 1) * c, gi * gw:(gi + 1) * gw], lane_head)

    pairs = [(ci, gi) for ci in range(n_chunks) for gi in range(n_groups)]
    half = gc // 2
    lane_pair = lax.broadcasted_iota(jnp.int32, (c, half), 1) // RWKV_HEAD
    tri_r = lax.broadcasted_iota(jnp.int32, (half, half), 0) % c
    tri_c = lax.broadcasted_iota(jnp.int32, (half, half), 1) % c
    zero_blk = jnp.zeros((half, half), F32)

    def pair_stack(x, ci, pi):
        xs = x[ci * c:(ci + 1) * c, pi * half:(pi + 1) * half]
        return jnp.concatenate([jnp.where(lane_pair == k, xs, 0.0) for k in range(2)], axis=0)

    def block_diag(a, b_):
        return jnp.concatenate([jnp.concatenate([a, zero_blk], axis=1),
                                jnp.concatenate([zero_blk, b_], axis=1)], axis=0)

    a_ka, a_kk, a_ra, a_rk, vs, n_pow, inv = {}, {}, {}, {}, {}, {}, {}
    for pr in pairs:
        ci, gi = pr
        blocks = []
        for pi in (2 * gi, 2 * gi + 1):
            ps = lambda x: pair_stack(x, ci, pi)
            kr = jnp.concatenate([ps(kap_t), ps(r_t)], axis=0).astype(BF16)
            ak = jnp.concatenate([ps(alp_t), ps(k_t)], axis=0).astype(BF16)
            blocks.append(_dot_nt(kr, ak))
        quad = lambda r0, c0, keep: block_diag(
            *[jnp.where(keep, blk[r0:r0 + half, c0:c0 + half], 0.0) for blk in blocks])
        a_ka[pr] = quad(0, 0, tri_c < tri_r)
        a_kk[pr] = quad(0, half, tri_c < tri_r).astype(BF16)
        a_ra[pr] = quad(half, 0, tri_c <= tri_r).astype(BF16)
        a_rk[pr] = quad(half, half, tri_c <= tri_r).astype(BF16)
        vs[pr] = stacked(v, *pr).astype(BF16)
    for pr in pairs:
        n_pow[pr] = a_ka[pr]
        inv[pr] = eye - n_pow[pr]
    def later_rows(x, t0):
        if t0 == 0:
            return x
        return jnp.concatenate([x[h * c + t0:(h + 1) * c] for h in range(RWKV_GROUP)], axis=0)

    def restore_rows(x, t0):
        if t0 == 0:
            return x
        rows = c - t0
        pad = jnp.zeros((t0, x.shape[1]), x.dtype)
        return jnp.concatenate(
            [blk for h in range(RWKV_GROUP) for blk in (pad, x[h * rows:(h + 1) * rows])], axis=0)

    for j in range(1, int(np.log2(c))):
        t0 = (1 << j) // 8 * 8
        for pr in pairs:
            pw = n_pow[pr]
            n_pow[pr] = restore_rows(_dot(later_rows(pw, t0).astype(BF16), pw.astype(BF16)), t0)
        for pr in pairs:
            upd = _dot(later_rows(inv[pr], t0).astype(BF16), n_pow[pr].astype(BF16))
            inv[pr] = inv[pr] + restore_rows(upd, t0)
    akv = {pr: _dot(a_kk[pr], vs[pr]).astype(BF16) for pr in pairs}
    pre = {}
    for pr in pairs:
        st = lambda x: stacked(x, *pr)
        invb = inv[pr].astype(BF16)
        kt = _dot(invb, st(kap_h).astype(BF16)).astype(BF16)
        w_mat = _dot(invb, akv[pr])
        akc = jnp.concatenate([st(alp_c), st(k_c)], axis=0).astype(BF16)
        pre[pr] = (kt, w_mat, a_ra[pr], a_rk[pr], st(r_h).astype(BF16), akc)

    y_rows = []
    for ci in range(n_chunks):
        s0 = [state_ref[gi] for gi in range(n_groups)]
        s0b = [x.astype(BF16) for x in s0]
        ub = [(-(_dot_nt(pre[ci, gi][0], s0b[gi]) + pre[ci, gi][1])).astype(BF16)
              for gi in range(n_groups)]
        ys = []
        for gi in range(n_groups):
            kt, w_mat, a_ra, a_rk, rh, akc = pre[ci, gi]
            uv = jnp.concatenate([ub[gi], vs[ci, gi]], axis=0)
            sl = slice(gi * gw, (gi + 1) * gw)
            state_ref[gi] = s0[gi] * gam_end[ci * c:ci * c + 1, sl] + _dot_tn(uv, akc)
            y_st = _dot_nt(rh, s0b[gi]) + _dot(a_ra, ub[gi]) + _dot(a_rk, vs[ci, gi])
            y = y_st[0:c]
            for h in range(1, RWKV_GROUP):
                y = y + y_st[h * c:(h + 1) * c]
            ys.append(y)
        y_rows.append(jnp.concatenate(ys, axis=1))
    y = jnp.concatenate(y_rows, axis=0)

    inv_n = 1.0 / RWKV_HEAD
    mean = _head_sum(y, ind) * inv_n
    yc = y - mean
    var = _head_sum(yc * yc, ind) * inv_n
    y = yc * lax.rsqrt(var + RWKV_GN_EPS) * ln_w + ln_b
    bonus = _head_sum(r * k2 * r_k, ind) * v
    o_ref[...] = ((y + bonus) * g).astype(o_ref.dtype)


def _rwkv(p_rwkv, mu, vecs, w2, a2, g2, *, batch, seq, n_chunks):
    t = p_rwkv.shape[0]
    c = RWKV_CHUNK
    tb = n_chunks * c
    nb = seq // tb
    gw = RWKV_GROUP * RWKV_HEAD
    head = np.arange(RWKV_DIM) // RWKV_HEAD
    ind = jnp.asarray(head[:, None] == head[None, :], BF16)
    tt = np.arange(tb)
    tri = jnp.asarray((tt[None, :] <= tt[:, None]) & (tt[None, :] // c == tt[:, None] // c), BF16)
    return pl.pallas_call(
        functools.partial(_rwkv_kernel, n_chunks=n_chunks),
        grid=(batch, nb),
        in_specs=[
            pl.BlockSpec((tb, RWKV_COLS), lambda b, j: (b * nb + j, 0)),
            _const_spec((1, RWKV_COLS)),
            _const_spec(vecs.shape),
            _const_spec(w2.shape),
            _const_spec(a2.shape),
            _const_spec(g2.shape),
            _const_spec(ind.shape),
            _const_spec(tri.shape),
        ],
        out_specs=pl.BlockSpec((tb, RWKV_DIM), lambda b, j: (b * nb + j, 0)),
        out_shape=jax.ShapeDtypeStruct((t, RWKV_DIM), BF16),
        scratch_shapes=[
            pltpu.VMEM((RWKV_HEADS // RWKV_GROUP, gw, gw), F32),
            pltpu.VMEM((1, RWKV_COLS), F32),
        ],
        compiler_params=_params("parallel", "arbitrary"),
        name="rwkv7",
    )(p_rwkv, mu, vecs, w2, a2, g2, ind, tri)


def _hgrn_masks():
    c = HG_CHUNK
    t = np.arange(c)
    mask = np.zeros((HG_LEVELS + 1, c, c), np.float32)
    mask[0] = np.eye(c)
    for l in range(1, HG_LEVELS + 1):
        m = 1 << l
        is_right = (t % m) >= m // 2
        same = (t[:, None] // m) == (t[None, :] // m)
        mask[l] = same & is_right[:, None] & ~is_right[None, :]
    return mask


def _hgrn_chunk(proj, lb, gain, tri, mask_ref, state_ref, side_jobs):
    c = HG_CHUNK
    jobs = iter(side_jobs)
    run_job = lambda: next(jobs, lambda: None)()
    q, f, vals, g = proj
    n = q.shape[1]
    q = q * jax.nn.sigmoid(q)
    forget = lb + (1.0 - lb) * jax.nn.sigmoid(f)
    key = 1.0 - forget
    logf = jnp.log(forget)

    b = _dot_split(tri, logf)
    e_cum = jnp.exp(b)
    b_end = b[c - 1:c, :]
    q_hat = (q * e_cum).astype(BF16)
    k_hat = (key * jnp.exp(b_end - b)).astype(BF16)
    decay_end = e_cum[c - 1:c, :]
    qb, kb = q.astype(BF16), key.astype(BF16)
    run_job()

    t_idx = lax.broadcasted_iota(jnp.int32, q.shape, 0)
    f_prev = pltpu.roll(forget, 1, 0)
    f_next = pltpu.roll(forget, c - 1, 0)
    level_ops = []
    for l in range(HG_LEVELS):
        m = 2 << l
        is_right = (t_idx & (m // 2)) != 0
        if m == 2:
            decay = jnp.where(is_right, forget, 1.0)
        elif m == 4:
            pos = t_idx & 3
            decay = jnp.where(pos == 0, f_next,
                              jnp.where(pos == 1, 1.0, jnp.where(pos == 2, forget, forget * f_prev)))
        else:
            b3 = b.reshape(c // m, m, n)
            b_ref = jnp.broadcast_to(b3[:, m // 2 - 1:m // 2, :], (c // m, m, n)).reshape(c, n)
            decay = jnp.exp(-jnp.abs(b - b_ref))
        level_ops.append((jnp.where(is_right, q, key) * decay).astype(BF16))
        run_job()

    heads = [slice(h * HG_K, (h + 1) * HG_K) for h in range(HG_HEADS)]
    attns = []
    for sl in heads:
        attn = mask_ref[0] * _dot_nt(qb[:, sl], kb[:, sl])
        for l in range(HG_LEVELS):
            ml = level_ops[l][:, sl]
            attn = attn + mask_ref[l + 1] * _dot_nt(ml, ml)
        attns.append(attn.astype(BF16))
        run_job()
    vhs = [vals[:, sl].astype(BF16) for sl in heads]
    sts = [state_ref[h] for h in range(HG_HEADS)]
    os = [_dot_nt(q_hat[:, sl], sts[h].astype(BF16)) + _dot(attns[h], vhs[h])
          for h, sl in enumerate(heads)]
    for h, sl in enumerate(heads):
        state_ref[h] = sts[h] * decay_end[:, sl] + _dot_tn(vhs[h], k_hat[:, sl])
    outs = []
    for h, sl in enumerate(heads):
        o = os[h]
        ms = jnp.mean(o * o, axis=-1, keepdims=True)
        gt = g[:, sl]
        outs.append((o * lax.rsqrt(ms + RMS_EPS) * gain * (gt * jax.nn.sigmoid(gt))).astype(BF16))
    for job in jobs:
        job()
    return jnp.concatenate(outs, axis=1)


def _hgrn_layer_kernel(x_ref, xn_ref, mod_ref, ngain_ref, win_ref, lbl_ref, gain_ref, tri_ref,
                       mask_ref, wout_ref, o_ref, state_ref, proj_ref, *, layer):
    c = HG_CHUNK
    mod = mod_ref[0]
    per_proj = proj_ref.shape[3] // HG_PROJ_PIECE

    def projection_jobs(x, slot):
        hin = _norm_mod(x, ngain_ref[...], mod, 0, 1).astype(BF16)

        def piece(k, p):
            def job():
                lo = p * HG_PROJ_PIECE
                proj_ref[slot, k, :, lo:lo + HG_PROJ_PIECE] = _dot(hin, win_ref[k * per_proj + p])
            return job

        return [piece(k, p) for k in range(4) for p in range(per_proj)]

    @pl.when(pl.program_id(1) == 0)
    def _():
        state_ref[...] = jnp.zeros_like(state_ref)
        for job in projection_jobs(x_ref[0:c, :], 0):
            job()

    logits = lbl_ref[...]
    pe = jnp.exp(logits - jnp.max(logits, axis=0, keepdims=True))
    lb = jnp.sum(pe[1:layer + 1], axis=0, keepdims=True) / jnp.sum(pe, axis=0, keepdims=True)

    def recur(slot, side_jobs):
        proj = tuple(proj_ref[slot, k] for k in range(4))
        return _hgrn_chunk(proj, lb, gain_ref[...], tri_ref[...], mask_ref, state_ref, side_jobs)

    y0 = recur(0, projection_jobs(x_ref[c:2 * c, :], 1))
    y1 = recur(1, projection_jobs(xn_ref[...], 0))
    y = jnp.concatenate([y0, y1], axis=0)
    o_ref[...] = x_ref[...] + mod[2:3] * _dot(y, wout_ref[...])


def _hgrn_layer(x, mod, norm_gain, w_in, lb_logits, gain, w_out, *, batch, seq, layer):
    t, d = x.shape
    c = HG_CHUNK
    n = HG_HEADS * HG_K
    nb = seq // (2 * c)
    tri = jnp.asarray(np.tril(np.ones((c, c))), BF16)
    mask = jnp.asarray(_hgrn_masks(), F32)
    w_pieces = w_in.reshape(d, 4 * n // HG_PROJ_PIECE, HG_PROJ_PIECE).transpose(1, 0, 2)
    next_chunk = lambda b, j: (b * 2 * nb + jnp.minimum(2 * j + 2, 2 * nb - 1), 0)
    return pl.pallas_call(
        functools.partial(_hgrn_layer_kernel, layer=layer),
        grid=(batch, nb),
        in_specs=[pl.BlockSpec((2 * c, d), lambda b, j: (b * nb + j, 0)),
                  pl.BlockSpec((c, d), next_chunk),
                  pl.BlockSpec((1, 6, d), lambda b, j: (b, 0, 0)),
                  _const_spec((1, d)), _const_spec(w_pieces.shape),
                  _const_spec(lb_logits.shape), _const_spec((1, HG_V)),
                  _const_spec(tri.shape), _const_spec(mask.shape), _const_spec(w_out.shape)],
        out_specs=pl.BlockSpec((2 * c, d), lambda b, j: (b * nb + j, 0)),
        out_shape=jax.ShapeDtypeStruct((t, d), F32),
        scratch_shapes=[pltpu.VMEM((HG_HEADS, HG_V, HG_K), F32),
                        pltpu.VMEM((2, 4, c, n), F32)],
        compiler_params=_params("parallel", "arbitrary"),
        name="hgrn2_layer",
    )(x, x, mod, norm_gain.reshape(1, d), w_pieces, lb_logits, gain, tri, mask, w_out)


def _pad_cols(w, n):
    return jnp.pad(w, ((0, 0), (0, n - w.shape[1])))


def _mla_weights(w_uq, w_ukv):
    hq = MLA_NOPE + MLA_ROPE
    wq = w_uq.reshape(Q_LORA, MLA_HEADS, hq)
    wq = jnp.pad(wq, ((0, 0), (0, 0), (0, MLA_HEAD_PAD - hq))).reshape(Q_LORA, -1)
    wkv = w_ukv.reshape(KV_LORA, MLA_HEADS, MLA_NOPE + MLA_V)
    wk = jnp.pad(wkv[:, :, :MLA_NOPE], ((0, 0), (0, 0), (0, MLA_HEAD_PAD - MLA_NOPE)))
    wk = wk.reshape(KV_LORA, -1)
    n_extra = MLA_COLS_PAD - Q_LORA - KV_LORA
    place = np.zeros((n_extra, MLA_HEADS, MLA_HEAD_PAD), np.float32)
    for r in range(MLA_ROPE):
        place[r, :, MLA_NOPE + r] = 1.0
    wk_full = jnp.concatenate([wk, jnp.asarray(place.reshape(n_extra, -1))], axis=0)
    wv = jnp.pad(wkv[:, :, MLA_NOPE:], ((0, 0), (0, 0), (0, MLA_VA - MLA_V))).reshape(KV_LORA, -1)
    wv_full = jnp.concatenate([wv, jnp.zeros((n_extra, wv.shape[1]), F32)], axis=0)
    v_ones = np.zeros((MLA_HEADS, MLA_VA, 1), np.float32)
    v_ones[:, MLA_V] = 1.0
    return (wq.T.astype(BF16), wk_full.astype(BF16), wv_full.T.astype(BF16),
            jnp.asarray(v_ones.reshape(-1, 1)))


def _rope_inv_freq():
    inv = 1.0 / (ROPE_BASE ** (jnp.arange(0, MLA_ROPE, 2, dtype=F32) / MLA_ROPE))
    return inv.reshape(MLA_ROPE // 2, 1)


def kernel(x, c, positions, ada_w, ada_b, norm_mix, norm_ffn, w_in_even, mla_q_norm, mla_w_uq, mla_kv_norm, mla_w_ukv, rwkv_mu, rwkv_w0, rwkv_w2, rwkv_a0, rwkv_a2, rwkv_g2, rwkv_k_k, rwkv_k_a, rwkv_r_k, rwkv_ln_w, rwkv_ln_b, w_out_even, w_in_odd, hg_lb_logits, hg_out_norm, w_out_odd, ffn_w_gate, ffn_w_up, ffn_w_down, final_norm):
    batch, seq, d = x.shape
    depth = ada_w.shape[0]
    t = batch * seq
    tm = min(512, seq)
    xt = x.reshape(t, d)

    mod_all = _ada_mod(c, ada_w, ada_b).reshape(depth, batch, 6, d)

    for l in range(depth):
        mod = mod_all[l]
        j = l // 2
        if l % 2 == 0:
            w_in = w_in_even[j]
            w_mla = _pad_cols(w_in[:, :MLA_COLS], MLA_COLS_PAD).astype(BF16)
            w_rwkv = w_in[:, MLA_COLS:].astype(BF16)
            p_mla, p_rwkv = _in_proj(xt, mod, norm_mix[l], [w_mla, w_rwkv],
                                     seq=seq, sh=0, sc=1, tm=tm, out_dtype=BF16)
            wqt, wk, wvt, v_ones = _mla_weights(mla_w_uq[j], mla_w_ukv[j])
            qt, kh, vt = _mla_prep(p_mla, positions, _rope_inv_freq(),
                                   mla_q_norm[j].reshape(1, -1), mla_kv_norm[j].reshape(1, -1),
                                   wqt, wk, wvt, v_ones, batch=batch, seq=seq, tm=tm, tk=ATTN_TILE)
            y_a = _attention(qt, kh, vt, batch=batch, seq=seq, tq=ATTN_TILE, nh=ATTN_HEADS)

            zeros_lo = jnp.zeros((DECAY_LORA, RWKV_DIM), F32)
            w2 = jnp.concatenate([rwkv_w2[j], zeros_lo], axis=0).astype(BF16)
            a2 = jnp.concatenate([zeros_lo, rwkv_a2[j]], axis=0).astype(BF16)
            vecs = jnp.stack([rwkv_w0[j], rwkv_a0[j], rwkv_k_k[j], rwkv_k_a[j],
                              rwkv_r_k[j].reshape(-1), rwkv_ln_w[j], rwkv_ln_b[j],
                              jnp.zeros((RWKV_DIM,), F32)])
            y_b = _rwkv(p_rwkv, rwkv_mu[j].reshape(1, -1), vecs, w2, a2,
                        rwkv_g2[j].astype(BF16), batch=batch, seq=seq,
                        n_chunks=min(RWKV_BLOCK_CHUNKS, seq // RWKV_CHUNK))
            w_out = w_out_even[j].astype(BF16)
            n_a = MLA_HEADS * MLA_V
            ys, w_outs = [y_a, y_b], [w_out[:n_a], w_out[n_a:]]
        else:
            xt = _hgrn_layer(xt, mod, norm_mix[l], w_in_odd[j].astype(BF16), hg_lb_logits,
                             hg_out_norm[j].reshape(1, -1), w_out_odd[j].astype(BF16),
                             batch=batch, seq=seq, layer=l)
            ys, w_outs = [], []
        xt = _ffn(xt, mod, norm_ffn[l], ffn_w_gate[l].astype(BF16), ffn_w_up[l].astype(BF16),
                  ffn_w_down[l].astype(BF16), final_norm, ys, w_outs, seq=seq, tm=tm,
                  final=(l == depth - 1))
    return xt.reshape(batch, seq, d)
```
